```python
import math
import jax, jax.numpy as jnp
from jax import lax
import numpy as np

D_MODEL = 1024
BATCH = 8
SEQ = 2048
DEPTH = 4
DEC_BATCH = 128
DEC_SEQ = 8
PAST_LEN = 16384
PAGE_SIZE = 128

SSD_INNER = 2 * D_MODEL
SSD_HEAD_DIM = 64
SSD_HEADS = SSD_INNER // SSD_HEAD_DIM
SSD_STATE = 128
SSD_GROUPS = 4
SSD_HPG = SSD_HEADS // SSD_GROUPS
CONV_K = 4
CONV_DIM = SSD_INNER + 2 * SSD_GROUPS * SSD_STATE
SSD_CHUNK = 128
GLA_HEADS = 4
GLA_KEY_DIM = D_MODEL // 2
GLA_VAL_DIM = D_MODEL
GLA_DK = GLA_KEY_DIM // GLA_HEADS
GLA_DV = GLA_VAL_DIM // GLA_HEADS
GLA_GATE_RANK = 16
GLA_TAU = 16.0
GLA_CHUNK = 64
N_EXPERTS = 16
N_EXPERT_GROUPS = 4
EXPERTS_PER_GROUP = N_EXPERTS // N_EXPERT_GROUPS
TOP_K = 2
EXPERT_FF = D_MODEL // 2
EPS = 1e-6
IN_SIZES = (SSD_INNER, CONV_DIM, SSD_HEADS, GLA_KEY_DIM, GLA_KEY_DIM, GLA_VAL_DIM, GLA_VAL_DIM,
            GLA_GATE_RANK, D_MODEL, D_MODEL)
IN_DIM = sum(IN_SIZES)
IN_OFFSETS = tuple(int(o) for o in np.cumsum(IN_SIZES)[:-1])

kernel_name = "hybrid_ssd_gla_moe_adaln_step"


def rms_norm(x, w):
    xf = x.astype(jnp.float32)
    y = xf * lax.rsqrt(jnp.mean(xf * xf, axis=-1, keepdims=True) + EPS)
    return (y * w.astype(jnp.float32)).astype(x.dtype)


def chunk_major(a, chunk):
    bt, s = a.shape[:2]
    nc = -(-s // chunk)
    pad = nc * chunk - s
    a = jnp.pad(a, [(0, 0), (0, pad)] + [(0, 0)] * (a.ndim - 2))
    return a.reshape((bt, nc, chunk) + a.shape[2:]).swapaxes(0, 1)


def unchunk(a, s):
    nc, bt, chunk = a.shape[:3]
    return a.swapaxes(0, 1).reshape((bt, nc * chunk) + a.shape[3:])[:, :s]


def causal_conv(u, buf, w, b):
    s = u.shape[1]
    full = jnp.concatenate([buf.astype(u.dtype), u], axis=1)
    out = b + full[:, 0:s] * w[0]
    for i in range(1, CONV_K):
        out = out + full[:, i:i + s] * w[i]
    return out, full[:, -(CONV_K - 1):]


def ssd_chunked(x, dt, a, bm, cm, state0):
    s = x.shape[1]
    chunk = min(SSD_CHUNK, s)
    f32 = jnp.float32
    mask = jnp.tril(jnp.ones((chunk, chunk), dtype=bool))
    xs = (chunk_major(x.astype(f32), chunk), chunk_major(dt.astype(f32), chunk),
          chunk_major(bm.astype(f32), chunk), chunk_major(cm.astype(f32), chunk))

    def body(state, inp):
        xc, dtc, bc, cc = inp
        cs = jnp.cumsum(dtc * a, axis=1)
        seg = cs[:, :, None, :] - cs[:, None, :, :]
        decay = jnp.exp(jnp.where(mask[None, :, :, None], seg, -jnp.inf))
        cb = jnp.repeat(jnp.einsum('bign,bjgn->bijg', cc, bc), SSD_HPG, axis=-1)
        xdt = xc * dtc[..., None]
        y = jnp.einsum('bijh,bjhp->bihp', cb * decay, xdt)
        ch = jnp.repeat(cc, SSD_HPG, axis=2)
        y = y + jnp.einsum('bihn,bhpn->bihp', ch, state) * jnp.exp(cs)[..., None]
        bh = jnp.repeat(bc, SSD_HPG, axis=2) * jnp.exp(cs[:, -1:, :] - cs)[..., None]
        state = state * jnp.exp(cs[:, -1, :])[:, :, None, None] + jnp.einsum('bjhn,bjhp->bhpn', bh, xdt)
        return state, y

    state, ys = lax.scan(body, state0.astype(f32), xs)
    return unchunk(ys, s), state


def gla_chunked(q, k, v, lg, state0):
    s = q.shape[1]
    chunk = min(GLA_CHUNK, s)
    f32 = jnp.float32
    mask = jnp.tril(jnp.ones((chunk, chunk), dtype=bool))
    xs = (chunk_major(q.astype(f32), chunk), chunk_major(k.astype(f32), chunk),
          chunk_major(v.astype(f32), chunk), chunk_major(lg.astype(f32), chunk))

    def body(state, inp):
        qc, kc, vc, gc = inp
        g = jnp.cumsum(gc, axis=1)
        seg = g[:, :, None] - g[:, None, :]
        decay = jnp.exp(jnp.where(mask[None, :, :, None, None], seg, -jnp.inf))
        att = jnp.einsum('bihk,bjhk,bijhk->bijh', qc, kc, decay)
        o = jnp.einsum('bijh,bjhv->bihv', att, vc)
        o = o + jnp.einsum('bihk,bhkv->bihv', qc * jnp.exp(g), state)
        kd = kc * jnp.exp(g[:, -1:] - g)
        state = state * jnp.exp(g[:, -1])[..., None] + jnp.einsum('bjhk,bjhv->bhkv', kd, vc)
        return state, o

    state, os_ = lax.scan(body, state0.astype(f32), xs)
    return unchunk(os_, s), state


def route(h, w_router, router_bias):
    t = h.shape[0]
    scores = jax.nn.sigmoid(h.astype(jnp.float32) @ w_router.astype(jnp.float32))
    biased = scores + router_bias.astype(jnp.float32)
    grp = biased.reshape(t, N_EXPERT_GROUPS, EXPERTS_PER_GROUP)
    grp_score = lax.top_k(grp, TOP_K)[0].sum(-1)
    best = jnp.argmax(grp_score, axis=-1)
    gmask = jnp.repeat(jnp.arange(N_EXPERT_GROUPS)[None, :] == best[:, None], EXPERTS_PER_GROUP, axis=-1)
    _, idx = lax.top_k(jnp.where(gmask, biased, -jnp.inf), TOP_K)
    w = jnp.take_along_axis(scores, idx, axis=-1)
    w = w / jnp.sum(w, axis=-1, keepdims=True)
    return jnp.sum(jax.nn.one_hot(idx, N_EXPERTS, dtype=jnp.float32) * w[..., None], axis=1)


def moe_ffn(h, comb, wg, wu, wd):
    out = jnp.zeros_like(h)
    for e in range(N_EXPERTS):
        ye = (jax.nn.silu(h @ wg[e]) * (h @ wu[e])) @ wd[e]
        out = out + comb[:, e:e + 1].astype(h.dtype) * ye
    return out


def layer_fwd(x, c, conv_buf, ssm_state, gla_state, p, w_router, router_bias):
    bt, s, _ = x.shape
    f32 = jnp.float32
    mod = (jax.nn.silu(c) @ p['w_ada'] + p['b_ada'])[:, None, :]
    sh1, sc1, g1, sh2, sc2, g2 = jnp.split(mod, 6, axis=-1)
    h = rms_norm(x, p['norm1_w']) * (1 + sc1) + sh1
    z, xbc, dt, q, k, v, r, glr, ga, gb = jnp.split(h @ p['w_in'], IN_OFFSETS, axis=-1)
    xbc, conv_new = causal_conv(xbc, conv_buf, p['conv_w'], p['conv_b'])
    xbc = jax.nn.silu(xbc)
    xs, bm, cm = jnp.split(xbc, [SSD_INNER, SSD_INNER + SSD_GROUPS * SSD_STATE], axis=-1)
    xs = xs.reshape(bt, s, SSD_HEADS, SSD_HEAD_DIM)
    bm = bm.reshape(bt, s, SSD_GROUPS, SSD_STATE)
    cm = cm.reshape(bt, s, SSD_GROUPS, SSD_STATE)
    dtp = jax.nn.softplus(dt.astype(f32) + p['dt_bias'].astype(f32))
    a = -jnp.exp(p['a_log'].astype(f32))
    y_ssd, ssm_new = ssd_chunked(xs, dtp, a, bm, cm, ssm_state)
    y_ssd = y_ssd + p['d_skip'].astype(f32)[:, None] * xs.astype(f32)
    y_ssd = y_ssd.reshape(bt, s, SSD_INNER).astype(x.dtype) * jax.nn.silu(z)
    y_ssd = rms_norm(y_ssd.reshape(bt, s, SSD_GROUPS, SSD_INNER // SSD_GROUPS),
                     p['ssd_norm_w'].reshape(SSD_GROUPS, SSD_INNER // SSD_GROUPS))
    branch_a = y_ssd.reshape(bt, s, SSD_INNER) @ p['w_ssd_br']
    q = q.reshape(bt, s, GLA_HEADS, GLA_DK) * (GLA_DK ** -0.5)
    k = k.reshape(bt, s, GLA_HEADS, GLA_DK)
    v = v.reshape(bt, s, GLA_HEADS, GLA_DV)
    lg = jax.nn.log_sigmoid((glr @ p['gla_gate_up'] + p['gla_gate_b']).astype(f32)) / GLA_TAU
    lg = lg.reshape(bt, s, GLA_HEADS, GLA_DK)
    o, gla_new = gla_chunked(q, k, v, lg, gla_state)
    o = rms_norm(o.astype(x.dtype), p['gla_norm_w'])
    o = o.reshape(bt, s, GLA_VAL_DIM) * jax.nn.silu(r)
    branch_b = o @ p['w_gla_br']
    gate_a = jax.nn.sigmoid(ga + p['merge_b'][:D_MODEL])
    gate_b = jax.nn.sigmoid(gb + p['merge_b'][D_MODEL:])
    mixed = (gate_a * branch_a + gate_b * branch_b) @ p['w_out']
    x = x + g1 * mixed
    h2 = (rms_norm(x, p['norm2_w']) * (1 + sc2) + sh2).reshape(bt * s, D_MODEL)
    comb = route(h2, w_router, router_bias)
    y_moe = moe_ffn(h2, comb, p['w_exp_gate'], p['w_exp_up'], p['w_exp_down']).reshape(bt, s, D_MODEL)
    x = x + g2 * y_moe
    return x, conv_new, ssm_new.astype(x.dtype), gla_new.astype(x.dtype)


def setup_inputs(seed: int = 0) -> dict:
    key = jax.random.key(seed)
    ks = iter(jax.random.split(key, 40))
    nrm = lambda shape, scale: jax.random.normal(next(ks), shape, jnp.float32) * scale
    dt0 = jnp.exp(jax.random.uniform(next(ks), (DEPTH, SSD_HEADS), jnp.float32)
                  * (math.log(0.1) - math.log(0.001)) + math.log(0.001))
    return {
        "x_prompt": nrm((BATCH, SEQ, D_MODEL), 1.0),
        "x_sample": nrm((DEC_BATCH, DEC_SEQ, D_MODEL), 1.0),
        "c_prompt": nrm((BATCH, D_MODEL), 1.0),
        "c_sample": nrm((DEC_BATCH, D_MODEL), 1.0),
        "state_conv": nrm((DEPTH, DEC_BATCH, CONV_K - 1, CONV_DIM), 1.0),
        "state_ssm": nrm((DEPTH, DEC_BATCH, SSD_HEADS, SSD_HEAD_DIM, SSD_STATE), 0.1),
        "state_gla": nrm((DEPTH, DEC_BATCH, GLA_HEADS, GLA_DK, GLA_DV), 0.1),
        "w_ada": nrm((DEPTH, D_MODEL, 6 * D_MODEL), 0.3 * D_MODEL ** -0.5),
        "b_ada": nrm((DEPTH, 6 * D_MODEL), 0.02),
        "norm1_w": 1.0 + nrm((DEPTH, D_MODEL), 0.02),
        "w_in": nrm((DEPTH, D_MODEL, IN_DIM), D_MODEL ** -0.5),
        "conv_w": nrm((DEPTH, CONV_K, CONV_DIM), 0.5),
        "conv_b": nrm((DEPTH, CONV_DIM), 0.02),
        "dt_bias": dt0 + jnp.log(-jnp.expm1(-dt0)),
        "a_log": jnp.log(jax.random.uniform(next(ks), (DEPTH, SSD_HEADS), jnp.float32, 1.0, 16.0)),
        "d_skip": 1.0 + nrm((DEPTH, SSD_HEADS), 0.1),
        "ssd_norm_w": 1.0 + nrm((DEPTH, SSD_INNER), 0.02),
        "w_ssd_br": nrm((DEPTH, SSD_INNER, D_MODEL), SSD_INNER ** -0.5),
        "gla_gate_up": nrm((DEPTH, GLA_GATE_RANK, GLA_KEY_DIM), GLA_GATE_RANK ** -0.5),
        "gla_gate_b": nrm((DEPTH, GLA_KEY_DIM), 0.1),
        "gla_norm_w": 1.0 + nrm((DEPTH, GLA_DV), 0.02),
        "w_gla_br": nrm((DEPTH, GLA_VAL_DIM, D_MODEL), GLA_VAL_DIM ** -0.5),
        "merge_b": nrm((DEPTH, 2 * D_MODEL), 0.02),
        "w_out": nrm((DEPTH, D_MODEL, D_MODEL), D_MODEL ** -0.5),
        "norm2_w": 1.0 + nrm((DEPTH, D_MODEL), 0.02),
        "w_router": nrm((D_MODEL, N_EXPERTS), D_MODEL ** -0.5),
        "router_bias": nrm((N_EXPERTS,), 0.01),
        "w_exp_gate": nrm((DEPTH, N_EXPERTS, D_MODEL, EXPERT_FF), D_MODEL ** -0.5),
        "w_exp_up": nrm((DEPTH, N_EXPERTS, D_MODEL, EXPERT_FF), D_MODEL ** -0.5),
        "w_exp_down": nrm((DEPTH, N_EXPERTS, EXPERT_FF, D_MODEL), EXPERT_FF ** -0.5),
        "final_norm_w": 1.0 + nrm((D_MODEL,), 0.02),
    }


def reference(x_prompt, x_sample, c_prompt, c_sample, state_conv, state_ssm, state_gla,
              w_ada, b_ada, norm1_w, w_in, conv_w, conv_b, dt_bias, a_log, d_skip, ssd_norm_w,
              w_ssd_br, gla_gate_up, gla_gate_b, gla_norm_w, w_gla_br, merge_b, w_out, norm2_w,
              w_router, router_bias, w_exp_gate, w_exp_up, w_exp_down, final_norm_w):
    bp = x_prompt.shape[0]
    dt_ = x_prompt.dtype
    xp, xs = x_prompt, x_sample
    conv_p, ssm_p, gla_p, conv_s, ssm_s, gla_s = [], [], [], [], [], []
    for l in range(DEPTH):
        p = dict(w_ada=w_ada[l], b_ada=b_ada[l], norm1_w=norm1_w[l], w_in=w_in[l], conv_w=conv_w[l],
                 conv_b=conv_b[l], dt_bias=dt_bias[l], a_log=a_log[l], d_skip=d_skip[l],
                 ssd_norm_w=ssd_norm_w[l], w_ssd_br=w_ssd_br[l], gla_gate_up=gla_gate_up[l],
                 gla_gate_b=gla_gate_b[l], gla_norm_w=gla_norm_w[l], w_gla_br=w_gla_br[l],
                 merge_b=merge_b[l], w_out=w_out[l], norm2_w=norm2_w[l], w_exp_gate=w_exp_gate[l],
                 w_exp_up=w_exp_up[l], w_exp_down=w_exp_down[l])
        zc = jnp.zeros((bp, CONV_K - 1, CONV_DIM), dt_)
        zs = jnp.zeros((bp, SSD_HEADS, SSD_HEAD_DIM, SSD_STATE), dt_)
        zg = jnp.zeros((bp, GLA_HEADS, GLA_DK, GLA_DV), dt_)
        xp, cp, sp, gp = layer_fwd(xp, c_prompt, zc, zs, zg, p, w_router, router_bias)
        xs, cs_, ss, gs = layer_fwd(xs, c_sample, state_conv[l], state_ssm[l], state_gla[l], p, w_router, router_bias)
        conv_p.append(cp); ssm_p.append(sp); gla_p.append(gp)
        conv_s.append(cs_); ssm_s.append(ss); gla_s.append(gs)
    y_prompt = rms_norm(xp, final_norm_w)
    y_sample = rms_norm(xs, final_norm_w)
    return (y_prompt, y_sample, jnp.stack(conv_p), jnp.stack(ssm_p), jnp.stack(gla_p),
            jnp.stack(conv_s), jnp.stack(ssm_s), jnp.stack(gla_s))
```

```python
import functools

import jax
import jax.numpy as jnp
from jax import lax
from jax.experimental import pallas as pl
from jax.experimental.pallas import tpu as pltpu

f32 = jnp.float32
bf16 = jnp.bfloat16

D = 1024
DEPTH = 4
SSD_INNER = 2048
SSD_HD = 64
SSD_NH = 32
SSD_NS = 128
SSD_NG = 4
SSD_HPG = 8
SSD_GW = SSD_INNER // SSD_NG
CONV_K = 4
CONV_DIM = SSD_INNER + 2 * SSD_NG * SSD_NS
GLA_NH = 4
GLA_DK = 128
GLA_DV = 256
GLA_KD = GLA_NH * GLA_DK
GLA_VD = GLA_NH * GLA_DV
GLA_RANK = 16
GLA_TAU = 16.0
N_EXP = 16
N_EGRP = 4
EXP_PER_GRP = 4
EXP_FF = 512
EPS = 1e-6
LANES = 128
TAIL_GLR0 = SSD_HPG
MAIN_W = SSD_INNER + CONV_DIM + 2 * GLA_KD + 2 * GLA_VD + 2 * D
VMEM_LIMIT = 52 * 1024 * 1024


def _sigmoid(x):
    return 1.0 / (1.0 + jnp.exp(-x))


def _silu(x):
    return x * _sigmoid(x)


def _softplus(x):
    return jnp.maximum(x, 0.0) + jnp.log1p(jnp.exp(-jnp.abs(x)))


def _log_sigmoid(x):
    return jnp.minimum(x, 0.0) - jnp.log1p(jnp.exp(-jnp.abs(x)))


def _dot(a, b):
    return jnp.dot(a, b, preferred_element_type=f32)


def _dot_nt(a, b):
    return lax.dot_general(a, b, (((1,), (1,)), ((), ())), preferred_element_type=f32)


def _dot_tn(a, b):
    return lax.dot_general(a, b, (((0,), (0,)), ((), ())), preferred_element_type=f32)


def _split3(x):
    hi = x.astype(bf16)
    r1 = x - hi.astype(f32)
    mid = r1.astype(bf16)
    lo = (r1 - mid.astype(f32)).astype(bf16)
    return hi, mid, lo


def _cumsum_rows(x, tri):
    hi, mid, lo = _split3(x)
    return _dot(tri, hi) + _dot(tri, mid) + _dot(tri, lo)


def _tri(n):
    r = lax.broadcasted_iota(jnp.int32, (n, n), 0)
    c = lax.broadcasted_iota(jnp.int32, (n, n), 1)
    return r >= c


def _norm_mod(x, mod_ref, w_ref, i_shift, i_scale, seq3d):
    ms = jnp.mean(x * x, axis=-1, keepdims=True)
    y = x * lax.rsqrt(ms + EPS) * w_ref[...]
    if seq3d:
        sc = mod_ref[:, i_scale:i_scale + 1, :]
        sh = mod_ref[:, i_shift:i_shift + 1, :]
        h = y * (1.0 + sc) + sh
        return h.reshape(h.shape[0] * h.shape[1], h.shape[2])
    sc = mod_ref[i_scale:i_scale + 1, :]
    sh = mod_ref[i_shift:i_shift + 1, :]
    return y * (1.0 + sc) + sh


def _adaln_kernel(c_ref, w_ref, b_ref, o_ref):
    s = _silu(c_ref[...]).astype(bf16)
    o_ref[...] = _dot(s, w_ref[...].astype(bf16)) + b_ref[...]


def _adaln(c_all, w_ada, b_ada):
    n = c_all.shape[0]
    tn = 1024
    return pl.pallas_call(
        _adaln_kernel,
        grid=(DEPTH, 6 * D // tn),
        in_specs=[
            pl.BlockSpec((n, D), lambda l, j: (0, 0)),
            pl.BlockSpec((None, D, tn), lambda l, j: (l, 0, j)),
            pl.BlockSpec((None, 1, tn), lambda l, j: (l, 0, j)),
        ],
        out_specs=pl.BlockSpec((None, n, tn), lambda l, j: (l, 0, j)),
        out_shape=jax.ShapeDtypeStruct((DEPTH, n, 6 * D), f32),
        compiler_params=pltpu.CompilerParams(
            dimension_semantics=("arbitrary", "arbitrary"), vmem_limit_bytes=VMEM_LIMIT),
        name="adaln",
    )(c_all, w_ada, b_ada.reshape(DEPTH, 1, 6 * D))


IN_TN = 1024
IN_NJ = MAIN_W // IN_TN


def _inproj_kernel(x_ref, mod_ref, n1_ref, wm_ref, wt_ref,
                   z_ref, xbc_ref, gla_ref, gate_ref, tail_ref, h_scr, *, seq3d):
    j = pl.program_id(1)

    @pl.when(j == 0)
    def _():
        h = _norm_mod(x_ref[...], mod_ref, n1_ref, 0, 1, seq3d).astype(bf16)
        h_scr[...] = h
        t = _dot(h, wt_ref[...])
        for g in range(SSD_NG):
            tail_ref[g] = t[:, LANES * g:LANES * (g + 1)]

    acc = _dot(h_scr[...], wm_ref[...])

    @pl.when(j < 2)
    def _():
        z_ref[...] = acc.astype(z_ref.dtype)

    @pl.when(jnp.logical_and(j >= 2, j < 5))
    def _():
        xbc_ref[...] = acc.astype(xbc_ref.dtype)

    @pl.when(jnp.logical_and(j >= 5, j < 8))
    def _():
        gla_ref[...] = acc.astype(gla_ref.dtype)

    @pl.when(j >= 8)
    def _():
        gate_ref[...] = acc.astype(gate_ref.dtype)


def _seq_tile(nseq, want):
    return want if nseq % want == 0 else nseq


def _inproj(x, mod, n1w, w_main, w_tail, *, seq3d, act_dtype):
    if seq3d:
        ns = _seq_tile(x.shape[0], 64)
        ntok = x.shape[0] * x.shape[1]
        tm = ns * x.shape[1]
        nt = x.shape[0] // ns
        x_spec = pl.BlockSpec((ns, x.shape[1], D), lambda i, j: (i, 0, 0))
        mod_spec = pl.BlockSpec((ns, 6, D), lambda i, j: (i, 0, 0))
    else:
        ntok = x.shape[0]
        tm = min(1024, ntok // mod.shape[0])
        nt = ntok // tm
        per_seq = ntok // mod.shape[0] // tm
        x_spec = pl.BlockSpec((tm, D), lambda i, j: (i, 0))
        mod_spec = pl.BlockSpec((None, 6, D), lambda i, j: (i // per_seq, 0, 0))

    def col(lo, hi):
        return lambda i, j: (i, jnp.clip(j - lo, 0, hi - lo))

    out_shapes = (
        jax.ShapeDtypeStruct((ntok, SSD_INNER), act_dtype),
        jax.ShapeDtypeStruct((ntok, CONV_DIM), act_dtype),
        jax.ShapeDtypeStruct((ntok, 2 * GLA_KD + 2 * GLA_VD), act_dtype),
        jax.ShapeDtypeStruct((ntok, 2 * D), act_dtype),
        jax.ShapeDtypeStruct((SSD_NG, ntok, LANES), f32),
    )
    return pl.pallas_call(
        functools.partial(_inproj_kernel, seq3d=seq3d),
        grid=(nt, IN_NJ),
        in_specs=[
            x_spec,
            mod_spec,
            pl.BlockSpec((1, D), lambda i, j: (0, 0)),
            pl.BlockSpec((D, IN_TN), lambda i, j: (0, j)),
            pl.BlockSpec((D, SSD_NG * LANES), lambda i, j: (0, 0)),
        ],
        out_specs=(
            pl.BlockSpec((tm, IN_TN), col(0, 1)),
            pl.BlockSpec((tm, IN_TN), col(2, 4)),
            pl.BlockSpec((tm, IN_TN), col(5, 7)),
            pl.BlockSpec((tm, IN_TN), col(8, 9)),
            pl.BlockSpec((SSD_NG, tm, LANES), lambda i, j: (0, i, 0)),
        ),
        out_shape=out_shapes,
        scratch_shapes=[pltpu.VMEM((tm, D), bf16)],
        compiler_params=pltpu.CompilerParams(
            dimension_semantics=("arbitrary", "arbitrary"), vmem_limit_bytes=VMEM_LIMIT),
        name="inproj",
    )(x, mod, n1w, w_main, w_tail)


CONV_PAD = 8


def _ssd_kernel(*refs, L, sb, nchunks, has_state):
    if has_state:
        (xbc_ref, tail_ref, z_ref, conv0_ref, ssm0_ref, cw_ref, cb_ref, dtb_ref, alog_ref, dsk_ref,
         nw_ref, wbr_ref, br_ref, convn_ref, ssmn_ref, ubuf, xs_scr, b_scr, c_scr, y_scr) = refs
    else:
        (xbc_ref, tail_ref, z_ref, cw_ref, cb_ref, dtb_ref, alog_ref, dsk_ref,
         nw_ref, wbr_ref, br_ref, convn_ref, ssmn_ref, ubuf, xs_scr, b_scr, c_scr, y_scr) = refs
        conv0_ref = ssm0_ref = None
    c = pl.program_id(1)
    low = _tri(L)
    tri = low.astype(bf16)
    kprev = CONV_K - 1

    def seq_body(s, carry):
        r0 = pl.multiple_of(s * L, 8)
        rows = pl.ds(r0, L)

        def init():
            if has_state:
                ubuf[CONV_PAD - kprev:CONV_PAD, :] = conv0_ref[s]
                ssmn_ref[s] = ssm0_ref[s]
            else:
                ubuf[CONV_PAD - kprev:CONV_PAD, :] = jnp.zeros((kprev, CONV_DIM), f32)
                ssmn_ref[s] = jnp.zeros(ssmn_ref.shape[1:], f32)

        if nchunks == 1:
            init()
        else:
            pl.when(c == 0)(init)

        u = xbc_ref[rows, :].astype(f32)
        ubuf[CONV_PAD:CONV_PAD + L, :] = u
        conv = cb_ref[...] + u * cw_ref[kprev:kprev + 1, :]
        for i in range(kprev):
            conv = conv + ubuf[CONV_PAD - kprev + i:CONV_PAD - kprev + i + L, :] * cw_ref[i:i + 1, :]
        new_tail = ubuf[CONV_PAD + L - kprev:CONV_PAD + L, :]
        convn_ref[s] = new_tail
        ubuf[CONV_PAD - kprev:CONV_PAD, :] = new_tail
        xc = _silu(conv)
        for g in range(SSD_NG):
            xs_scr[g] = xc[:, SSD_GW * g:SSD_GW * (g + 1)]
            b_scr[g] = xc[:, SSD_INNER + SSD_NS * g:SSD_INNER + SSD_NS * (g + 1)]
            c_scr[g] = xc[:, SSD_INNER + SSD_NG * SSD_NS + SSD_NS * g:
                          SSD_INNER + SSD_NG * SSD_NS + SSD_NS * (g + 1)]

        def group_body(g, carry2):
            tl = tail_ref[g, rows, :]
            dtp = _softplus(tl + dtb_ref[g])
            a = -jnp.exp(alog_ref[g])
            cs = _cumsum_rows(dtp * a, tri)
            cs_t = cs.T
            cs_last = cs[L - 1:L, :]
            e_last = jnp.exp(cs_last)
            bg = b_scr[g].astype(bf16)
            cg = c_scr[g].astype(bf16)
            cb = _dot_nt(cg, bg)
            xs = xs_scr[g]
            dsk = dsk_ref[g]
            for j in range(SSD_HPG):
                hs = slice(SSD_HD * j, SSD_HD * (j + 1))
                col = cs[:, j:j + 1]
                seg = col - cs_t[j:j + 1, :]
                dec = jnp.exp(jnp.where(low, seg, -jnp.inf))
                w = (cb * dec).astype(bf16)
                xh = xs[:, hs]
                xdt = xh * dtp[:, j:j + 1]
                y = _dot(w, xdt.astype(bf16))
                st = ssmn_ref[s, g, hs, :]
                y = y + _dot_nt(cg, st.astype(bf16)) * jnp.exp(col)
                sc = jnp.exp(cs_last[:, j:j + 1] - col)
                upd = _dot_tn((xdt * sc).astype(bf16), bg)
                ssmn_ref[s, g, hs, :] = st * e_last[:, j:j + 1] + upd
                y_scr[g, :, hs] = y + dsk[:, hs] * xh
            return carry2

        lax.fori_loop(0, SSD_NG, group_body, 0)

        acc = jnp.zeros((L, D), f32)
        for g in range(SSD_NG):
            gs = slice(SSD_GW * g, SSD_GW * (g + 1))
            zg = z_ref[rows, gs].astype(f32)
            yg = y_scr[g] * _silu(zg)
            yn = yg * lax.rsqrt(jnp.mean(yg * yg, axis=-1, keepdims=True) + EPS) * nw_ref[:, gs]
            acc = acc + _dot(yn.astype(bf16), wbr_ref[gs, :])
        br_ref[rows, :] = acc
        return carry

    lax.fori_loop(0, sb, seq_body, 0)


def _ssd(xbc, tail, z, conv0, ssm0, wts, *, nseq, L, sb, nchunks):
    has_state = conv0 is not None
    cw, cb, dtb, alog, dsk, nw, wbr = wts
    ntok = xbc.shape[0]
    rb = sb * L
    if nchunks == 1:
        row = lambda b, c: (b, 0)
        row3 = lambda b, c: (0, b, 0)
    else:
        row = lambda b, c: (b * nchunks + c, 0)
        row3 = lambda b, c: (0, b * nchunks + c, 0)
    const2 = lambda b, c: (0, 0)
    const3 = lambda b, c: (0, 0, 0)
    in_specs = [
        pl.BlockSpec((rb, CONV_DIM), row),
        pl.BlockSpec((SSD_NG, rb, LANES), row3),
        pl.BlockSpec((rb, SSD_INNER), row),
    ]
    args = [xbc, tail, z]
    if has_state:
        in_specs += [
            pl.BlockSpec((sb, CONV_K - 1, CONV_DIM), lambda b, c: (b, 0, 0)),
            pl.BlockSpec((sb, SSD_NG, SSD_GW, SSD_NS), lambda b, c: (b, 0, 0, 0)),
        ]
        args += [conv0, ssm0]
    in_specs += [
        pl.BlockSpec((CONV_K, CONV_DIM), const2),
        pl.BlockSpec((1, CONV_DIM), const2),
        pl.BlockSpec((SSD_NG, 1, LANES), const3),
        pl.BlockSpec((SSD_NG, 1, LANES), const3),
        pl.BlockSpec((SSD_NG, 1, SSD_GW), const3),
        pl.BlockSpec((1, SSD_INNER), const2),
        pl.BlockSpec((SSD_INNER, D), const2),
    ]
    args += [cw, cb, dtb, alog, dsk, nw, wbr]
    return pl.pallas_call(
        functools.partial(_ssd_kernel, L=L, sb=sb, nchunks=nchunks, has_state=has_state),
        grid=(nseq // sb, nchunks),
        in_specs=in_specs,
        out_specs=(
            pl.BlockSpec((rb, D), row),
            pl.BlockSpec((sb, CONV_K - 1, CONV_DIM), lambda b, c: (b, 0, 0)),
            pl.BlockSpec((sb, SSD_NG, SSD_GW, SSD_NS), lambda b, c: (b, 0, 0, 0)),
        ),
        out_shape=(
            jax.ShapeDtypeStruct((ntok, D), f32),
            jax.ShapeDtypeStruct((nseq, CONV_K - 1, CONV_DIM), f32),
            jax.ShapeDtypeStruct((nseq, SSD_NG, SSD_GW, SSD_NS), f32),
        ),
        scratch_shapes=[
            pltpu.VMEM((CONV_PAD + L, CONV_DIM), f32),
            pltpu.VMEM((SSD_NG, L, SSD_GW), f32),
            pltpu.VMEM((SSD_NG, L, SSD_NS), f32),
            pltpu.VMEM((SSD_NG, L, SSD_NS), f32),
            pltpu.VMEM((SSD_NG, L, SSD_GW), f32),
        ],
        compiler_params=pltpu.CompilerParams(
            dimension_semantics=("arbitrary", "arbitrary"), vmem_limit_bytes=VMEM_LIMIT),
        name="ssd",
    )(*args)


GLA_PAD = 16


def _gla_kernel(*refs, L, C, sb, nchunks, has_state):
    if has_state:
        (qkvr_ref, tail_ref, gla0_ref, wg_ref, gb_ref, gnw_ref, wbr_ref,
         br_ref, glan_ref, kpad, apad) = refs
    else:
        (qkvr_ref, tail_ref, wg_ref, gb_ref, gnw_ref, wbr_ref,
         br_ref, glan_ref, kpad, apad) = refs
        gla0_ref = None
    c = pl.program_id(1)
    nb = L // C
    tri = _tri(L).astype(bf16)
    rowi = lax.broadcasted_iota(jnp.int32, (L, L), 0)
    coli = lax.broadcasted_iota(jnp.int32, (L, L), 1)
    same_blk_off = rowi & (C - 1)
    rowl = lax.broadcasted_iota(jnp.int32, (L, 1), 0)

    kpad[0:GLA_PAD, :] = jnp.zeros((GLA_PAD, GLA_KD), f32)
    apad[0:GLA_PAD, :] = jnp.zeros((GLA_PAD, GLA_KD), f32)

    def seq_body(s, carry):
        r0 = pl.multiple_of(s * L, 8)
        rows = pl.ds(r0, L)

        def init():
            if has_state:
                glan_ref[s] = gla0_ref[s]
            else:
                glan_ref[s] = jnp.zeros(glan_ref.shape[1:], f32)

        if nchunks == 1:
            init()
        else:
            pl.when(c == 0)(init)

        pre = _dot(tail_ref[rows, :].astype(bf16), wg_ref[...]) + gb_ref[...]
        lg = _log_sigmoid(pre) * (1.0 / GLA_TAU)
        g = _cumsum_rows(lg, tri)
        g_last = g[L - 1:L, :]
        q = qkvr_ref[rows, 0:GLA_KD].astype(f32) * (GLA_DK ** -0.5)
        k = qkvr_ref[rows, GLA_KD:2 * GLA_KD].astype(f32)
        kpad[GLA_PAD:GLA_PAD + L, :] = k
        apad[GLA_PAD:GLA_PAD + L, :] = jnp.exp(lg)

        acc = jnp.zeros((L, D), f32)
        for h in range(GLA_NH):
            kl = slice(GLA_DK * h, GLA_DK * (h + 1))
            vl = slice(2 * GLA_KD + GLA_DV * h, 2 * GLA_KD + GLA_DV * (h + 1))
            rl = slice(2 * GLA_KD + GLA_VD + GLA_DV * h, 2 * GLA_KD + GLA_VD + GLA_DV * (h + 1))
            gh = g[:, kl]
            qh = q[:, kl]
            kh = k[:, kl]
            vh = qkvr_ref[rows, vl].astype(bf16)
            st = glan_ref[s, h]
            o = _dot((qh * jnp.exp(gh)).astype(bf16), st.astype(bf16))

            gbnd = [jnp.zeros((1, GLA_DK), f32)] + [gh[C * i - 1:C * i, :] for i in range(1, nb)]
            gblk = jnp.concatenate([jnp.broadcast_to(b, (C, GLA_DK)) for b in gbnd], axis=0)
            qt = (qh * jnp.exp(gh - gblk)).astype(bf16)
            blocks = [jnp.zeros((C, L), f32)]
            for i in range(1, nb):
                kt = kh * jnp.exp(jnp.where(rowl < C * i, gbnd[i] - gh, -jnp.inf))
                blocks.append(_dot_nt(qt[C * i:C * (i + 1), :], kt.astype(bf16)))
            att = jnp.concatenate(blocks, axis=0) if nb > 1 else blocks[0]

            e = None
            for d in range(C):
                ks = kpad[GLA_PAD - d:GLA_PAD - d + L, kl]
                if d == 0:
                    t = qh * ks
                else:
                    a_sh = apad[GLA_PAD - (d - 1):GLA_PAD - (d - 1) + L, kl]
                    e = a_sh if e is None else e * a_sh
                    t = qh * e * ks
                band = jnp.sum(t, axis=-1, keepdims=True)
                hit = jnp.logical_and(same_blk_off >= d, coli == rowi - d)
                att = att + jnp.where(hit, band, 0.0)
            o = o + _dot(att.astype(bf16), vh)

            kd = (kh * jnp.exp(g_last[:, kl] - gh)).astype(bf16)
            e_col = jnp.exp(jnp.broadcast_to(g_last[:, kl], (8, GLA_DK))).T[:, 0:1]
            glan_ref[s, h] = st * e_col + _dot_tn(kd, vh)

            on = o * lax.rsqrt(jnp.mean(o * o, axis=-1, keepdims=True) + EPS) * gnw_ref[...]
            og = on * _silu(qkvr_ref[rows, rl].astype(f32))
            acc = acc + _dot(og.astype(bf16), wbr_ref[GLA_DV * h:GLA_DV * (h + 1), :])
        br_ref[rows, :] = acc
        return carry

    lax.fori_loop(0, sb, seq_body, 0)


def _gla(qkvr, tail, gla0, wts, *, nseq, L, C, sb, nchunks):
    has_state = gla0 is not None
    wg, gb, gnw, wbr = wts
    ntok = qkvr.shape[0]
    rb = sb * L
    if nchunks == 1:
        row = lambda b, c: (b, 0)
        row3 = lambda b, c: (0, b, 0)
    else:
        row = lambda b, c: (b * nchunks + c, 0)
        row3 = lambda b, c: (0, b * nchunks + c, 0)
    const2 = lambda b, c: (0, 0)
    in_specs = [
        pl.BlockSpec((rb, 2 * GLA_KD + 2 * GLA_VD), row),
        pl.BlockSpec((None, rb, LANES), row3),
    ]
    args = [qkvr, tail]
    if has_state:
        in_specs.append(pl.BlockSpec((sb, GLA_NH, GLA_DK, GLA_DV), lambda b, c: (b, 0, 0, 0)))
        args.append(gla0)
    in_specs += [
        pl.BlockSpec((LANES, GLA_KD), const2),
        pl.BlockSpec((1, GLA_KD), const2),
        pl.BlockSpec((1, GLA_DV), const2),
        pl.BlockSpec((GLA_VD, D), const2),
    ]
    args += [wg, gb, gnw, wbr]
    return pl.pallas_call(
        functools.partial(_gla_kernel, L=L, C=C, sb=sb, nchunks=nchunks, has_state=has_state),
        grid=(nseq // sb, nchunks),
        in_specs=in_specs,
        out_specs=(
            pl.BlockSpec((rb, D), row),
            pl.BlockSpec((sb, GLA_NH, GLA_DK, GLA_DV), lambda b, c: (b, 0, 0, 0)),
        ),
        out_shape=(
            jax.ShapeDtypeStruct((ntok, D), f32),
            jax.ShapeDtypeStruct((nseq, GLA_NH, GLA_DK, GLA_DV), f32),
        ),
        scratch_shapes=[
            pltpu.VMEM((GLA_PAD + L, GLA_KD), f32),
            pltpu.VMEM((GLA_PAD + L, GLA_KD), f32),
        ],
        compiler_params=pltpu.CompilerParams(
            dimension_semantics=("arbitrary", "arbitrary"), vmem_limit_bytes=VMEM_LIMIT),
        name="gla",
    )(*args)


def _top2_sum(a, b, c, d):
    hi1, lo1 = jnp.maximum(a, b), jnp.minimum(a, b)
    hi2, lo2 = jnp.maximum(c, d), jnp.minimum(c, d)
    return jnp.maximum(hi1, hi2) + jnp.maximum(jnp.minimum(hi1, hi2), jnp.maximum(lo1, lo2))


def _route_rows(sig, biased):
    gsc = [_top2_sum(*biased[EXP_PER_GRP * g:EXP_PER_GRP * (g + 1)]) for g in range(N_EGRP)]
    best = jnp.zeros_like(gsc[0], dtype=jnp.int32)
    m = gsc[0]
    for g in range(1, N_EGRP):
        better = gsc[g] > m
        best = jnp.where(better, g, best)
        m = jnp.where(better, gsc[g], m)
    masked = [jnp.where(best == (e // EXP_PER_GRP), biased[e], -jnp.inf) for e in range(N_EXP)]

    def first_argmax(vals):
        idx = jnp.zeros_like(best)
        mx = vals[0]
        for e in range(1, N_EXP):
            better = vals[e] > mx
            idx = jnp.where(better, e, idx)
            mx = jnp.where(better, vals[e], mx)
        return idx

    i1 = first_argmax(masked)
    i2 = first_argmax([jnp.where(i1 == e, -jnp.inf, masked[e]) for e in range(N_EXP)])
    w1 = sum(jnp.where(i1 == e, sig[e], 0.0) for e in range(N_EXP))
    w2 = sum(jnp.where(i2 == e, sig[e], 0.0) for e in range(N_EXP))
    den = w1 + w2
    w1 = w1 / den
    w2 = w2 / den
    return [jnp.where(i1 == e, w1, 0.0) + jnp.where(i2 == e, w2, 0.0) for e in range(N_EXP)]


def _merge_kernel(x_ref, bra_ref, brb_ref, gate_ref, mod_ref, mb_ref, wo_ref, n2_ref,
                  wr_hi_ref, wr_lo_ref, rb_ref, x1_ref, h2_ref, comb_ref, *, seq3d):
    ga = gate_ref[:, 0:D].astype(f32) + mb_ref[:, 0:D]
    gb = gate_ref[:, D:2 * D].astype(f32) + mb_ref[:, D:2 * D]
    mixed_in = _sigmoid(ga) * bra_ref[...] + _sigmoid(gb) * brb_ref[...]
    mixed = _dot(mixed_in.astype(bf16), wo_ref[...])
    x = x_ref[...]
    if seq3d:
        x1 = x + mod_ref[:, 2:3, :] * mixed.reshape(x.shape)
    else:
        x1 = x + mod_ref[2:3, :] * mixed
    x1_ref[...] = x1
    h2 = _norm_mod(x1, mod_ref, n2_ref, 3, 4, seq3d)
    h2_ref[...] = h2.astype(bf16)
    hi = h2.astype(bf16)
    lo = (h2 - hi.astype(f32)).astype(bf16)
    logits = _dot(hi, wr_hi_ref[...]) + (_dot(hi, wr_lo_ref[...]) + _dot(lo, wr_hi_ref[...]))
    lt = logits.T
    sig_all = _sigmoid(lt[0:N_EXP, :])
    bias_all = sig_all + rb_ref[...]
    sig = [sig_all[e:e + 1, :] for e in range(N_EXP)]
    biased = [bias_all[e:e + 1, :] for e in range(N_EXP)]
    comb = _route_rows(sig, biased)
    tm = logits.shape[0]
    comb_t = jnp.concatenate(comb + [jnp.zeros((LANES - N_EXP, tm), f32)], axis=0)
    comb_ref[...] = comb_t.T


def _merge(x, bra, brb, gate, mod, mb, wo, n2w, wr_hi, wr_lo, rb, *, seq3d):
    if seq3d:
        ns = _seq_tile(x.shape[0], 32)
        ntok = x.shape[0] * x.shape[1]
        tm = ns * x.shape[1]
        nt = x.shape[0] // ns
        x_spec = pl.BlockSpec((ns, x.shape[1], D), lambda i: (i, 0, 0))
        mod_spec = pl.BlockSpec((ns, 6, D), lambda i: (i, 0, 0))
    else:
        ntok = x.shape[0]
        tm = min(512, ntok // mod.shape[0])
        nt = ntok // tm
        per_seq = ntok // mod.shape[0] // tm
        x_spec = pl.BlockSpec((tm, D), lambda i: (i, 0))
        mod_spec = pl.BlockSpec((None, 6, D), lambda i: (i // per_seq, 0, 0))
    row = lambda i: (i, 0)
    const = lambda i: (0, 0)
    return pl.pallas_call(
        functools.partial(_merge_kernel, seq3d=seq3d),
        grid=(nt,),
        in_specs=[
            x_spec,
            pl.BlockSpec((tm, D), row),
            pl.BlockSpec((tm, D), row),
            pl.BlockSpec((tm, 2 * D), row),
            mod_spec,
            pl.BlockSpec((1, 2 * D), const),
            pl.BlockSpec((D, D), const),
            pl.BlockSpec((1, D), const),
            pl.BlockSpec((D, LANES), const),
            pl.BlockSpec((D, LANES), const),
            pl.BlockSpec((N_EXP, 1), const),
        ],
        out_specs=(
            x_spec,
            pl.BlockSpec((tm, D), row),
            pl.BlockSpec((tm, LANES), row),
        ),
        out_shape=(
            jax.ShapeDtypeStruct(x.shape, f32),
            jax.ShapeDtypeStruct((ntok, D), bf16),
            jax.ShapeDtypeStruct((ntok, LANES), f32),
        ),
        compiler_params=pltpu.CompilerParams(
            dimension_semantics=("arbitrary",), vmem_limit_bytes=VMEM_LIMIT),
        name="merge",
    )(x, bra, brb, gate, mod, mb, wo, n2w, wr_hi, wr_lo, rb)


def _moe_kernel(h2_ref, comb_ref, wg_ref, wu_ref, wd_ref, x1_ref, mod_ref, x2_ref, acc_scr, *, seq3d):
    e = pl.program_id(1)

    @pl.when(e == 0)
    def _():
        acc_scr[...] = jnp.zeros_like(acc_scr)

    h = h2_ref[...]
    a = _dot(h, wg_ref[...])
    b = _dot(h, wu_ref[...])
    t = (_silu(a) * b).astype(bf16)
    ye = _dot(t, wd_ref[...])
    lane = lax.broadcasted_iota(jnp.int32, comb_ref.shape, 1)
    w = jnp.sum(jnp.where(lane == e, comb_ref[...], 0.0), axis=-1, keepdims=True)
    acc_scr[...] += w * ye

    @pl.when(e == N_EXP - 1)
    def _():
        x1 = x1_ref[...]
        if seq3d:
            x2_ref[...] = x1 + mod_ref[:, 5:6, :] * acc_scr[...].reshape(x1.shape)
        else:
            x2_ref[...] = x1 + mod_ref[5:6, :] * acc_scr[...]


def _moe(h2, comb, wg, wu, wd, x1, mod, *, seq3d):
    if seq3d:
        ns = x1.shape[0]
        ntok = ns * x1.shape[1]
        tm = ntok
        nt = 1
        x_spec = pl.BlockSpec((ns, x1.shape[1], D), lambda i, e: (0, 0, 0))
        mod_spec = pl.BlockSpec((ns, 6, D), lambda i, e: (0, 0, 0))
    else:
        ntok = x1.shape[0]
        tm = min(1024, ntok // mod.shape[0])
        nt = ntok // tm
        per_seq = ntok // mod.shape[0] // tm
        x_spec = pl.BlockSpec((tm, D), lambda i, e: (i, 0))
        mod_spec = pl.BlockSpec((None, 6, D), lambda i, e: (i // per_seq, 0, 0))
    row = lambda i, e: (i, 0)
    return pl.pallas_call(
        functools.partial(_moe_kernel, seq3d=seq3d),
        grid=(nt, N_EXP),
        in_specs=[
            pl.BlockSpec((tm, D), row),
            pl.BlockSpec((tm, LANES), row),
            pl.BlockSpec((None, D, EXP_FF), lambda i, e: (e, 0, 0)),
            pl.BlockSpec((None, D, EXP_FF), lambda i, e: (e, 0, 0)),
            pl.BlockSpec((None, EXP_FF, D), lambda i, e: (e, 0, 0)),
            x_spec,
            mod_spec,
        ],
        out_specs=x_spec,
        out_shape=jax.ShapeDtypeStruct(x1.shape, f32),
        scratch_shapes=[pltpu.VMEM((tm, D), f32)],
        compiler_params=pltpu.CompilerParams(
            dimension_semantics=("arbitrary", "arbitrary"), vmem_limit_bytes=VMEM_LIMIT),
        name="moe",
    )(h2, comb, wg, wu, wd, x1, mod)


def _final_norm_kernel(x_ref, w_ref, o_ref):
    x = x_ref[...]
    o_ref[...] = x * lax.rsqrt(jnp.mean(x * x, axis=-1, keepdims=True) + EPS) * w_ref[...]


def _final_norm(x2d, w):
    ntok = x2d.shape[0]
    tm = min(1024, ntok)
    return pl.pallas_call(
        _final_norm_kernel,
        grid=(ntok // tm,),
        in_specs=[pl.BlockSpec((tm, D), lambda i: (i, 0)), pl.BlockSpec((1, D), lambda i: (0, 0))],
        out_specs=pl.BlockSpec((tm, D), lambda i: (i, 0)),
        out_shape=jax.ShapeDtypeStruct((ntok, D), f32),
        compiler_params=pltpu.CompilerParams(dimension_semantics=("arbitrary",)),
        name="final_norm",
    )(x2d, w)


def _prep_in_weights(w_in):
    o = 0
    z = w_in[:, :, o:o + SSD_INNER]; o += SSD_INNER
    xbc = w_in[:, :, o:o + CONV_DIM]; o += CONV_DIM
    dt = w_in[:, :, o:o + SSD_NH]; o += SSD_NH
    qkvr = w_in[:, :, o:o + 2 * GLA_KD + 2 * GLA_VD]; o += 2 * GLA_KD + 2 * GLA_VD
    glr = w_in[:, :, o:o + GLA_RANK]; o += GLA_RANK
    gates = w_in[:, :, o:o + 2 * D]
    main = jnp.concatenate([z, xbc, qkvr, gates], axis=-1).astype(bf16)
    pad = jnp.zeros(w_in.shape[:2] + (LANES - SSD_HPG - GLA_RANK,), w_in.dtype)
    tails = [jnp.concatenate([dt[:, :, SSD_HPG * g:SSD_HPG * (g + 1)], glr, pad], axis=-1)
             for g in range(SSD_NG)]
    return main, jnp.concatenate(tails, axis=-1).astype(bf16)


def _pad_lanes(v, width=LANES):
    return jnp.concatenate([v, jnp.zeros(v.shape[:-1] + (width - v.shape[-1],), v.dtype)], axis=-1)


def kernel(x_prompt, x_sample, c_prompt, c_sample, state_conv, state_ssm, state_gla, w_ada, b_ada, norm1_w, w_in, conv_w, conv_b, dt_bias, a_log, d_skip, ssd_norm_w, w_ssd_br, gla_gate_up, gla_gate_b, gla_norm_w, w_gla_br, merge_b, w_out, norm2_w, w_router, router_bias, w_exp_gate, w_exp_up, w_exp_down, final_norm_w):
    bp, sp, _ = x_prompt.shape
    bs, ss, _ = x_sample.shape

    mod = _adaln(jnp.concatenate([c_prompt, c_sample], axis=0), w_ada, b_ada)
    mod_p = mod[:, :bp].reshape(DEPTH, bp, 6, D)
    mod_s = mod[:, bp:].reshape(DEPTH, bs, 6, D)

    w_main, w_tail = _prep_in_weights(w_in)
    w_ssd_b = w_ssd_br.astype(bf16)
    w_gla_b = w_gla_br.astype(bf16)
    w_out_b = w_out.astype(bf16)
    wg_b = w_exp_gate.astype(bf16)
    wu_b = w_exp_up.astype(bf16)
    wd_b = w_exp_down.astype(bf16)
    dtb = _pad_lanes(dt_bias.reshape(DEPTH, SSD_NG, 1, SSD_HPG))
    alog = _pad_lanes(a_log.reshape(DEPTH, SSD_NG, 1, SSD_HPG))
    dsk = jnp.repeat(d_skip, SSD_HD, axis=-1).reshape(DEPTH, SSD_NG, 1, SSD_GW)
    gate_up = jnp.concatenate(
        [jnp.zeros((DEPTH, TAIL_GLR0, GLA_KD), f32), gla_gate_up,
         jnp.zeros((DEPTH, LANES - TAIL_GLR0 - GLA_RANK, GLA_KD), f32)], axis=1).astype(bf16)
    wr = _pad_lanes(w_router)
    wr_hi = wr.astype(bf16)
    wr_lo = (wr - wr_hi.astype(f32)).astype(bf16)
    rb = router_bias.reshape(N_EXP, 1)

    xp = x_prompt.reshape(bp * sp, D)
    xs = x_sample
    ssm_s = state_ssm.reshape(DEPTH, bs, SSD_NG, SSD_GW, SSD_NS)
    outs = {k: [] for k in ("cp", "sp", "gp", "cs", "ss", "gs")}
    ssd_lp = min(128, sp)
    gla_lp = min(64, sp)
    for l in range(DEPTH):
        ssd_w = (conv_w[l], conv_b[l].reshape(1, CONV_DIM), dtb[l], alog[l], dsk[l],
                 ssd_norm_w[l].reshape(1, SSD_INNER), w_ssd_b[l])
        gla_w = (gate_up[l], gla_gate_b[l].reshape(1, GLA_KD), gla_norm_w[l].reshape(1, GLA_DV), w_gla_b[l])
        n1 = norm1_w[l].reshape(1, D)
        n2 = norm2_w[l].reshape(1, D)
        mb = merge_b[l].reshape(1, 2 * D)

        z, xbc, qkvr, gate, tail = _inproj(xp, mod_p[l], n1, w_main[l], w_tail[l], seq3d=False, act_dtype=bf16)
        bra, cnew, snew = _ssd(xbc, tail, z, None, None, ssd_w, nseq=bp, L=ssd_lp, sb=1, nchunks=sp // ssd_lp)
        brb, gnew = _gla(qkvr, tail, None, gla_w, nseq=bp, L=gla_lp, C=16, sb=1, nchunks=sp // gla_lp)
        x1, h2, comb = _merge(xp, bra, brb, gate, mod_p[l], mb, w_out_b[l], n2, wr_hi, wr_lo, rb, seq3d=False)
        xp = _moe(h2, comb, wg_b[l], wu_b[l], wd_b[l], x1, mod_p[l], seq3d=False)
        outs["cp"].append(cnew)
        outs["sp"].append(snew.reshape(bp, SSD_NH, SSD_HD, SSD_NS))
        outs["gp"].append(gnew)

        z, xbc, qkvr, gate, tail = _inproj(xs, mod_s[l], n1, w_main[l], w_tail[l], seq3d=True, act_dtype=f32)
        bra, cnew, snew = _ssd(xbc, tail, z, state_conv[l], ssm_s[l], ssd_w, nseq=bs, L=ss, sb=4, nchunks=1)
        brb, gnew = _gla(qkvr, tail, state_gla[l], gla_w, nseq=bs, L=ss, C=ss, sb=4, nchunks=1)
        x1, h2, comb = _merge(xs, bra, brb, gate, mod_s[l], mb, w_out_b[l], n2, wr_hi, wr_lo, rb, seq3d=True)
        xs = _moe(h2, comb, wg_b[l], wu_b[l], wd_b[l], x1, mod_s[l], seq3d=True)
        outs["cs"].append(cnew)
        outs["ss"].append(snew.reshape(bs, SSD_NH, SSD_HD, SSD_NS))
        outs["gs"].append(gnew)

    fw = final_norm_w.reshape(1, D)
    y_prompt = _final_norm(xp, fw).reshape(bp, sp, D)
    y_sample = _final_norm(xs.reshape(bs * ss, D), fw).reshape(bs, ss, D)
    return (y_prompt, y_sample, jnp.stack(outs["cp"]), jnp.stack(outs["sp"]), jnp.stack(outs["gp"]),
            jnp.stack(outs["cs"]), jnp.stack(outs["ss"]), jnp.stack(outs["gs"]))
```

```python
import functools

import numpy as np
import jax
import jax.numpy as jnp
from jax import lax
from jax.experimental import pallas as pl
from jax.experimental.pallas import tpu as pltpu

f32 = jnp.float32
bf16 = jnp.bfloat16

D = 1024
DEPTH = 4
SSD_INNER = 2048
SSD_HD = 64
SSD_NH = 32
SSD_NS = 128
SSD_NG = 4
SSD_HPG = 8
SSD_GW = SSD_INNER // SSD_NG
CONV_K = 4
CONV_DIM = SSD_INNER + 2 * SSD_NG * SSD_NS
GLA_NH = 4
GLA_DK = 128
GLA_DV = 256
GLA_KD = GLA_NH * GLA_DK
GLA_VD = GLA_NH * GLA_DV
GLA_RANK = 16
GLA_TAU = 16.0
N_EXP = 16
N_EGRP = 4
EXP_PER_GRP = 4
EXP_FF = 512
EPS = 1e-6
LANES = 128
TAIL_GLR0 = SSD_NH
MAIN_W = SSD_INNER + CONV_DIM + 2 * GLA_KD + 2 * GLA_VD + 2 * D
VMEM_LIMIT = 56 * 1024 * 1024


def _sigmoid(x):
    return 1.0 / (1.0 + jnp.exp(-x))


def _silu(x):
    return x * _sigmoid(x)


def _softplus(x):
    return jnp.maximum(x, 0.0) + jnp.log1p(jnp.exp(-jnp.abs(x)))


def _log_sigmoid(x):
    return jnp.minimum(x, 0.0) - jnp.log1p(jnp.exp(-jnp.abs(x)))


def _dot(a, b):
    return jnp.dot(a, b, preferred_element_type=f32)


def _dot_nt(a, b):
    return lax.dot_general(a, b, (((1,), (1,)), ((), ())), preferred_element_type=f32)


def _dot_tn(a, b):
    return lax.dot_general(a, b, (((0,), (0,)), ((), ())), preferred_element_type=f32)


def _split2(x):
    hi = x.astype(bf16)
    return hi, (x - hi.astype(f32)).astype(bf16)


def _split3(x):
    hi = x.astype(bf16)
    r1 = x - hi.astype(f32)
    mid = r1.astype(bf16)
    lo = (r1 - mid.astype(f32)).astype(bf16)
    return hi, mid, lo


def _cumsum_rows(x, tri):
    hi, mid, lo = _split3(x)
    return _dot(tri, hi) + _dot(tri, mid) + _dot(tri, lo)


def _tri(n):
    r = lax.broadcasted_iota(jnp.int32, (n, n), 0)
    c = lax.broadcasted_iota(jnp.int32, (n, n), 1)
    return r >= c


def _norm_mod(x, mod_ref, w_ref, i_shift, i_scale, seq3d):
    ms = jnp.mean(x * x, axis=-1, keepdims=True)
    y = x * lax.rsqrt(ms + EPS) * w_ref[...]
    if seq3d:
        sc = mod_ref[:, i_scale:i_scale + 1, :]
        sh = mod_ref[:, i_shift:i_shift + 1, :]
        h = y * (1.0 + sc) + sh
        return h.reshape(h.shape[0] * h.shape[1], h.shape[2])
    sc = mod_ref[i_scale:i_scale + 1, :]
    sh = mod_ref[i_shift:i_shift + 1, :]
    return y * (1.0 + sc) + sh


def _seq_tile(nseq, want):
    return want if nseq % want == 0 else nseq


def _stacked_out(prev, in_specs, args, aliases, out_index):
    if prev is not None:
        in_specs.append(pl.BlockSpec(memory_space=pl.ANY))
        args.append(prev)
        aliases[len(args) - 1] = out_index


def _adaln_kernel(c_ref, w_ref, b_ref, o_ref):
    s = _silu(c_ref[...]).astype(bf16)
    o_ref[...] = _dot(s, w_ref[...].astype(bf16)) + b_ref[...]


def _adaln(c_all, w_ada, b_ada):
    n = c_all.shape[0]
    tn = 1024
    return pl.pallas_call(
        _adaln_kernel,
        grid=(DEPTH, 6 * D // tn),
        in_specs=[
            pl.BlockSpec((n, D), lambda l, j: (0, 0)),
            pl.BlockSpec((None, D, tn), lambda l, j: (l, 0, j)),
            pl.BlockSpec((None, 1, tn), lambda l, j: (l, 0, j)),
        ],
        out_specs=pl.BlockSpec((None, n, tn), lambda l, j: (l, 0, j)),
        out_shape=jax.ShapeDtypeStruct((DEPTH, n, 6 * D), f32),
        compiler_params=pltpu.CompilerParams(
            dimension_semantics=("arbitrary", "arbitrary"), vmem_limit_bytes=VMEM_LIMIT),
        name="adaln",
    )(c_all, w_ada, b_ada.reshape(DEPTH, 1, 6 * D))


IN_TN = 1024
IN_NJ = MAIN_W // IN_TN
IN_SUB = 256
IN_NSUB = IN_TN // IN_SUB
CONV_PAD = 8
KPREV = CONV_K - 1


def _inproj_kernel(*refs, seq3d, has_state, per_seq, T):
    if has_state:
        (x_ref, mod_ref, n1_ref, wm_ref, wt_ref, wg_ref, gb_ref, cw_ref, cb_ref, mb_ref, conv0_ref) = refs[:11]
        rest = refs[11:]
    else:
        (x_ref, mod_ref, n1_ref, wm_ref, wt_ref, wg_ref, gb_ref, cw_ref, cb_ref, mb_ref) = refs[:10]
        conv0_ref = None
        rest = refs[10:]
    (z4_ref, xs4_ref, bc_ref, qkvr_ref, gate_ref, tail_ref, lg_ref, convn_ref,
     h_scr, cscr, carry) = rest[-11:]
    i = pl.program_id(0)
    j = pl.program_id(1)
    tm = h_scr.shape[0]
    ns = cscr.shape[1]

    @pl.when(j == 0)
    def _():
        h = _norm_mod(x_ref[...], mod_ref, n1_ref, 0, 1, seq3d).astype(bf16)
        h_scr[...] = h
        t = _dot(h, wt_ref[...])
        tail_ref[...] = t
        pre = _dot(t.astype(bf16), wg_ref[...]) + gb_ref[...]
        lg_ref[...] = _log_sigmoid(pre) * (1.0 / GLA_TAU)

    def sub_dot(c):
        return _dot(h_scr[...], wm_ref[:, IN_SUB * c:IN_SUB * (c + 1)])

    def conv_silu(a, cj, c):
        cols = slice(IN_SUB * c, IN_SUB * (c + 1))
        a3 = a.reshape(ns, T, IN_SUB)
        if has_state:
            prev = conv0_ref[:, :, cols]
        else:
            prev = jnp.where(i % per_seq == 0, 0.0, carry[cj, c, CONV_PAD - KPREV:CONV_PAD, :])[None]
        cscr[c, :, CONV_PAD - KPREV:CONV_PAD, :] = prev
        cscr[c, :, CONV_PAD:CONV_PAD + T, :] = a3
        out = cb_ref[:, cols] + a3 * cw_ref[KPREV:KPREV + 1, cols]
        for k in range(KPREV):
            out = out + cscr[c, :, CONV_PAD - KPREV + k:CONV_PAD - KPREV + k + T, :] * cw_ref[k:k + 1, cols]
        new_tail = cscr[c, :, CONV_PAD + T - KPREV:CONV_PAD + T, :]
        convn_ref[:, :, IN_TN * cj + IN_SUB * c:IN_TN * cj + IN_SUB * (c + 1)] = new_tail
        if not has_state:
            carry[cj, c, CONV_PAD - KPREV:CONV_PAD, :] = new_tail[0]
        return _silu(out).reshape(tm, IN_SUB).astype(bf16)

    per_grp = SSD_GW // IN_SUB

    @pl.when(j < 2)
    def _():
        for c in range(IN_NSUB):
            lanes = slice(IN_SUB * (c % per_grp), IN_SUB * (c % per_grp + 1))
            z4_ref[c // per_grp, :, lanes] = _silu(sub_dot(c)).astype(bf16)

    for cj in range(2):
        @pl.when(j == 2 + cj)
        def _(cj=cj):
            for c in range(IN_NSUB):
                lanes = slice(IN_SUB * (c % per_grp), IN_SUB * (c % per_grp + 1))
                xs4_ref[c // per_grp, :, lanes] = conv_silu(sub_dot(c), cj, c)

    @pl.when(j == 4)
    def _():
        for c in range(IN_NSUB):
            xc = conv_silu(sub_dot(c), 2, c)
            for q in range(IN_SUB // SSD_NS):
                bc_ref[c * (IN_SUB // SSD_NS) + q] = xc[:, SSD_NS * q:SSD_NS * (q + 1)]

    @pl.when(jnp.logical_and(j >= 5, j < 7))
    def _():
        for c in range(IN_NSUB):
            qkvr_ref[:, IN_SUB * c:IN_SUB * (c + 1)] = sub_dot(c).astype(bf16)

    @pl.when(j == 7)
    def _():
        for c in range(IN_NSUB):
            qkvr_ref[:, IN_SUB * c:IN_SUB * (c + 1)] = _silu(sub_dot(c)).astype(bf16)

    @pl.when(j >= 8)
    def _():
        for c in range(IN_NSUB):
            cols = slice(IN_SUB * c, IN_SUB * (c + 1))
            gate_ref[:, cols] = _sigmoid(sub_dot(c) + mb_ref[:, cols]).astype(bf16)


def _inproj(l, x, mod, conv0, convn_prev, w, *, seq3d):
    (n1w, w_main, w_tail, gate_up, gate_b, conv_w, conv_b, merge_b) = w
    has_state = conv0 is not None
    if seq3d:
        nseq, T = x.shape[0], x.shape[1]
        ns = _seq_tile(nseq, 64)
        ntok = nseq * T
        tm = ns * T
        nt = nseq // ns
        per_seq = 1
        x_spec = pl.BlockSpec((ns, T, D), lambda i, j: (i, 0, 0))
        mod_spec = pl.BlockSpec((None, ns, 6, D), lambda i, j: (l, i, 0, 0))
        seq_blk = lambda i: i
    else:
        nseq = mod.shape[1]
        ntok = x.shape[0]
        tm = min(1024, ntok // nseq)
        T = tm
        ns = 1
        nt = ntok // tm
        per_seq = ntok // nseq // tm
        x_spec = pl.BlockSpec((tm, D), lambda i, j: (i, 0))
        mod_spec = pl.BlockSpec((None, None, 6, D), lambda i, j: (l, i // per_seq, 0, 0))
        seq_blk = lambda i: i // per_seq

    def cj(j):
        return jnp.clip(j - 2, 0, 2)

    in_specs = [
        x_spec,
        mod_spec,
        pl.BlockSpec((None, 1, D), lambda i, j: (l, 0, 0)),
        pl.BlockSpec((None, D, IN_TN), lambda i, j: (l, 0, j)),
        pl.BlockSpec((None, D, LANES), lambda i, j: (l, 0, 0)),
        pl.BlockSpec((None, LANES, GLA_KD), lambda i, j: (l, 0, 0)),
        pl.BlockSpec((None, 1, GLA_KD), lambda i, j: (l, 0, 0)),
        pl.BlockSpec((None, CONV_K, IN_TN), lambda i, j: (l, 0, cj(j))),
        pl.BlockSpec((None, 1, IN_TN), lambda i, j: (l, 0, cj(j))),
        pl.BlockSpec((None, 1, IN_TN), lambda i, j: (l, 0, jnp.clip(j - 8, 0, 1))),
    ]
    args = [x, mod, n1w, w_main, w_tail, gate_up, gate_b, conv_w, conv_b, merge_b]
    if has_state:
        in_specs.append(pl.BlockSpec((None, ns, KPREV, IN_TN), lambda i, j: (l, i, 0, cj(j))))
        args.append(conv0)
    aliases = {}
    _stacked_out(convn_prev, in_specs, args, aliases, 7)
    out_shapes = (
        jax.ShapeDtypeStruct((SSD_NG, ntok, SSD_GW), bf16),
        jax.ShapeDtypeStruct((SSD_NG, ntok, SSD_GW), bf16),
        jax.ShapeDtypeStruct((2 * SSD_NG, ntok, SSD_NS), bf16),
        jax.ShapeDtypeStruct((ntok, 2 * GLA_KD + 2 * GLA_VD), bf16),
        jax.ShapeDtypeStruct((ntok, 2 * D), bf16),
        jax.ShapeDtypeStruct((ntok, LANES), f32),
        jax.ShapeDtypeStruct((ntok, GLA_KD), f32),
        jax.ShapeDtypeStruct((DEPTH, nseq, KPREV, CONV_DIM), f32),
    )
    out_specs = (
        pl.BlockSpec((2, tm, SSD_GW), lambda i, j: (jnp.clip(j, 0, 1), i, 0)),
        pl.BlockSpec((2, tm, SSD_GW), lambda i, j: (jnp.clip(j - 2, 0, 1), i, 0)),
        pl.BlockSpec((2 * SSD_NG, tm, SSD_NS), lambda i, j: (0, i, 0)),
        pl.BlockSpec((tm, IN_TN), lambda i, j: (i, jnp.clip(j - 5, 0, 2))),
        pl.BlockSpec((tm, IN_TN), lambda i, j: (i, jnp.clip(j - 8, 0, 1))),
        pl.BlockSpec((tm, LANES), lambda i, j: (i, 0)),
        pl.BlockSpec((tm, GLA_KD), lambda i, j: (i, 0)),
        pl.BlockSpec((None, ns, KPREV, CONV_DIM), lambda i, j: (l, seq_blk(i), 0, 0)),
    )
    return pl.pallas_call(
        functools.partial(_inproj_kernel, seq3d=seq3d, has_state=has_state, per_seq=per_seq, T=T),
        grid=(nt, IN_NJ),
        in_specs=in_specs,
        out_specs=out_specs,
        out_shape=out_shapes,
        input_output_aliases=aliases,
        scratch_shapes=[
            pltpu.VMEM((tm, D), bf16),
            pltpu.VMEM((IN_NSUB, ns, CONV_PAD + T, IN_SUB), f32),
            pltpu.VMEM((3, IN_NSUB, CONV_PAD, IN_SUB), f32),
        ],
        compiler_params=pltpu.CompilerParams(
            dimension_semantics=("arbitrary", "arbitrary"), vmem_limit_bytes=VMEM_LIMIT),
        name="inproj",
    )(*args)


def _ssd_consts():
    bigsel = np.zeros((SSD_NG, 3 * LANES, SSD_HPG * LANES), np.float32)
    e8 = np.zeros((SSD_NG, 2 * LANES, SSD_GW), np.float32)
    for g in range(SSD_NG):
        for j in range(SSD_HPG):
            for k in range(3):
                bigsel[g, k * LANES + SSD_HPG * g + j, LANES * j:LANES * (j + 1)] = 1.0
            for k in range(2):
                e8[g, k * LANES + SSD_HPG * g + j, SSD_HD * j:SSD_HD * (j + 1)] = 1.0
    return jnp.asarray(bigsel, bf16), jnp.asarray(e8, bf16)


def _ssd_kernel(*refs, L, sb, nchunks, has_state):
    R = sb * L
    xs4_ref, bc_ref, z4_ref, tail_ref = refs[:4]
    k = 4
    ssm0_ref = None
    if has_state:
        ssm0_ref = refs[k]
        k += 1
    dtb_ref, alog_ref, dsk_ref, nw_ref, wbr_ref, bigsel_ref, e8_ref = refs[k:k + 7]
    br_ref, ssmn_ref, cst_scr = refs[-3:]
    c = pl.program_id(1)

    rowi = lax.broadcasted_iota(jnp.int32, (R, R), 0)
    coli = lax.broadcasted_iota(jnp.int32, (R, R), 1)
    mask = rowi >= coli
    if sb > 1:
        sh = L.bit_length() - 1
        mask = jnp.logical_and(mask, (rowi >> sh) == (coli >> sh))
    tri = mask.astype(bf16)
    lane = lax.broadcasted_iota(jnp.int32, (R, LANES), 1)
    lo_half = lane < SSD_HD

    def init():
        if has_state:
            ssmn_ref[...] = ssm0_ref[...]
        else:
            ssmn_ref[...] = jnp.zeros(ssmn_ref.shape, f32)

    if nchunks == 1:
        init()
    else:
        pl.when(c == 0)(init)

    dtp = _softplus(tail_ref[...] + dtb_ref[...])
    a = -jnp.exp(alog_ref[...])
    cs = _cumsum_rows(dtp * a, tri)
    cst_scr[...] = cs.T
    h3 = jnp.concatenate(_split3(cs), axis=1)
    d2 = jnp.concatenate(_split2(dtp), axis=1)
    br_ref[...] = jnp.zeros(br_ref.shape, f32)

    def seq_last(x):
        if sb == 1:
            return x[R - 1:R, :]
        x3 = x.reshape(sb, L, x.shape[-1])
        return jnp.broadcast_to(x3[:, L - 1:L, :], x3.shape).reshape(x.shape)

    def group_body(g, carry):
        cm = _dot(h3, bigsel_ref[g])
        dt_exp = _dot(d2, e8_ref[g])
        cs_exp = jnp.concatenate(
            [jnp.where(lo_half, cm[:, 2 * LANES * i:2 * LANES * i + LANES],
                       cm[:, 2 * LANES * i + LANES:2 * LANES * (i + 1)]) for i in range(SSD_HPG // 2)], axis=1)
        ecs = jnp.exp(cs_exp)
        ce = seq_last(cs_exp)
        xs = xs4_ref[g].astype(f32)
        xdt = xs * dt_exp
        xdt_b = xdt.astype(bf16)
        xse = xdt * jnp.exp(ce - cs_exp)
        bg = bc_ref[g]
        cg = bc_ref[SSD_NG + g]
        cb = _dot_nt(cg, bg)

        pairs = []
        for i in range(SSD_HPG // 2):
            ws = []
            for j in (2 * i, 2 * i + 1):
                row = cst_scr[pl.ds(SSD_HPG * g + j, 1), :]
                seg = cm[:, LANES * j:LANES * j + R] - row
                dec = jnp.exp(jnp.where(mask, seg, -jnp.inf))
                ws.append((cb * dec).astype(bf16))
            yy = _dot(jnp.concatenate(ws, axis=0), xdt_b[:, LANES * i:LANES * (i + 1)])
            pairs.append(jnp.where(lo_half, yy[:R], yy[R:]))
        y = jnp.concatenate(pairs, axis=1)

        if sb > 1:
            bgf = bg.astype(f32)
            cgf = cg.astype(f32)
        ys_parts = []
        for s in range(sb):
            rs = slice(s * L, (s + 1) * L)
            st = ssmn_ref[s, g]
            b_s = bg if sb == 1 else bgf[rs].astype(bf16)
            c_s = cg if sb == 1 else cgf[rs].astype(bf16)
            ys_parts.append(_dot_nt(c_s, st.astype(bf16)))
            upd = _dot_tn(xse[rs].astype(bf16), b_s)
            e_end = jnp.exp(ce[s * L:s * L + 1, :])
            ssmn_ref[s, g] = jnp.concatenate(
                [st[SSD_HD * j:SSD_HD * (j + 1)] * e_end[:, SSD_HD * j:SSD_HD * j + 1]
                 + upd[SSD_HD * j:SSD_HD * (j + 1)] for j in range(SSD_HPG)], axis=0)
        ys = ys_parts[0] if sb == 1 else jnp.concatenate(ys_parts, axis=0)
        y = y + ys * ecs + dsk_ref[g] * xs

        yg = y * z4_ref[g].astype(f32)
        yn = yg * lax.rsqrt(jnp.mean(yg * yg, axis=-1, keepdims=True) + EPS) * nw_ref[g]
        br_ref[...] += _dot(yn.astype(bf16), wbr_ref[g])
        return carry

    for g in range(SSD_NG):
        group_body(g, 0)


def _ssd(l, xs4, bc, z4, tail, ssm0, ssmn_prev, w, *, nseq, L, sb, nchunks):
    has_state = ssm0 is not None
    dtb, alog, dsk, nw, wbr, bigsel, e8 = w
    ntok = tail.shape[0]
    R = sb * L
    if nchunks == 1:
        rblk = lambda b, c: b
    else:
        rblk = lambda b, c: b * nchunks + c
    in_specs = [
        pl.BlockSpec((SSD_NG, R, SSD_GW), lambda b, c: (0, rblk(b, c), 0)),
        pl.BlockSpec((2 * SSD_NG, R, SSD_NS), lambda b, c: (0, rblk(b, c), 0)),
        pl.BlockSpec((SSD_NG, R, SSD_GW), lambda b, c: (0, rblk(b, c), 0)),
        pl.BlockSpec((R, LANES), lambda b, c: (rblk(b, c), 0)),
    ]
    args = [xs4, bc, z4, tail]
    if has_state:
        in_specs.append(pl.BlockSpec((None, sb, SSD_NG, SSD_GW, SSD_NS), lambda b, c: (l, b, 0, 0, 0)))
        args.append(ssm0)
    in_specs += [
        pl.BlockSpec((None, 1, LANES), lambda b, c: (l, 0, 0)),
        pl.BlockSpec((None, 1, LANES), lambda b, c: (l, 0, 0)),
        pl.BlockSpec((None, SSD_NG, 1, SSD_GW), lambda b, c: (l, 0, 0, 0)),
        pl.BlockSpec((None, SSD_NG, 1, SSD_GW), lambda b, c: (l, 0, 0, 0)),
        pl.BlockSpec((None, SSD_NG, SSD_GW, D), lambda b, c: (l, 0, 0, 0)),
        pl.BlockSpec((SSD_NG, 3 * LANES, SSD_HPG * LANES), lambda b, c: (0, 0, 0)),
        pl.BlockSpec((SSD_NG, 2 * LANES, SSD_GW), lambda b, c: (0, 0, 0)),
    ]
    args += [dtb, alog, dsk, nw, wbr, bigsel, e8]
    aliases = {}
    _stacked_out(ssmn_prev, in_specs, args, aliases, 1)
    return pl.pallas_call(
        functools.partial(_ssd_kernel, L=L, sb=sb, nchunks=nchunks, has_state=has_state),
        grid=(nseq // sb, nchunks),
        in_specs=in_specs,
        out_specs=(
            pl.BlockSpec((R, D), lambda b, c: (rblk(b, c), 0)),
            pl.BlockSpec((None, sb, SSD_NG, SSD_GW, SSD_NS), lambda b, c: (l, b, 0, 0, 0)),
        ),
        out_shape=(
            jax.ShapeDtypeStruct((ntok, D), f32),
            jax.ShapeDtypeStruct((DEPTH, nseq, SSD_NG, SSD_GW, SSD_NS), f32),
        ),
        input_output_aliases=aliases,
        scratch_shapes=[pltpu.VMEM((LANES, R), f32)],
        compiler_params=pltpu.CompilerParams(
            dimension_semantics=("arbitrary", "arbitrary"), vmem_limit_bytes=VMEM_LIMIT),
        name="ssd",
    )(*args)


GLA_PAD = 16


def _gla_kernel(*refs, L, C, sb, nchunks, has_state):
    qkvr_ref, lg_ref = refs[:2]
    k = 2
    gla0_ref = None
    if has_state:
        gla0_ref = refs[k]
        k += 1
    gnw_ref, wbr_ref = refs[k:k + 2]
    br_ref, glan_ref, kpad, apad, qf = refs[-5:]
    c = pl.program_id(1)
    nb = L // C
    qf[...] = qkvr_ref[...].astype(f32)
    tri = _tri(L).astype(bf16)
    rowi = lax.broadcasted_iota(jnp.int32, (L, L), 0)
    coli = lax.broadcasted_iota(jnp.int32, (L, L), 1)
    same_blk_off = rowi & (C - 1)
    rowl = lax.broadcasted_iota(jnp.int32, (L, 1), 0)

    kpad[0:GLA_PAD, :] = jnp.zeros((GLA_PAD, GLA_KD), f32)
    apad[0:GLA_PAD, :] = jnp.zeros((GLA_PAD, GLA_KD), f32)

    def seq_body(s, carry):
        r0 = pl.multiple_of(s * L, 8)
        rows = pl.ds(r0, L)

        def init():
            if has_state:
                glan_ref[s] = gla0_ref[s]
            else:
                glan_ref[s] = jnp.zeros(glan_ref.shape[1:], f32)

        if nchunks == 1:
            init()
        else:
            pl.when(c == 0)(init)

        lg = lg_ref[rows, :]
        g = _cumsum_rows(lg, tri)
        g_last = g[L - 1:L, :]
        q = qf[rows, 0:GLA_KD]
        k_ = qf[rows, GLA_KD:2 * GLA_KD]
        kpad[GLA_PAD:GLA_PAD + L, :] = k_
        apad[GLA_PAD:GLA_PAD + L, :] = jnp.exp(lg)

        acc = jnp.zeros((L, D), f32)
        for h in range(GLA_NH):
            kl = slice(GLA_DK * h, GLA_DK * (h + 1))
            vl = slice(2 * GLA_KD + GLA_DV * h, 2 * GLA_KD + GLA_DV * (h + 1))
            rl = slice(2 * GLA_KD + GLA_VD + GLA_DV * h, 2 * GLA_KD + GLA_VD + GLA_DV * (h + 1))
            gh = g[:, kl]
            qh = q[:, kl]
            kh = k_[:, kl]
            vh = qf[rows, vl].astype(bf16)
            st = glan_ref[s, h]
            o = _dot((qh * jnp.exp(gh)).astype(bf16), st.astype(bf16))

            gbnd = [jnp.zeros((1, GLA_DK), f32)] + [gh[C * i - 1:C * i, :] for i in range(1, nb)]
            gblk = jnp.concatenate([jnp.broadcast_to(b, (C, GLA_DK)) for b in gbnd], axis=0)
            qt = (qh * jnp.exp(gh - gblk)).astype(bf16)
            blocks = [jnp.zeros((C, L), f32)]
            for i in range(1, nb):
                kt = kh * jnp.exp(jnp.where(rowl < C * i, gbnd[i] - gh, -jnp.inf))
                blocks.append(_dot_nt(qt[C * i:C * (i + 1), :], kt.astype(bf16)))
            att = jnp.concatenate(blocks, axis=0) if nb > 1 else blocks[0]

            e = None
            for d in range(C):
                ks = kpad[GLA_PAD - d:GLA_PAD - d + L, kl]
                if d == 0:
                    t = qh * ks
                else:
                    a_sh = apad[GLA_PAD - (d - 1):GLA_PAD - (d - 1) + L, kl]
                    e = a_sh if e is None else e * a_sh
                    t = qh * e * ks
                band = jnp.sum(t, axis=-1, keepdims=True)
                hit = jnp.logical_and(same_blk_off >= d, coli == rowi - d)
                att = att + jnp.where(hit, band, 0.0)
            o = o + _dot(att.astype(bf16), vh)

            kd = (kh * jnp.exp(g_last[:, kl] - gh)).astype(bf16)
            e_col = jnp.exp(jnp.broadcast_to(g_last[:, kl], (8, GLA_DK))).T[:, 0:1]
            glan_ref[s, h] = st * e_col + _dot_tn(kd, vh)

            on = o * lax.rsqrt(jnp.mean(o * o, axis=-1, keepdims=True) + EPS) * gnw_ref[...]
            og = on * qf[rows, rl]
            acc = acc + _dot(og.astype(bf16), wbr_ref[GLA_DV * h:GLA_DV * (h + 1), :])
        br_ref[rows, :] = acc
        return carry

    lax.fori_loop(0, sb, seq_body, 0)


def _gla(l, qkvr, lg, gla0, glan_prev, w, *, nseq, L, C, sb, nchunks):
    has_state = gla0 is not None
    gnw, wbr = w
    ntok = qkvr.shape[0]
    rb = sb * L
    if nchunks == 1:
        row = lambda b, c: (b, 0)
    else:
        row = lambda b, c: (b * nchunks + c, 0)
    in_specs = [
        pl.BlockSpec((rb, 2 * GLA_KD + 2 * GLA_VD), row),
        pl.BlockSpec((rb, GLA_KD), row),
    ]
    args = [qkvr, lg]
    if has_state:
        in_specs.append(pl.BlockSpec((None, sb, GLA_NH, GLA_DK, GLA_DV), lambda b, c: (l, b, 0, 0, 0)))
        args.append(gla0)
    in_specs += [
        pl.BlockSpec((None, 1, GLA_DV), lambda b, c: (l, 0, 0)),
        pl.BlockSpec((None, GLA_VD, D), lambda b, c: (l, 0, 0)),
    ]
    args += [gnw, wbr]
    aliases = {}
    _stacked_out(glan_prev, in_specs, args, aliases, 1)
    return pl.pallas_call(
        functools.partial(_gla_kernel, L=L, C=C, sb=sb, nchunks=nchunks, has_state=has_state),
        grid=(nseq // sb, nchunks),
        in_specs=in_specs,
        out_specs=(
            pl.BlockSpec((rb, D), row),
            pl.BlockSpec((None, sb, GLA_NH, GLA_DK, GLA_DV), lambda b, c: (l, b, 0, 0, 0)),
        ),
        out_shape=(
            jax.ShapeDtypeStruct((ntok, D), f32),
            jax.ShapeDtypeStruct((DEPTH, nseq, GLA_NH, GLA_DK, GLA_DV), f32),
        ),
        input_output_aliases=aliases,
        scratch_shapes=[
            pltpu.VMEM((GLA_PAD + L, GLA_KD), f32),
            pltpu.VMEM((GLA_PAD + L, GLA_KD), f32),
            pltpu.VMEM((rb, 2 * GLA_KD + 2 * GLA_VD), f32),
        ],
        compiler_params=pltpu.CompilerParams(
            dimension_semantics=("arbitrary", "arbitrary"), vmem_limit_bytes=VMEM_LIMIT),
        name="gla",
    )(*args)


def _top2_sum(a, b, c, d):
    hi1, lo1 = jnp.maximum(a, b), jnp.minimum(a, b)
    hi2, lo2 = jnp.maximum(c, d), jnp.minimum(c, d)
    return jnp.maximum(hi1, hi2) + jnp.maximum(jnp.minimum(hi1, hi2), jnp.maximum(lo1, lo2))


def _route_rows(sig, biased):
    gsc = [_top2_sum(*biased[EXP_PER_GRP * g:EXP_PER_GRP * (g + 1)]) for g in range(N_EGRP)]
    best = jnp.zeros_like(gsc[0], dtype=jnp.int32)
    m = gsc[0]
    for g in range(1, N_EGRP):
        better = gsc[g] > m
        best = jnp.where(better, g, best)
        m = jnp.where(better, gsc[g], m)
    masked = [jnp.where(best == (e // EXP_PER_GRP), biased[e], -jnp.inf) for e in range(N_EXP)]

    def first_argmax(vals):
        idx = jnp.zeros_like(best)
        mx = vals[0]
        for e in range(1, N_EXP):
            better = vals[e] > mx
            idx = jnp.where(better, e, idx)
            mx = jnp.where(better, vals[e], mx)
        return idx

    i1 = first_argmax(masked)
    i2 = first_argmax([jnp.where(i1 == e, -jnp.inf, masked[e]) for e in range(N_EXP)])
    w1 = sum(jnp.where(i1 == e, sig[e], 0.0) for e in range(N_EXP))
    w2 = sum(jnp.where(i2 == e, sig[e], 0.0) for e in range(N_EXP))
    den = w1 + w2
    w1 = w1 / den
    w2 = w2 / den
    return [jnp.where(i1 == e, w1, 0.0) + jnp.where(i2 == e, w2, 0.0) for e in range(N_EXP)]


def _merge_kernel(x_ref, bra_ref, brb_ref, gate_ref, mod_ref, wo_ref, n2_ref,
                  wr_hi_ref, wr_lo_ref, rb_ref, x1_ref, h2_ref, comb_ref, *, seq3d):
    mixed_in = (gate_ref[:, 0:D].astype(f32) * bra_ref[...]
                + gate_ref[:, D:2 * D].astype(f32) * brb_ref[...])
    mixed = _dot(mixed_in.astype(bf16), wo_ref[...])
    x = x_ref[...]
    if seq3d:
        x1 = x + mod_ref[:, 2:3, :] * mixed.reshape(x.shape)
    else:
        x1 = x + mod_ref[2:3, :] * mixed
    x1_ref[...] = x1
    h2 = _norm_mod(x1, mod_ref, n2_ref, 3, 4, seq3d)
    h2_ref[...] = h2.astype(bf16)
    hi, lo = _split2(h2)
    logits = _dot(hi, wr_hi_ref[...]) + (_dot(hi, wr_lo_ref[...]) + _dot(lo, wr_hi_ref[...]))
    lt = logits.T
    sig_all = _sigmoid(lt[0:N_EXP, :])
    bias_all = sig_all + rb_ref[...]
    sig = [sig_all[e:e + 1, :] for e in range(N_EXP)]
    biased = [bias_all[e:e + 1, :] for e in range(N_EXP)]
    comb = _route_rows(sig, biased)
    tm = logits.shape[0]
    comb_t = jnp.concatenate(comb + [jnp.zeros((LANES - N_EXP, tm), f32)], axis=0)
    comb_ref[...] = comb_t.T


def _merge(l, x, bra, brb, gate, mod, wo, n2w, wr_hi, wr_lo, rb, *, seq3d):
    if seq3d:
        ns = _seq_tile(x.shape[0], 32)
        ntok = x.shape[0] * x.shape[1]
        tm = ns * x.shape[1]
        nt = x.shape[0] // ns
        x_spec = pl.BlockSpec((ns, x.shape[1], D), lambda i: (i, 0, 0))
        mod_spec = pl.BlockSpec((None, ns, 6, D), lambda i: (l, i, 0, 0))
    else:
        ntok = x.shape[0]
        nseq = mod.shape[1]
        tm = min(512, ntok // nseq)
        nt = ntok // tm
        per_seq = ntok // nseq // tm
        x_spec = pl.BlockSpec((tm, D), lambda i: (i, 0))
        mod_spec = pl.BlockSpec((None, None, 6, D), lambda i: (l, i // per_seq, 0, 0))
    row = lambda i: (i, 0)
    const = lambda i: (0, 0)
    return pl.pallas_call(
        functools.partial(_merge_kernel, seq3d=seq3d),
        grid=(nt,),
        in_specs=[
            x_spec,
            pl.BlockSpec((tm, D), row),
            pl.BlockSpec((tm, D), row),
            pl.BlockSpec((tm, 2 * D), row),
            mod_spec,
            pl.BlockSpec((None, D, D), lambda i: (l, 0, 0)),
            pl.BlockSpec((None, 1, D), lambda i: (l, 0, 0)),
            pl.BlockSpec((D, LANES), const),
            pl.BlockSpec((D, LANES), const),
            pl.BlockSpec((N_EXP, 1), const),
        ],
        out_specs=(
            x_spec,
            pl.BlockSpec((tm, D), row),
            pl.BlockSpec((tm, LANES), row),
        ),
        out_shape=(
            jax.ShapeDtypeStruct(x.shape, f32),
            jax.ShapeDtypeStruct((ntok, D), bf16),
            jax.ShapeDtypeStruct((ntok, LANES), f32),
        ),
        compiler_params=pltpu.CompilerParams(
            dimension_semantics=("arbitrary",), vmem_limit_bytes=VMEM_LIMIT),
        name="merge",
    )(x, bra, brb, gate, mod, wo, n2w, wr_hi, wr_lo, rb)


def _moe_kernel(h2_ref, comb_ref, wg_ref, wu_ref, wd_ref, x1_ref, mod_ref, x2_ref, acc_scr, *, seq3d):
    e = pl.program_id(1)

    @pl.when(e == 0)
    def _():
        acc_scr[...] = jnp.zeros_like(acc_scr)

    h = h2_ref[...]
    a = _dot(h, wg_ref[...])
    b = _dot(h, wu_ref[...])
    t = (_silu(a) * b).astype(bf16)
    ye = _dot(t, wd_ref[...])
    lane = lax.broadcasted_iota(jnp.int32, comb_ref.shape, 1)
    w = jnp.sum(jnp.where(lane == e, comb_ref[...], 0.0), axis=-1, keepdims=True)
    acc_scr[...] += w * ye

    @pl.when(e == N_EXP - 1)
    def _():
        x1 = x1_ref[...]
        if seq3d:
            x2_ref[...] = x1 + mod_ref[:, 5:6, :] * acc_scr[...].reshape(x1.shape)
        else:
            x2_ref[...] = x1 + mod_ref[5:6, :] * acc_scr[...]


def _moe(l, h2, comb, wg, wu, wd, x1, mod, *, seq3d):
    if seq3d:
        ns = x1.shape[0]
        ntok = ns * x1.shape[1]
        tm = ntok
        nt = 1
        x_spec = pl.BlockSpec((ns, x1.shape[1], D), lambda i, e: (0, 0, 0))
        mod_spec = pl.BlockSpec((None, ns, 6, D), lambda i, e: (l, 0, 0, 0))
    else:
        ntok = x1.shape[0]
        nseq = mod.shape[1]
        tm = min(1024, ntok // nseq)
        nt = ntok // tm
        per_seq = ntok // nseq // tm
        x_spec = pl.BlockSpec((tm, D), lambda i, e: (i, 0))
        mod_spec = pl.BlockSpec((None, None, 6, D), lambda i, e: (l, i // per_seq, 0, 0))
    row = lambda i, e: (i, 0)
    return pl.pallas_call(
        functools.partial(_moe_kernel, seq3d=seq3d),
        grid=(nt, N_EXP),
        in_specs=[
            pl.BlockSpec((tm, D), row),
            pl.BlockSpec((tm, LANES), row),
            pl.BlockSpec((None, None, D, EXP_FF), lambda i, e: (l, e, 0, 0)),
            pl.BlockSpec((None, None, D, EXP_FF), lambda i, e: (l, e, 0, 0)),
            pl.BlockSpec((None, None, EXP_FF, D), lambda i, e: (l, e, 0, 0)),
            x_spec,
            mod_spec,
        ],
        out_specs=x_spec,
        out_shape=jax.ShapeDtypeStruct(x1.shape, f32),
        scratch_shapes=[pltpu.VMEM((tm, D), f32)],
        compiler_params=pltpu.CompilerParams(
            dimension_semantics=("arbitrary", "arbitrary"), vmem_limit_bytes=VMEM_LIMIT),
        name="moe",
    )(h2, comb, wg, wu, wd, x1, mod)


def _final_norm_kernel(x_ref, w_ref, o_ref):
    x = x_ref[...]
    o_ref[...] = x * lax.rsqrt(jnp.mean(x * x, axis=-1, keepdims=True) + EPS) * w_ref[...]


def _final_norm(x2d, w):
    ntok = x2d.shape[0]
    tm = min(1024, ntok)
    return pl.pallas_call(
        _final_norm_kernel,
        grid=(ntok // tm,),
        in_specs=[pl.BlockSpec((tm, D), lambda i: (i, 0)), pl.BlockSpec((1, D), lambda i: (0, 0))],
        out_specs=pl.BlockSpec((tm, D), lambda i: (i, 0)),
        out_shape=jax.ShapeDtypeStruct((ntok, D), f32),
        compiler_params=pltpu.CompilerParams(dimension_semantics=("arbitrary",)),
        name="final_norm",
    )(x2d, w)


def _prep_in_weights(w_in):
    o = 0
    z = w_in[:, :, o:o + SSD_INNER]; o += SSD_INNER
    xbc = w_in[:, :, o:o + CONV_DIM]; o += CONV_DIM
    dt = w_in[:, :, o:o + SSD_NH]; o += SSD_NH
    q = w_in[:, :, o:o + GLA_KD] * (GLA_DK ** -0.5); o += GLA_KD
    kvr = w_in[:, :, o:o + GLA_KD + 2 * GLA_VD]; o += GLA_KD + 2 * GLA_VD
    glr = w_in[:, :, o:o + GLA_RANK]; o += GLA_RANK
    gates = w_in[:, :, o:o + 2 * D]
    main = jnp.concatenate([z, xbc, q, kvr, gates], axis=-1).astype(bf16)
    pad = jnp.zeros(w_in.shape[:2] + (LANES - SSD_NH - GLA_RANK,), w_in.dtype)
    tail = jnp.concatenate([dt, glr, pad], axis=-1).astype(bf16)
    return main, tail


def _pad_lanes(v, width=LANES):
    return jnp.concatenate([v, jnp.zeros(v.shape[:-1] + (width - v.shape[-1],), v.dtype)], axis=-1)


def kernel(x_prompt, x_sample, c_prompt, c_sample, state_conv, state_ssm, state_gla, w_ada, b_ada, norm1_w, w_in, conv_w, conv_b, dt_bias, a_log, d_skip, ssd_norm_w, w_ssd_br, gla_gate_up, gla_gate_b, gla_norm_w, w_gla_br, merge_b, w_out, norm2_w, w_router, router_bias, w_exp_gate, w_exp_up, w_exp_down, final_norm_w):
    bp, sp, _ = x_prompt.shape
    bs, ss, _ = x_sample.shape

    mod = _adaln(jnp.concatenate([c_prompt, c_sample], axis=0), w_ada, b_ada)
    mod_p = mod[:, :bp].reshape(DEPTH, bp, 6, D)
    mod_s = mod[:, bp:].reshape(DEPTH, bs, 6, D)

    w_main, w_tail = _prep_in_weights(w_in)
    gate_up = jnp.concatenate(
        [jnp.zeros((DEPTH, TAIL_GLR0, GLA_KD), f32), gla_gate_up,
         jnp.zeros((DEPTH, LANES - TAIL_GLR0 - GLA_RANK, GLA_KD), f32)], axis=1).astype(bf16)
    in_w = (norm1_w.reshape(DEPTH, 1, D), w_main, w_tail, gate_up, gla_gate_b.reshape(DEPTH, 1, GLA_KD),
            conv_w, conv_b.reshape(DEPTH, 1, CONV_DIM), merge_b.reshape(DEPTH, 1, 2 * D))
    bigsel, e8 = _ssd_consts()
    ssd_w = (_pad_lanes(dt_bias).reshape(DEPTH, 1, LANES), _pad_lanes(a_log).reshape(DEPTH, 1, LANES),
             jnp.repeat(d_skip, SSD_HD, axis=-1).reshape(DEPTH, SSD_NG, 1, SSD_GW),
             ssd_norm_w.reshape(DEPTH, SSD_NG, 1, SSD_GW),
             w_ssd_br.astype(bf16).reshape(DEPTH, SSD_NG, SSD_GW, D), bigsel, e8)
    gla_w = (gla_norm_w.reshape(DEPTH, 1, GLA_DV), w_gla_br.astype(bf16))
    w_out_b = w_out.astype(bf16)
    n2 = norm2_w.reshape(DEPTH, 1, D)
    wg_b = w_exp_gate.astype(bf16)
    wu_b = w_exp_up.astype(bf16)
    wd_b = w_exp_down.astype(bf16)
    wr = _pad_lanes(w_router)
    wr_hi = wr.astype(bf16)
    wr_lo = (wr - wr_hi.astype(f32)).astype(bf16)
    rb = router_bias.reshape(N_EXP, 1)

    xp = x_prompt.reshape(bp * sp, D)
    xs = x_sample
    ssm_s = state_ssm.reshape(DEPTH, bs, SSD_NG, SSD_GW, SSD_NS)
    ssd_lp = min(128, sp)
    gla_lp = min(64, sp)
    cp = sp_ = gp = cs_ = ss_ = gs = None
    for l in range(DEPTH):
        z4, xs4, bc, qkvr, gate, tail, lg, cp = _inproj(l, xp, mod_p, None, cp, in_w, seq3d=False)
        bra, sp_ = _ssd(l, xs4, bc, z4, tail, None, sp_, ssd_w, nseq=bp, L=ssd_lp, sb=1, nchunks=sp // ssd_lp)
        brb, gp = _gla(l, qkvr, lg, None, gp, gla_w, nseq=bp, L=gla_lp, C=16, sb=1, nchunks=sp // gla_lp)
        x1, h2, comb = _merge(l, xp, bra, brb, gate, mod_p, w_out_b, n2, wr_hi, wr_lo, rb, seq3d=False)
        xp = _moe(l, h2, comb, wg_b, wu_b, wd_b, x1, mod_p, seq3d=False)

        z4, xs4, bc, qkvr, gate, tail, lg, cs_ = _inproj(l, xs, mod_s, state_conv, cs_, in_w, seq3d=True)
        bra, ss_ = _ssd(l, xs4, bc, z4, tail, ssm_s, ss_, ssd_w, nseq=bs, L=ss, sb=4, nchunks=1)
        brb, gs = _gla(l, qkvr, lg, state_gla, gs, gla_w, nseq=bs, L=ss, C=ss, sb=4, nchunks=1)
        x1, h2, comb = _merge(l, xs, bra, brb, gate, mod_s, w_out_b, n2, wr_hi, wr_lo, rb, seq3d=True)
        xs = _moe(l, h2, comb, wg_b, wu_b, wd_b, x1, mod_s, seq3d=True)

    fw = final_norm_w.reshape(1, D)
    y_prompt = _final_norm(xp, fw).reshape(bp, sp, D)
    y_sample = _final_norm(xs.reshape(bs * ss, D), fw).reshape(bs, ss, D)
    return (y_prompt, y_sample, cp, sp_.reshape(DEPTH, bp, SSD_NH, SSD_HD, SSD_NS), gp,
            cs_, ss_.reshape(DEPTH, bs, SSD_NH, SSD_HD, SSD_NS), gs)
```

```python
import functools

import numpy as np
import jax
import jax.numpy as jnp
from jax import lax
from jax.experimental import pallas as pl
from jax.experimental.pallas import tpu as pltpu

f32 = jnp.float32
bf16 = jnp.bfloat16

D = 1024
DEPTH = 4
SSD_INNER = 2048
SSD_HD = 64
SSD_NH = 32
SSD_NS = 128
SSD_NG = 4
SSD_HPG = 8
SSD_GW = SSD_INNER // SSD_NG
CONV_K = 4
CONV_DIM = SSD_INNER + 2 * SSD_NG * SSD_NS
GLA_NH = 4
GLA_DK = 128
GLA_DV = 256
GLA_KD = GLA_NH * GLA_DK
GLA_VD = GLA_NH * GLA_DV
GLA_RANK = 16
GLA_TAU = 16.0
N_EXP = 16
N_EGRP = 4
EXP_PER_GRP = 4
EXP_FF = 512
N_PAIR = 6
N_CLS = N_EGRP * N_PAIR
CLS_PAD = 32
MOE_EXT = 128
MOE_R = 256
MOE_TM = 512
EPS = 1e-6
LANES = 128
TAIL_GLR0 = SSD_NH
MAIN_W = SSD_INNER + CONV_DIM + 2 * GLA_KD + 2 * GLA_VD + 2 * D
VMEM_LIMIT = 56 * 1024 * 1024


def _sigmoid(x):
    return 1.0 / (1.0 + jnp.exp(-x))


def _silu(x):
    return x * _sigmoid(x)


def _softplus(x):
    return jnp.maximum(x, 0.0) + jnp.log1p(jnp.exp(-jnp.abs(x)))


def _log_sigmoid(x):
    return jnp.minimum(x, 0.0) - jnp.log1p(jnp.exp(-jnp.abs(x)))


def _dot(a, b):
    return jnp.dot(a, b, preferred_element_type=f32)


def _dot_nt(a, b):
    return lax.dot_general(a, b, (((1,), (1,)), ((), ())), preferred_element_type=f32)


def _dot_tn(a, b):
    return lax.dot_general(a, b, (((0,), (0,)), ((), ())), preferred_element_type=f32)


def _split2(x):
    hi = x.astype(bf16)
    return hi, (x - hi.astype(f32)).astype(bf16)


def _split3(x):
    hi = x.astype(bf16)
    r1 = x - hi.astype(f32)
    mid = r1.astype(bf16)
    lo = (r1 - mid.astype(f32)).astype(bf16)
    return hi, mid, lo


def _cumsum_rows(x, tri):
    hi, mid, lo = _split3(x)
    return _dot(tri, hi) + _dot(tri, mid) + _dot(tri, lo)


def _tri(n):
    r = lax.broadcasted_iota(jnp.int32, (n, n), 0)
    c = lax.broadcasted_iota(jnp.int32, (n, n), 1)
    return r >= c


def _norm_mod(x, mod_ref, w_ref, i_shift, i_scale, seq3d):
    ms = jnp.mean(x * x, axis=-1, keepdims=True)
    y = x * lax.rsqrt(ms + EPS) * w_ref[...]
    if seq3d:
        sc = mod_ref[:, i_scale:i_scale + 1, :]
        sh = mod_ref[:, i_shift:i_shift + 1, :]
        h = y * (1.0 + sc) + sh
        return h.reshape(h.shape[0] * h.shape[1], h.shape[2])
    sc = mod_ref[i_scale:i_scale + 1, :]
    sh = mod_ref[i_shift:i_shift + 1, :]
    return y * (1.0 + sc) + sh


def _seq_tile(nseq, want):
    return want if nseq % want == 0 else nseq


def _stacked_out(prev, in_specs, args, aliases, out_index):
    if prev is not None:
        in_specs.append(pl.BlockSpec(memory_space=pl.ANY))
        args.append(prev)
        aliases[len(args) - 1] = out_index


def _adaln_kernel(c_ref, w_ref, b_ref, o_ref):
    s = _silu(c_ref[...]).astype(bf16)
    o_ref[...] = _dot(s, w_ref[...].astype(bf16)) + b_ref[...]


def _adaln(c_all, w_ada, b_ada):
    n = c_all.shape[0]
    tn = 1024
    return pl.pallas_call(
        _adaln_kernel,
        grid=(DEPTH, 6 * D // tn),
        in_specs=[
            pl.BlockSpec((n, D), lambda l, j: (0, 0)),
            pl.BlockSpec((None, D, tn), lambda l, j: (l, 0, j)),
            pl.BlockSpec((None, 1, tn), lambda l, j: (l, 0, j)),
        ],
        out_specs=pl.BlockSpec((None, n, tn), lambda l, j: (l, 0, j)),
        out_shape=jax.ShapeDtypeStruct((DEPTH, n, 6 * D), f32),
        compiler_params=pltpu.CompilerParams(
            dimension_semantics=("arbitrary", "arbitrary"), vmem_limit_bytes=VMEM_LIMIT),
        name="adaln",
    )(c_all, w_ada, b_ada.reshape(DEPTH, 1, 6 * D))


IN_TN = 1024
IN_NJ = MAIN_W // IN_TN
IN_SUB = 256
IN_NSUB = IN_TN // IN_SUB
CONV_PAD = 8
KPREV = CONV_K - 1


def _inproj_kernel(*refs, seq3d, has_state, per_seq, T):
    if has_state:
        (x_ref, mod_ref, n1_ref, wm_ref, wt_ref, wg_ref, gb_ref, cw_ref, cb_ref, mb_ref, conv0_ref) = refs[:11]
        rest = refs[11:]
    else:
        (x_ref, mod_ref, n1_ref, wm_ref, wt_ref, wg_ref, gb_ref, cw_ref, cb_ref, mb_ref) = refs[:10]
        conv0_ref = None
        rest = refs[10:]
    (z4_ref, xs4_ref, bc_ref, qkvr_ref, gate_ref, tail_ref, lg_ref, convn_ref,
     h_scr, cscr, carry) = rest[-11:]
    i = pl.program_id(0)
    j = pl.program_id(1)
    tm = h_scr.shape[0]
    ns = cscr.shape[1]

    @pl.when(j == 0)
    def _():
        h = _norm_mod(x_ref[...], mod_ref, n1_ref, 0, 1, seq3d).astype(bf16)
        h_scr[...] = h
        t = _dot(h, wt_ref[...])
        tail_ref[...] = t
        pre = _dot(t.astype(bf16), wg_ref[...]) + gb_ref[...]
        lg_ref[...] = _log_sigmoid(pre) * (1.0 / GLA_TAU)

    def sub_dot(c):
        return _dot(h_scr[...], wm_ref[:, IN_SUB * c:IN_SUB * (c + 1)])

    def conv_silu(a, cj, c):
        cols = slice(IN_SUB * c, IN_SUB * (c + 1))
        a3 = a.reshape(ns, T, IN_SUB)
        if has_state:
            prev = conv0_ref[:, :, cols]
        else:
            prev = jnp.where(i % per_seq == 0, 0.0, carry[cj, c, CONV_PAD - KPREV:CONV_PAD, :])[None]
        cscr[c, :, CONV_PAD - KPREV:CONV_PAD, :] = prev
        cscr[c, :, CONV_PAD:CONV_PAD + T, :] = a3
        out = cb_ref[:, cols] + a3 * cw_ref[KPREV:KPREV + 1, cols]
        for k in range(KPREV):
            out = out + cscr[c, :, CONV_PAD - KPREV + k:CONV_PAD - KPREV + k + T, :] * cw_ref[k:k + 1, cols]
        new_tail = cscr[c, :, CONV_PAD + T - KPREV:CONV_PAD + T, :]
        convn_ref[:, :, IN_TN * cj + IN_SUB * c:IN_TN * cj + IN_SUB * (c + 1)] = new_tail
        if not has_state:
            carry[cj, c, CONV_PAD - KPREV:CONV_PAD, :] = new_tail[0]
        return _silu(out).reshape(tm, IN_SUB).astype(bf16)

    per_grp = SSD_GW // IN_SUB

    @pl.when(j < 2)
    def _():
        for c in range(IN_NSUB):
            lanes = slice(IN_SUB * (c % per_grp), IN_SUB * (c % per_grp + 1))
            z4_ref[c // per_grp, :, lanes] = _silu(sub_dot(c)).astype(bf16)

    for cj in range(2):
        @pl.when(j == 2 + cj)
        def _(cj=cj):
            for c in range(IN_NSUB):
                lanes = slice(IN_SUB * (c % per_grp), IN_SUB * (c % per_grp + 1))
                xs4_ref[c // per_grp, :, lanes] = conv_silu(sub_dot(c), cj, c)

    @pl.when(j == 4)
    def _():
        for c in range(IN_NSUB):
            xc = conv_silu(sub_dot(c), 2, c)
            for q in range(IN_SUB // SSD_NS):
                bc_ref[c * (IN_SUB // SSD_NS) + q] = xc[:, SSD_NS * q:SSD_NS * (q + 1)]

    @pl.when(jnp.logical_and(j >= 5, j < 7))
    def _():
        for c in range(IN_NSUB):
            qkvr_ref[:, IN_SUB * c:IN_SUB * (c + 1)] = sub_dot(c).astype(bf16)

    @pl.when(j == 7)
    def _():
        for c in range(IN_NSUB):
            qkvr_ref[:, IN_SUB * c:IN_SUB * (c + 1)] = _silu(sub_dot(c)).astype(bf16)

    @pl.when(j >= 8)
    def _():
        for c in range(IN_NSUB):
            cols = slice(IN_SUB * c, IN_SUB * (c + 1))
            gate_ref[:, cols] = _sigmoid(sub_dot(c) + mb_ref[:, cols]).astype(bf16)


def _inproj(l, x, mod, conv0, convn_prev, w, *, seq3d):
    (n1w, w_main, w_tail, gate_up, gate_b, conv_w, conv_b, merge_b) = w
    has_state = conv0 is not None
    if seq3d:
        nseq, T = x.shape[0], x.shape[1]
        ns = _seq_tile(nseq, 64)
        ntok = nseq * T
        tm = ns * T
        nt = nseq // ns
        per_seq = 1
        x_spec = pl.BlockSpec((ns, T, D), lambda i, j: (i, 0, 0))
        mod_spec = pl.BlockSpec((None, ns, 6, D), lambda i, j: (l, i, 0, 0))
        seq_blk = lambda i: i
    else:
        nseq = mod.shape[1]
        ntok = x.shape[0]
        tm = min(1024, ntok // nseq)
        T = tm
        ns = 1
        nt = ntok // tm
        per_seq = ntok // nseq // tm
        x_spec = pl.BlockSpec((tm, D), lambda i, j: (i, 0))
        mod_spec = pl.BlockSpec((None, None, 6, D), lambda i, j: (l, i // per_seq, 0, 0))
        seq_blk = lambda i: i // per_seq

    def cj(j):
        return jnp.clip(j - 2, 0, 2)

    in_specs = [
        x_spec,
        mod_spec,
        pl.BlockSpec((None, 1, D), lambda i, j: (l, 0, 0)),
        pl.BlockSpec((None, D, IN_TN), lambda i, j: (l, 0, j)),
        pl.BlockSpec((None, D, LANES), lambda i, j: (l, 0, 0)),
        pl.BlockSpec((None, LANES, GLA_KD), lambda i, j: (l, 0, 0)),
        pl.BlockSpec((None, 1, GLA_KD), lambda i, j: (l, 0, 0)),
        pl.BlockSpec((None, CONV_K, IN_TN), lambda i, j: (l, 0, cj(j))),
        pl.BlockSpec((None, 1, IN_TN), lambda i, j: (l, 0, cj(j))),
        pl.BlockSpec((None, 1, IN_TN), lambda i, j: (l, 0, jnp.clip(j - 8, 0, 1))),
    ]
    args = [x, mod, n1w, w_main, w_tail, gate_up, gate_b, conv_w, conv_b, merge_b]
    if has_state:
        in_specs.append(pl.BlockSpec((None, ns, KPREV, IN_TN), lambda i, j: (l, i, 0, cj(j))))
        args.append(conv0)
    aliases = {}
    _stacked_out(convn_prev, in_specs, args, aliases, 7)
    out_shapes = (
        jax.ShapeDtypeStruct((SSD_NG, ntok, SSD_GW), bf16),
        jax.ShapeDtypeStruct((SSD_NG, ntok, SSD_GW), bf16),
        jax.ShapeDtypeStruct((2 * SSD_NG, ntok, SSD_NS), bf16),
        jax.ShapeDtypeStruct((ntok, 2 * GLA_KD + 2 * GLA_VD), bf16),
        jax.ShapeDtypeStruct((ntok, 2 * D), bf16),
        jax.ShapeDtypeStruct((ntok, LANES), f32),
        jax.ShapeDtypeStruct((ntok, GLA_KD), f32),
        jax.ShapeDtypeStruct((DEPTH, nseq, KPREV, CONV_DIM), f32),
    )
    out_specs = (
        pl.BlockSpec((2, tm, SSD_GW), lambda i, j: (jnp.clip(j, 0, 1), i, 0)),
        pl.BlockSpec((2, tm, SSD_GW), lambda i, j: (jnp.clip(j - 2, 0, 1), i, 0)),
        pl.BlockSpec((2 * SSD_NG, tm, SSD_NS), lambda i, j: (0, i, 0)),
        pl.BlockSpec((tm, IN_TN), lambda i, j: (i, jnp.clip(j - 5, 0, 2))),
        pl.BlockSpec((tm, IN_TN), lambda i, j: (i, jnp.clip(j - 8, 0, 1))),
        pl.BlockSpec((tm, LANES), lambda i, j: (i, 0)),
        pl.BlockSpec((tm, GLA_KD), lambda i, j: (i, 0)),
        pl.BlockSpec((None, ns, KPREV, CONV_DIM), lambda i, j: (l, seq_blk(i), 0, 0)),
    )
    return pl.pallas_call(
        functools.partial(_inproj_kernel, seq3d=seq3d, has_state=has_state, per_seq=per_seq, T=T),
        grid=(nt, IN_NJ),
        in_specs=in_specs,
        out_specs=out_specs,
        out_shape=out_shapes,
        input_output_aliases=aliases,
        scratch_shapes=[
            pltpu.VMEM((tm, D), bf16),
            pltpu.VMEM((IN_NSUB, ns, CONV_PAD + T, IN_SUB), f32),
            pltpu.VMEM((3, IN_NSUB, CONV_PAD, IN_SUB), f32),
        ],
        compiler_params=pltpu.CompilerParams(
            dimension_semantics=("arbitrary", "arbitrary"), vmem_limit_bytes=VMEM_LIMIT),
        name="inproj",
    )(*args)


def _ssd_consts():
    bigsel = np.zeros((SSD_NG, 3 * LANES, SSD_HPG * LANES), np.float32)
    e8 = np.zeros((SSD_NG, 2 * LANES, SSD_GW), np.float32)
    for g in range(SSD_NG):
        for j in range(SSD_HPG):
            for k in range(3):
                bigsel[g, k * LANES + SSD_HPG * g + j, LANES * j:LANES * (j + 1)] = 1.0
            for k in range(2):
                e8[g, k * LANES + SSD_HPG * g + j, SSD_HD * j:SSD_HD * (j + 1)] = 1.0
    return jnp.asarray(bigsel, bf16), jnp.asarray(e8, bf16)


def _ssd_kernel(*refs, L, sb, nchunks, has_state):
    R = sb * L
    xs4_ref, bc_ref, z4_ref, tail_ref = refs[:4]
    k = 4
    ssm0_ref = None
    if has_state:
        ssm0_ref = refs[k]
        k += 1
    dtb_ref, alog_ref, dsk_ref, nw_ref, wbr_ref, bigsel_ref, e8_ref = refs[k:k + 7]
    br_ref, ssmn_ref, cst_scr = refs[-3:]
    c = pl.program_id(1)

    rowi = lax.broadcasted_iota(jnp.int32, (R, R), 0)
    coli = lax.broadcasted_iota(jnp.int32, (R, R), 1)
    mask = rowi >= coli
    if sb > 1:
        sh = L.bit_length() - 1
        mask = jnp.logical_and(mask, (rowi >> sh) == (coli >> sh))
    tri = mask.astype(bf16)
    lane = lax.broadcasted_iota(jnp.int32, (R, LANES), 1)
    lo_half = lane < SSD_HD

    def init():
        if has_state:
            ssmn_ref[...] = ssm0_ref[...]
        else:
            ssmn_ref[...] = jnp.zeros(ssmn_ref.shape, f32)

    if nchunks == 1:
        init()
    else:
        pl.when(c == 0)(init)

    dtp = _softplus(tail_ref[...] + dtb_ref[...])
    a = -jnp.exp(alog_ref[...])
    cs = _cumsum_rows(dtp * a, tri)
    cst_scr[...] = cs.T
    h3 = jnp.concatenate(_split3(cs), axis=1)
    d2 = jnp.concatenate(_split2(dtp), axis=1)
    br_ref[...] = jnp.zeros(br_ref.shape, f32)

    def seq_last(x):
        if sb == 1:
            return x[R - 1:R, :]
        x3 = x.reshape(sb, L, x.shape[-1])
        return jnp.broadcast_to(x3[:, L - 1:L, :], x3.shape).reshape(x.shape)

    def group_body(g, carry):
        cm = _dot(h3, bigsel_ref[g])
        dt_exp = _dot(d2, e8_ref[g])
        cs_exp = jnp.concatenate(
            [jnp.where(lo_half, cm[:, 2 * LANES * i:2 * LANES * i + LANES],
                       cm[:, 2 * LANES * i + LANES:2 * LANES * (i + 1)]) for i in range(SSD_HPG // 2)], axis=1)
        ecs = jnp.exp(cs_exp)
        ce = seq_last(cs_exp)
        xs = xs4_ref[g].astype(f32)
        xdt = xs * dt_exp
        xdt_b = xdt.astype(bf16)
        xse = xdt * jnp.exp(ce - cs_exp)
        bg = bc_ref[g]
        cg = bc_ref[SSD_NG + g]
        cb = _dot_nt(cg, bg)

        pairs = []
        for i in range(SSD_HPG // 2):
            ws = []
            for j in (2 * i, 2 * i + 1):
                row = cst_scr[pl.ds(SSD_HPG * g + j, 1), :]
                seg = cm[:, LANES * j:LANES * j + R] - row
                dec = jnp.exp(jnp.where(mask, seg, -jnp.inf))
                ws.append((cb * dec).astype(bf16))
            yy = _dot(jnp.concatenate(ws, axis=0), xdt_b[:, LANES * i:LANES * (i + 1)])
            pairs.append(jnp.where(lo_half, yy[:R], yy[R:]))
        y = jnp.concatenate(pairs, axis=1)

        if sb > 1:
            bgf = bg.astype(f32)
            cgf = cg.astype(f32)
        ys_parts = []
        for s in range(sb):
            rs = slice(s * L, (s + 1) * L)
            st = ssmn_ref[s, g]
            b_s = bg if sb == 1 else bgf[rs].astype(bf16)
            c_s = cg if sb == 1 else cgf[rs].astype(bf16)
            ys_parts.append(_dot_nt(c_s, st.astype(bf16)))
            upd = _dot_tn(xse[rs].astype(bf16), b_s)
            e_end = jnp.exp(ce[s * L:s * L + 1, :])
            ssmn_ref[s, g] = jnp.concatenate(
                [st[SSD_HD * j:SSD_HD * (j + 1)] * e_end[:, SSD_HD * j:SSD_HD * j + 1]
                 + upd[SSD_HD * j:SSD_HD * (j + 1)] for j in range(SSD_HPG)], axis=0)
        ys = ys_parts[0] if sb == 1 else jnp.concatenate(ys_parts, axis=0)
        y = y + ys * ecs + dsk_ref[g] * xs

        yg = y * z4_ref[g].astype(f32)
        yn = yg * lax.rsqrt(jnp.mean(yg * yg, axis=-1, keepdims=True) + EPS) * nw_ref[g]
        br_ref[...] += _dot(yn.astype(bf16), wbr_ref[g])
        return carry

    for g in range(SSD_NG):
        group_body(g, 0)


def _ssd(l, xs4, bc, z4, tail, ssm0, ssmn_prev, w, *, nseq, L, sb, nchunks):
    has_state = ssm0 is not None
    dtb, alog, dsk, nw, wbr, bigsel, e8 = w
    ntok = tail.shape[0]
    R = sb * L
    if nchunks == 1:
        rblk = lambda b, c: b
    else:
        rblk = lambda b, c: b * nchunks + c
    in_specs = [
        pl.BlockSpec((SSD_NG, R, SSD_GW), lambda b, c: (0, rblk(b, c), 0)),
        pl.BlockSpec((2 * SSD_NG, R, SSD_NS), lambda b, c: (0, rblk(b, c), 0)),
        pl.BlockSpec((SSD_NG, R, SSD_GW), lambda b, c: (0, rblk(b, c), 0)),
        pl.BlockSpec((R, LANES), lambda b, c: (rblk(b, c), 0)),
    ]
    args = [xs4, bc, z4, tail]
    if has_state:
        in_specs.append(pl.BlockSpec((None, sb, SSD_NG, SSD_GW, SSD_NS), lambda b, c: (l, b, 0, 0, 0)))
        args.append(ssm0)
    in_specs += [
        pl.BlockSpec((None, 1, LANES), lambda b, c: (l, 0, 0)),
        pl.BlockSpec((None, 1, LANES), lambda b, c: (l, 0, 0)),
        pl.BlockSpec((None, SSD_NG, 1, SSD_GW), lambda b, c: (l, 0, 0, 0)),
        pl.BlockSpec((None, SSD_NG, 1, SSD_GW), lambda b, c: (l, 0, 0, 0)),
        pl.BlockSpec((None, SSD_NG, SSD_GW, D), lambda b, c: (l, 0, 0, 0)),
        pl.BlockSpec((SSD_NG, 3 * LANES, SSD_HPG * LANES), lambda b, c: (0, 0, 0)),
        pl.BlockSpec((SSD_NG, 2 * LANES, SSD_GW), lambda b, c: (0, 0, 0)),
    ]
    args += [dtb, alog, dsk, nw, wbr, bigsel, e8]
    aliases = {}
    _stacked_out(ssmn_prev, in_specs, args, aliases, 1)
    return pl.pallas_call(
        functools.partial(_ssd_kernel, L=L, sb=sb, nchunks=nchunks, has_state=has_state),
        grid=(nseq // sb, nchunks),
        in_specs=in_specs,
        out_specs=(
            pl.BlockSpec((R, D), lambda b, c: (rblk(b, c), 0)),
            pl.BlockSpec((None, sb, SSD_NG, SSD_GW, SSD_NS), lambda b, c: (l, b, 0, 0, 0)),
        ),
        out_shape=(
            jax.ShapeDtypeStruct((ntok, D), f32),
            jax.ShapeDtypeStruct((DEPTH, nseq, SSD_NG, SSD_GW, SSD_NS), f32),
        ),
        input_output_aliases=aliases,
        scratch_shapes=[pltpu.VMEM((LANES, R), f32)],
        compiler_params=pltpu.CompilerParams(
            dimension_semantics=("arbitrary", "arbitrary"), vmem_limit_bytes=VMEM_LIMIT),
        name="ssd",
    )(*args)


GLA_PAD = 16


def _gla_kernel(*refs, L, C, sb, nchunks, has_state):
    qkvr_ref, lg_ref = refs[:2]
    k = 2
    gla0_ref = None
    if has_state:
        gla0_ref = refs[k]
        k += 1
    gnw_ref, wbr_ref = refs[k:k + 2]
    br_ref, glan_ref, kpad, apad, qf = refs[-5:]
    c = pl.program_id(1)
    nb = L // C
    qf[...] = qkvr_ref[...].astype(f32)
    tri = _tri(L).astype(bf16)
    rowi = lax.broadcasted_iota(jnp.int32, (L, L), 0)
    coli = lax.broadcasted_iota(jnp.int32, (L, L), 1)
    same_blk_off = rowi & (C - 1)
    rowl = lax.broadcasted_iota(jnp.int32, (L, 1), 0)

    kpad[0:GLA_PAD, :] = jnp.zeros((GLA_PAD, GLA_KD), f32)
    apad[0:GLA_PAD, :] = jnp.zeros((GLA_PAD, GLA_KD), f32)

    def seq_body(s, carry):
        r0 = pl.multiple_of(s * L, 8)
        rows = pl.ds(r0, L)

        def init():
            if has_state:
                glan_ref[s] = gla0_ref[s]
            else:
                glan_ref[s] = jnp.zeros(glan_ref.shape[1:], f32)

        if nchunks == 1:
            init()
        else:
            pl.when(c == 0)(init)

        lg = lg_ref[rows, :]
        g = _cumsum_rows(lg, tri)
        g_last = g[L - 1:L, :]
        q = qf[rows, 0:GLA_KD]
        k_ = qf[rows, GLA_KD:2 * GLA_KD]
        kpad[GLA_PAD:GLA_PAD + L, :] = k_
        apad[GLA_PAD:GLA_PAD + L, :] = jnp.exp(lg)

        acc = jnp.zeros((L, D), f32)
        for h in range(GLA_NH):
            kl = slice(GLA_DK * h, GLA_DK * (h + 1))
            vl = slice(2 * GLA_KD + GLA_DV * h, 2 * GLA_KD + GLA_DV * (h + 1))
            rl = slice(2 * GLA_KD + GLA_VD + GLA_DV * h, 2 * GLA_KD + GLA_VD + GLA_DV * (h + 1))
            gh = g[:, kl]
            qh = q[:, kl]
            kh = k_[:, kl]
            vh = qf[rows, vl].astype(bf16)
            st = glan_ref[s, h]
            o = _dot((qh * jnp.exp(gh)).astype(bf16), st.astype(bf16))

            gbnd = [jnp.zeros((1, GLA_DK), f32)] + [gh[C * i - 1:C * i, :] for i in range(1, nb)]
            gblk = jnp.concatenate([jnp.broadcast_to(b, (C, GLA_DK)) for b in gbnd], axis=0)
            qt = (qh * jnp.exp(gh - gblk)).astype(bf16)
            blocks = [jnp.zeros((C, L), f32)]
            for i in range(1, nb):
                kt = kh * jnp.exp(jnp.where(rowl < C * i, gbnd[i] - gh, -jnp.inf))
                blocks.append(_dot_nt(qt[C * i:C * (i + 1), :], kt.astype(bf16)))
            att = jnp.concatenate(blocks, axis=0) if nb > 1 else blocks[0]

            e = None
            for d in range(C):
                ks = kpad[GLA_PAD - d:GLA_PAD - d + L, kl]
                if d == 0:
                    t = qh * ks
                else:
                    a_sh = apad[GLA_PAD - (d - 1):GLA_PAD - (d - 1) + L, kl]
                    e = a_sh if e is None else e * a_sh
                    t = qh * e * ks
                band = jnp.sum(t, axis=-1, keepdims=True)
                hit = jnp.logical_and(same_blk_off >= d, coli == rowi - d)
                att = att + jnp.where(hit, band, 0.0)
            o = o + _dot(att.astype(bf16), vh)

            kd = (kh * jnp.exp(g_last[:, kl] - gh)).astype(bf16)
            e_col = jnp.exp(jnp.broadcast_to(g_last[:, kl], (8, GLA_DK))).T[:, 0:1]
            glan_ref[s, h] = st * e_col + _dot_tn(kd, vh)

            on = o * lax.rsqrt(jnp.mean(o * o, axis=-1, keepdims=True) + EPS) * gnw_ref[...]
            og = on * qf[rows, rl]
            acc = acc + _dot(og.astype(bf16), wbr_ref[GLA_DV * h:GLA_DV * (h + 1), :])
        br_ref[rows, :] = acc
        return carry

    lax.fori_loop(0, sb, seq_body, 0)


def _gla(l, qkvr, lg, gla0, glan_prev, w, *, nseq, L, C, sb, nchunks):
    has_state = gla0 is not None
    gnw, wbr = w
    ntok = qkvr.shape[0]
    rb = sb * L
    if nchunks == 1:
        row = lambda b, c: (b, 0)
    else:
        row = lambda b, c: (b * nchunks + c, 0)
    in_specs = [
        pl.BlockSpec((rb, 2 * GLA_KD + 2 * GLA_VD), row),
        pl.BlockSpec((rb, GLA_KD), row),
    ]
    args = [qkvr, lg]
    if has_state:
        in_specs.append(pl.BlockSpec((None, sb, GLA_NH, GLA_DK, GLA_DV), lambda b, c: (l, b, 0, 0, 0)))
        args.append(gla0)
    in_specs += [
        pl.BlockSpec((None, 1, GLA_DV), lambda b, c: (l, 0, 0)),
        pl.BlockSpec((None, GLA_VD, D), lambda b, c: (l, 0, 0)),
    ]
    args += [gnw, wbr]
    aliases = {}
    _stacked_out(glan_prev, in_specs, args, aliases, 1)
    return pl.pallas_call(
        functools.partial(_gla_kernel, L=L, C=C, sb=sb, nchunks=nchunks, has_state=has_state),
        grid=(nseq // sb, nchunks),
        in_specs=in_specs,
        out_specs=(
            pl.BlockSpec((rb, D), row),
            pl.BlockSpec((None, sb, GLA_NH, GLA_DK, GLA_DV), lambda b, c: (l, b, 0, 0, 0)),
        ),
        out_shape=(
            jax.ShapeDtypeStruct((ntok, D), f32),
            jax.ShapeDtypeStruct((DEPTH, nseq, GLA_NH, GLA_DK, GLA_DV), f32),
        ),
        input_output_aliases=aliases,
        scratch_shapes=[
            pltpu.VMEM((GLA_PAD + L, GLA_KD), f32),
            pltpu.VMEM((GLA_PAD + L, GLA_KD), f32),
            pltpu.VMEM((rb, 2 * GLA_KD + 2 * GLA_VD), f32),
        ],
        compiler_params=pltpu.CompilerParams(
            dimension_semantics=("arbitrary", "arbitrary"), vmem_limit_bytes=VMEM_LIMIT),
        name="gla",
    )(*args)


def _top2_sum(a, b, c, d):
    hi1, lo1 = jnp.maximum(a, b), jnp.minimum(a, b)
    hi2, lo2 = jnp.maximum(c, d), jnp.minimum(c, d)
    return jnp.maximum(hi1, hi2) + jnp.maximum(jnp.minimum(hi1, hi2), jnp.maximum(lo1, lo2))


def _route_rows(sig, biased):
    gsc = [_top2_sum(*biased[EXP_PER_GRP * g:EXP_PER_GRP * (g + 1)]) for g in range(N_EGRP)]
    best = jnp.zeros_like(gsc[0], dtype=jnp.int32)
    m = gsc[0]
    for g in range(1, N_EGRP):
        better = gsc[g] > m
        best = jnp.where(better, g, best)
        m = jnp.where(better, gsc[g], m)
    masked = [jnp.where(best == (e // EXP_PER_GRP), biased[e], -jnp.inf) for e in range(N_EXP)]

    def first_argmax(vals):
        idx = jnp.zeros_like(best)
        mx = vals[0]
        for e in range(1, N_EXP):
            better = vals[e] > mx
            idx = jnp.where(better, e, idx)
            mx = jnp.where(better, vals[e], mx)
        return idx

    i1 = first_argmax(masked)
    i2 = first_argmax([jnp.where(i1 == e, -jnp.inf, masked[e]) for e in range(N_EXP)])
    w1 = sum(jnp.where(i1 == e, sig[e], 0.0) for e in range(N_EXP))
    w2 = sum(jnp.where(i2 == e, sig[e], 0.0) for e in range(N_EXP))
    den = w1 + w2
    return best, i1, i2, w1 / den, w2 / den


def _merge_kernel(*refs, seq3d, sparse):
    (x_ref, bra_ref, brb_ref, gate_ref, mod_ref, wo_ref, n2_ref, wr_hi_ref, wr_lo_ref, rb_ref) = refs[:10]
    mixed_in = (gate_ref[:, 0:D].astype(f32) * bra_ref[...]
                + gate_ref[:, D:2 * D].astype(f32) * brb_ref[...])
    mixed = _dot(mixed_in.astype(bf16), wo_ref[...])
    x = x_ref[...]
    if seq3d:
        x1 = x + mod_ref[:, 2:3, :] * mixed.reshape(x.shape)
    else:
        x1 = x + mod_ref[2:3, :] * mixed
    refs[10][...] = x1
    h2 = _norm_mod(x1, mod_ref, n2_ref, 3, 4, seq3d)
    hi, lo = _split2(h2)
    logits = _dot(hi, wr_hi_ref[...]) + (_dot(hi, wr_lo_ref[...]) + _dot(lo, wr_hi_ref[...]))
    tm = logits.shape[0]
    lt = logits.T
    sig_all = _sigmoid(lt[0:N_EXP, :])
    bias_all = sig_all + rb_ref[...]
    sig = [sig_all[e:e + 1, :] for e in range(N_EXP)]
    biased = [bias_all[e:e + 1, :] for e in range(N_EXP)]
    best, i1, i2, w1, w2 = _route_rows(sig, biased)
    if not sparse:
        h2_ref, comb_ref = refs[11:13]
        h2_ref[...] = h2.astype(bf16)
        comb = [jnp.where(i1 == e, w1, 0.0) + jnp.where(i2 == e, w2, 0.0) for e in range(N_EXP)]
        comb_t = jnp.concatenate(comb + [jnp.zeros((LANES - N_EXP, tm), f32)], axis=0)
        comb_ref[...] = comb_t.T
        return

    h2x_ref, plan_ref, counts_ref, base_scr = refs[11:15]
    swap = i1 > i2
    wa = jnp.where(swap, w2, w1)
    wb = jnp.where(swap, w1, w2)
    a = jnp.minimum(i1, i2) & (EXP_PER_GRP - 1)
    b = jnp.maximum(i1, i2) & (EXP_PER_GRP - 1)
    pair = jnp.where(a == 0, b - 1, jnp.where(a == 1, b + 1, 5))
    cls = best * N_PAIR + pair
    sub = lax.broadcasted_iota(jnp.int32, (CLS_PAD, tm), 0)
    onehot = (sub == cls).astype(f32)
    before = (lax.broadcasted_iota(jnp.int32, (tm, tm), 0)
              < lax.broadcasted_iota(jnp.int32, (tm, tm), 1)).astype(bf16)
    prefix = _dot(onehot.astype(bf16), before)

    @pl.when(pl.program_id(0) == 0)
    def _():
        base_scr[...] = jnp.zeros_like(base_scr)

    base = base_scr[...]
    rank = jnp.sum(onehot * (prefix + base[:, 0:1]), axis=0, keepdims=True)
    base = base + jnp.sum(onehot, axis=1, keepdims=True)
    base_scr[...] = base
    counts_ref[...] = base
    plan_ref[...] = jnp.concatenate(
        [cls, rank.astype(jnp.int32), jnp.zeros((6, tm), jnp.int32)], axis=0)
    ext_t = jnp.concatenate([wa, wb, jnp.zeros((MOE_EXT - 2, tm), f32)], axis=0)
    h2x_ref[:, 0:D] = h2
    h2x_ref[:, D:D + MOE_EXT] = ext_t.T


def _merge(l, x, bra, brb, gate, mod, wo, n2w, wr_hi, wr_lo, rb, *, seq3d, sparse):
    if seq3d:
        ns = _seq_tile(x.shape[0], 32)
        ntok = x.shape[0] * x.shape[1]
        tm = ns * x.shape[1]
        nt = x.shape[0] // ns
        x_spec = pl.BlockSpec((ns, x.shape[1], D), lambda i: (i, 0, 0))
        mod_spec = pl.BlockSpec((None, ns, 6, D), lambda i: (l, i, 0, 0))
    else:
        ntok = x.shape[0]
        nseq = mod.shape[1]
        tm = min(512, ntok // nseq)
        nt = ntok // tm
        per_seq = ntok // nseq // tm
        x_spec = pl.BlockSpec((tm, D), lambda i: (i, 0))
        mod_spec = pl.BlockSpec((None, None, 6, D), lambda i: (l, i // per_seq, 0, 0))
    row = lambda i: (i, 0)
    const = lambda i: (0, 0)
    if sparse:
        out_specs = (x_spec, pl.BlockSpec((tm, D + MOE_EXT), row), pl.BlockSpec((8, tm), lambda i: (0, i)),
                     pl.BlockSpec((CLS_PAD, LANES), const))
        out_shape = (jax.ShapeDtypeStruct(x.shape, f32), jax.ShapeDtypeStruct((ntok, D + MOE_EXT), f32),
                     jax.ShapeDtypeStruct((8, ntok), jnp.int32), jax.ShapeDtypeStruct((CLS_PAD, LANES), f32))
        scratch = [pltpu.VMEM((CLS_PAD, LANES), f32)]
    else:
        out_specs = (x_spec, pl.BlockSpec((tm, D), row), pl.BlockSpec((tm, LANES), row))
        out_shape = (jax.ShapeDtypeStruct(x.shape, f32), jax.ShapeDtypeStruct((ntok, D), bf16),
                     jax.ShapeDtypeStruct((ntok, LANES), f32))
        scratch = []
    return pl.pallas_call(
        functools.partial(_merge_kernel, seq3d=seq3d, sparse=sparse),
        grid=(nt,),
        in_specs=[
            x_spec,
            pl.BlockSpec((tm, D), row),
            pl.BlockSpec((tm, D), row),
            pl.BlockSpec((tm, 2 * D), row),
            mod_spec,
            pl.BlockSpec((None, D, D), lambda i: (l, 0, 0)),
            pl.BlockSpec((None, 1, D), lambda i: (l, 0, 0)),
            pl.BlockSpec((D, LANES), const),
            pl.BlockSpec((D, LANES), const),
            pl.BlockSpec((N_EXP, 1), const),
        ],
        out_specs=out_specs,
        out_shape=out_shape,
        scratch_shapes=scratch,
        compiler_params=pltpu.CompilerParams(
            dimension_semantics=("arbitrary",), vmem_limit_bytes=VMEM_LIMIT),
        name="merge",
    )(x, bra, brb, gate, mod, wo, n2w, wr_hi, wr_lo, rb)


def _dispatch_kernel(pos_ref, src_ref, buf_in, xs_hbm, sem):
    del buf_in
    tm = src_ref.shape[0]

    def issue(t, c):
        p = pos_ref[0, t]
        pltpu.make_async_copy(src_ref.at[pl.ds(t, 1), :], xs_hbm.at[pl.ds(p, 1), :], sem).start()
        return c

    lax.fori_loop(0, tm, issue, 0, unroll=8)

    def drain(t, c):
        pltpu.make_async_copy(src_ref.at[pl.ds(0, 1), :], xs_hbm.at[pl.ds(0, 1), :], sem).wait()
        return c

    lax.fori_loop(0, tm, drain, 0, unroll=8)


def _dispatch(pos3, h2x, xs_buf):
    nt, _, tm = pos3.shape
    return pl.pallas_call(
        _dispatch_kernel,
        grid=(nt,),
        in_specs=[
            pl.BlockSpec((None, 1, tm), lambda i: (i, 0, 0), memory_space=pltpu.SMEM),
            pl.BlockSpec((tm, D + MOE_EXT), lambda i: (i, 0)),
            pl.BlockSpec(memory_space=pl.ANY),
        ],
        out_specs=pl.BlockSpec(memory_space=pl.ANY),
        out_shape=jax.ShapeDtypeStruct(xs_buf.shape, f32),
        input_output_aliases={2: 0},
        scratch_shapes=[pltpu.SemaphoreType.DMA(())],
        compiler_params=pltpu.CompilerParams(dimension_semantics=("arbitrary",)),
        name="dispatch",
    )(pos3, h2x, xs_buf)


def _moe_sparse_kernel(src_ref, valid_ref, ea_ref, eb_ref, xs_ref,
                       wga_ref, wua_ref, wda_ref, wgb_ref, wub_ref, wdb_ref, ys_ref):
    del src_ref, ea_ref, eb_ref

    @pl.when(valid_ref[pl.program_id(0)] == 1)
    def _():
        x = xs_ref[...]
        h = x[:, 0:D].astype(bf16)

        def ffn(wg, wu, wd):
            t = (_silu(_dot(h, wg[...])) * _dot(h, wu[...])).astype(bf16)
            return _dot(t, wd[...])

        ys_ref[...] = (x[:, D:D + 1] * ffn(wga_ref, wua_ref, wda_ref)
                       + x[:, D + 1:D + 2] * ffn(wgb_ref, wub_ref, wdb_ref))

    @pl.when(valid_ref[pl.program_id(0)] == 0)
    def _():
        ys_ref[...] = jnp.zeros(ys_ref.shape, f32)


def _moe_sparse(l, tiles, xs, wg, wu, wd):
    src, valid, ea, eb = tiles
    nrow = xs.shape[0]
    ntile = nrow // MOE_R
    wa_map = lambda t, src, valid, ea, eb: (l, ea[t], 0, 0)
    wb_map = lambda t, src, valid, ea, eb: (l, eb[t], 0, 0)
    row_map = lambda t, src, valid, ea, eb: (src[t], 0)
    return pl.pallas_call(
        _moe_sparse_kernel,
        grid_spec=pltpu.PrefetchScalarGridSpec(
            num_scalar_prefetch=4,
            grid=(ntile,),
            in_specs=[
                pl.BlockSpec((MOE_R, D + MOE_EXT), row_map),
                pl.BlockSpec((None, None, D, EXP_FF), wa_map),
                pl.BlockSpec((None, None, D, EXP_FF), wa_map),
                pl.BlockSpec((None, None, EXP_FF, D), wa_map),
                pl.BlockSpec((None, None, D, EXP_FF), wb_map),
                pl.BlockSpec((None, None, D, EXP_FF), wb_map),
                pl.BlockSpec((None, None, EXP_FF, D), wb_map),
            ],
            out_specs=pl.BlockSpec((MOE_R, D), lambda t, src, valid, ea, eb: (t, 0)),
        ),
        out_shape=jax.ShapeDtypeStruct((nrow, D), f32),
        compiler_params=pltpu.CompilerParams(
            dimension_semantics=("arbitrary",), vmem_limit_bytes=VMEM_LIMIT),
        name="moe_sparse",
    )(src, valid, ea, eb, xs, wg, wu, wd, wg, wu, wd)


def _combine_kernel(pos_ref, ys_hbm, x1_ref, mod_ref, x2_ref, buf, sem):
    tm = x1_ref.shape[0]

    def issue(t, c):
        p = pos_ref[0, t]
        pltpu.make_async_copy(ys_hbm.at[pl.ds(p, 1), :], buf.at[pl.ds(t, 1), :], sem).start()
        return c

    lax.fori_loop(0, tm, issue, 0, unroll=8)

    def drain(t, c):
        pltpu.make_async_copy(ys_hbm.at[pl.ds(0, 1), :], buf.at[pl.ds(0, 1), :], sem).wait()
        return c

    lax.fori_loop(0, tm, drain, 0, unroll=8)
    x2_ref[...] = x1_ref[...] + mod_ref[5:6, :] * buf[...]


def _combine(l, pos3, ys, x1, mod):
    nt, _, tm = pos3.shape
    ntok = x1.shape[0]
    per_seq = ntok // mod.shape[1] // tm
    return pl.pallas_call(
        _combine_kernel,
        grid=(nt,),
        in_specs=[
            pl.BlockSpec((None, 1, tm), lambda i: (i, 0, 0), memory_space=pltpu.SMEM),
            pl.BlockSpec(memory_space=pl.ANY),
            pl.BlockSpec((tm, D), lambda i: (i, 0)),
            pl.BlockSpec((None, None, 6, D), lambda i: (l, i // per_seq, 0, 0)),
        ],
        out_specs=pl.BlockSpec((tm, D), lambda i: (i, 0)),
        out_shape=jax.ShapeDtypeStruct((ntok, D), f32),
        scratch_shapes=[pltpu.VMEM((tm, D), f32), pltpu.SemaphoreType.DMA(())],
        compiler_params=pltpu.CompilerParams(dimension_semantics=("arbitrary",)),
        name="combine",
    )(pos3, ys, x1, mod)


def _moe_plan(plan, counts, ntile):
    cnt = counts[:N_CLS, 0].astype(jnp.int32)
    tiles_per = (cnt + (MOE_R - 1)) // MOE_R
    tstart = jnp.cumsum(tiles_per) - tiles_per
    total = jnp.sum(tiles_per)
    pos = tstart[plan[0]] * MOE_R + plan[1]
    t = jnp.arange(ntile, dtype=jnp.int32)
    src = jnp.minimum(t, total - 1)
    tcls = jnp.sum((src[:, None] >= tstart[None, :]).astype(jnp.int32), axis=1) - 1
    grp = tcls // N_PAIR
    pair = tcls % N_PAIR
    pa = jnp.asarray(np.array([0, 0, 0, 1, 1, 2], np.int32))
    pb = jnp.asarray(np.array([1, 2, 3, 2, 3, 3], np.int32))
    ea = grp * EXP_PER_GRP + pa[pair]
    eb = grp * EXP_PER_GRP + pb[pair]
    valid = (t < total).astype(jnp.int32)
    return pos, (src, valid, ea, eb)


def _moe_kernel(h2_ref, comb_ref, wg_ref, wu_ref, wd_ref, x1_ref, mod_ref, x2_ref, acc_scr, *, seq3d):
    e = pl.program_id(1)

    @pl.when(e == 0)
    def _():
        acc_scr[...] = jnp.zeros_like(acc_scr)

    h = h2_ref[...]
    a = _dot(h, wg_ref[...])
    b = _dot(h, wu_ref[...])
    t = (_silu(a) * b).astype(bf16)
    ye = _dot(t, wd_ref[...])
    lane = lax.broadcasted_iota(jnp.int32, comb_ref.shape, 1)
    w = jnp.sum(jnp.where(lane == e, comb_ref[...], 0.0), axis=-1, keepdims=True)
    acc_scr[...] += w * ye

    @pl.when(e == N_EXP - 1)
    def _():
        x1 = x1_ref[...]
        if seq3d:
            x2_ref[...] = x1 + mod_ref[:, 5:6, :] * acc_scr[...].reshape(x1.shape)
        else:
            x2_ref[...] = x1 + mod_ref[5:6, :] * acc_scr[...]


def _moe(l, h2, comb, wg, wu, wd, x1, mod, *, seq3d):
    if seq3d:
        ns = x1.shape[0]
        ntok = ns * x1.shape[1]
        tm = ntok
        nt = 1
        x_spec = pl.BlockSpec((ns, x1.shape[1], D), lambda i, e: (0, 0, 0))
        mod_spec = pl.BlockSpec((None, ns, 6, D), lambda i, e: (l, 0, 0, 0))
    else:
        ntok = x1.shape[0]
        nseq = mod.shape[1]
        tm = min(1024, ntok // nseq)
        nt = ntok // tm
        per_seq = ntok // nseq // tm
        x_spec = pl.BlockSpec((tm, D), lambda i, e: (i, 0))
        mod_spec = pl.BlockSpec((None, None, 6, D), lambda i, e: (l, i // per_seq, 0, 0))
    row = lambda i, e: (i, 0)
    return pl.pallas_call(
        functools.partial(_moe_kernel, seq3d=seq3d),
        grid=(nt, N_EXP),
        in_specs=[
            pl.BlockSpec((tm, D), row),
            pl.BlockSpec((tm, LANES), row),
            pl.BlockSpec((None, None, D, EXP_FF), lambda i, e: (l, e, 0, 0)),
            pl.BlockSpec((None, None, D, EXP_FF), lambda i, e: (l, e, 0, 0)),
            pl.BlockSpec((None, None, EXP_FF, D), lambda i, e: (l, e, 0, 0)),
            x_spec,
            mod_spec,
        ],
        out_specs=x_spec,
        out_shape=jax.ShapeDtypeStruct(x1.shape, f32),
        scratch_shapes=[pltpu.VMEM((tm, D), f32)],
        compiler_params=pltpu.CompilerParams(
            dimension_semantics=("arbitrary", "arbitrary"), vmem_limit_bytes=VMEM_LIMIT),
        name="moe",
    )(h2, comb, wg, wu, wd, x1, mod)


def _final_norm_kernel(x_ref, w_ref, o_ref):
    x = x_ref[...]
    o_ref[...] = x * lax.rsqrt(jnp.mean(x * x, axis=-1, keepdims=True) + EPS) * w_ref[...]


def _final_norm(x2d, w):
    ntok = x2d.shape[0]
    tm = min(1024, ntok)
    return pl.pallas_call(
        _final_norm_kernel,
        grid=(ntok // tm,),
        in_specs=[pl.BlockSpec((tm, D), lambda i: (i, 0)), pl.BlockSpec((1, D), lambda i: (0, 0))],
        out_specs=pl.BlockSpec((tm, D), lambda i: (i, 0)),
        out_shape=jax.ShapeDtypeStruct((ntok, D), f32),
        compiler_params=pltpu.CompilerParams(dimension_semantics=("arbitrary",)),
        name="final_norm",
    )(x2d, w)


def _prep_in_weights(w_in):
    o = 0
    z = w_in[:, :, o:o + SSD_INNER]; o += SSD_INNER
    xbc = w_in[:, :, o:o + CONV_DIM]; o += CONV_DIM
    dt = w_in[:, :, o:o + SSD_NH]; o += SSD_NH
    q = w_in[:, :, o:o + GLA_KD] * (GLA_DK ** -0.5); o += GLA_KD
    kvr = w_in[:, :, o:o + GLA_KD + 2 * GLA_VD]; o += GLA_KD + 2 * GLA_VD
    glr = w_in[:, :, o:o + GLA_RANK]; o += GLA_RANK
    gates = w_in[:, :, o:o + 2 * D]
    main = jnp.concatenate([z, xbc, q, kvr, gates], axis=-1).astype(bf16)
    pad = jnp.zeros(w_in.shape[:2] + (LANES - SSD_NH - GLA_RANK,), w_in.dtype)
    tail = jnp.concatenate([dt, glr, pad], axis=-1).astype(bf16)
    return main, tail


def _pad_lanes(v, width=LANES):
    return jnp.concatenate([v, jnp.zeros(v.shape[:-1] + (width - v.shape[-1],), v.dtype)], axis=-1)


def kernel(x_prompt, x_sample, c_prompt, c_sample, state_conv, state_ssm, state_gla, w_ada, b_ada, norm1_w, w_in, conv_w, conv_b, dt_bias, a_log, d_skip, ssd_norm_w, w_ssd_br, gla_gate_up, gla_gate_b, gla_norm_w, w_gla_br, merge_b, w_out, norm2_w, w_router, router_bias, w_exp_gate, w_exp_up, w_exp_down, final_norm_w):
    bp, sp, _ = x_prompt.shape
    bs, ss, _ = x_sample.shape

    mod = _adaln(jnp.concatenate([c_prompt, c_sample], axis=0), w_ada, b_ada)
    mod_p = mod[:, :bp].reshape(DEPTH, bp, 6, D)
    mod_s = mod[:, bp:].reshape(DEPTH, bs, 6, D)

    w_main, w_tail = _prep_in_weights(w_in)
    gate_up = jnp.concatenate(
        [jnp.zeros((DEPTH, TAIL_GLR0, GLA_KD), f32), gla_gate_up,
         jnp.zeros((DEPTH, LANES - TAIL_GLR0 - GLA_RANK, GLA_KD), f32)], axis=1).astype(bf16)
    in_w = (norm1_w.reshape(DEPTH, 1, D), w_main, w_tail, gate_up, gla_gate_b.reshape(DEPTH, 1, GLA_KD),
            conv_w, conv_b.reshape(DEPTH, 1, CONV_DIM), merge_b.reshape(DEPTH, 1, 2 * D))
    bigsel, e8 = _ssd_consts()
    ssd_w = (_pad_lanes(dt_bias).reshape(DEPTH, 1, LANES), _pad_lanes(a_log).reshape(DEPTH, 1, LANES),
             jnp.repeat(d_skip, SSD_HD, axis=-1).reshape(DEPTH, SSD_NG, 1, SSD_GW),
             ssd_norm_w.reshape(DEPTH, SSD_NG, 1, SSD_GW),
             w_ssd_br.astype(bf16).reshape(DEPTH, SSD_NG, SSD_GW, D), bigsel, e8)
    gla_w = (gla_norm_w.reshape(DEPTH, 1, GLA_DV), w_gla_br.astype(bf16))
    w_out_b = w_out.astype(bf16)
    n2 = norm2_w.reshape(DEPTH, 1, D)
    wg_b = w_exp_gate.astype(bf16)
    wu_b = w_exp_up.astype(bf16)
    wd_b = w_exp_down.astype(bf16)
    wr = _pad_lanes(w_router)
    wr_hi = wr.astype(bf16)
    wr_lo = (wr - wr_hi.astype(f32)).astype(bf16)
    rb = router_bias.reshape(N_EXP, 1)

    xp = x_prompt.reshape(bp * sp, D)
    xs = x_sample
    ssm_s = state_ssm.reshape(DEPTH, bs, SSD_NG, SSD_GW, SSD_NS)
    ssd_lp = min(128, sp)
    gla_lp = min(64, sp)
    cp = sp_ = gp = cs_ = ss_ = gs = None
    moe_tm = min(MOE_TM, sp)
    moe_tiles = -(-(bp * sp) // MOE_R) + N_CLS
    xs_rows = jnp.zeros((moe_tiles * MOE_R, D + MOE_EXT), f32)
    for l in range(DEPTH):
        z4, xs4, bc, qkvr, gate, tail, lg, cp = _inproj(l, xp, mod_p, None, cp, in_w, seq3d=False)
        bra, sp_ = _ssd(l, xs4, bc, z4, tail, None, sp_, ssd_w, nseq=bp, L=ssd_lp, sb=1, nchunks=sp // ssd_lp)
        brb, gp = _gla(l, qkvr, lg, None, gp, gla_w, nseq=bp, L=gla_lp, C=16, sb=1, nchunks=sp // gla_lp)
        x1, h2x, plan, counts = _merge(l, xp, bra, brb, gate, mod_p, w_out_b, n2, wr_hi, wr_lo, rb,
                                       seq3d=False, sparse=True)
        pos, tiles = _moe_plan(plan, counts, moe_tiles)
        pos3 = pos.reshape(bp * sp // moe_tm, 1, moe_tm)
        xs_rows = _dispatch(pos3, h2x, xs_rows)
        ys_rows = _moe_sparse(l, tiles, xs_rows, wg_b, wu_b, wd_b)
        xp = _combine(l, pos3, ys_rows, x1, mod_p)

        z4, xs4, bc, qkvr, gate, tail, lg, cs_ = _inproj(l, xs, mod_s, state_conv, cs_, in_w, seq3d=True)
        bra, ss_ = _ssd(l, xs4, bc, z4, tail, ssm_s, ss_, ssd_w, nseq=bs, L=ss, sb=4, nchunks=1)
        brb, gs = _gla(l, qkvr, lg, state_gla, gs, gla_w, nseq=bs, L=ss, C=ss, sb=4, nchunks=1)
        x1, h2, comb = _merge(l, xs, bra, brb, gate, mod_s, w_out_b, n2, wr_hi, wr_lo, rb,
                              seq3d=True, sparse=False)
        xs = _moe(l, h2, comb, wg_b, wu_b, wd_b, x1, mod_s, seq3d=True)

    fw = final_norm_w.reshape(1, D)
    y_prompt = _final_norm(xp, fw).reshape(bp, sp, D)
    y_sample = _final_norm(xs.reshape(bs * ss, D), fw).reshape(bs, ss, D)
    return (y_prompt, y_sample, cp, sp_.reshape(DEPTH, bp, SSD_NH, SSD_HD, SSD_NS), gp,
            cs_, ss_.reshape(DEPTH, bs, SSD_NH, SSD_HD, SSD_NS), gs)
```

```python
import functools

import numpy as np
import jax
import jax.numpy as jnp
from jax import lax
from jax.experimental import pallas as pl
from jax.experimental.pallas import tpu as pltpu

f32 = jnp.float32
bf16 = jnp.bfloat16

D = 1024
DEPTH = 4
SSD_INNER = 2048
SSD_HD = 64
SSD_NH = 32
SSD_NS = 128
SSD_NG = 4
SSD_HPG = 8
SSD_GW = SSD_INNER // SSD_NG
CONV_K = 4
CONV_DIM = SSD_INNER + 2 * SSD_NG * SSD_NS
GLA_NH = 4
GLA_DK = 128
GLA_DV = 256
GLA_KD = GLA_NH * GLA_DK
GLA_VD = GLA_NH * GLA_DV
GLA_RANK = 16
GLA_TAU = 16.0
N_EXP = 16
N_EGRP = 4
EXP_PER_GRP = 4
EXP_FF = 512
N_PAIR = 6
N_CLS = N_EGRP * N_PAIR
CLS_PAD = 32
MOE_EXT = 128
MOE_R = 256
MOE_TM = 512
EPS = 1e-6
LANES = 128
TAIL_GLR0 = SSD_NH
MAIN_W = SSD_INNER + CONV_DIM + 2 * GLA_KD + 2 * GLA_VD + 2 * D
VMEM_LIMIT = 56 * 1024 * 1024


def _sigmoid(x):
    return 1.0 / (1.0 + jnp.exp(-x))


def _silu(x):
    return x * _sigmoid(x)


def _softplus(x):
    return jnp.maximum(x, 0.0) + jnp.log1p(jnp.exp(-jnp.abs(x)))


def _log_sigmoid(x):
    return jnp.minimum(x, 0.0) - jnp.log1p(jnp.exp(-jnp.abs(x)))


def _dot(a, b):
    return jnp.dot(a, b, preferred_element_type=f32)


def _dot_nt(a, b):
    return lax.dot_general(a, b, (((1,), (1,)), ((), ())), preferred_element_type=f32)


def _dot_tn(a, b):
    return lax.dot_general(a, b, (((0,), (0,)), ((), ())), preferred_element_type=f32)


def _split2(x):
    hi = x.astype(bf16)
    return hi, (x - hi.astype(f32)).astype(bf16)


def _split3(x):
    hi = x.astype(bf16)
    r1 = x - hi.astype(f32)
    mid = r1.astype(bf16)
    lo = (r1 - mid.astype(f32)).astype(bf16)
    return hi, mid, lo


def _cumsum_rows(x, tri):
    hi, mid, lo = _split3(x)
    return _dot(tri, hi) + _dot(tri, mid) + _dot(tri, lo)


def _tri(n):
    r = lax.broadcasted_iota(jnp.int32, (n, n), 0)
    c = lax.broadcasted_iota(jnp.int32, (n, n), 1)
    return r >= c


def _norm_mod(x, mod_ref, w_ref, i_shift, i_scale, seq3d):
    ms = jnp.mean(x * x, axis=-1, keepdims=True)
    y = x * lax.rsqrt(ms + EPS) * w_ref[...]
    if seq3d:
        sc = mod_ref[:, i_scale:i_scale + 1, :]
        sh = mod_ref[:, i_shift:i_shift + 1, :]
        h = y * (1.0 + sc) + sh
        return h.reshape(h.shape[0] * h.shape[1], h.shape[2])
    sc = mod_ref[i_scale:i_scale + 1, :]
    sh = mod_ref[i_shift:i_shift + 1, :]
    return y * (1.0 + sc) + sh


def _seq_tile(nseq, want):
    return want if nseq % want == 0 else nseq


def _stacked_out(prev, in_specs, args, aliases, out_index):
    if prev is not None:
        in_specs.append(pl.BlockSpec(memory_space=pl.ANY))
        args.append(prev)
        aliases[len(args) - 1] = out_index


def _adaln_kernel(c_ref, w_ref, b_ref, o_ref):
    s = _silu(c_ref[...]).astype(bf16)
    o_ref[...] = _dot(s, w_ref[...].astype(bf16)) + b_ref[...]


def _adaln(c_all, w_ada, b_ada):
    n = c_all.shape[0]
    tn = 1024
    return pl.pallas_call(
        _adaln_kernel,
        grid=(DEPTH, 6 * D // tn),
        in_specs=[
            pl.BlockSpec((n, D), lambda l, j: (0, 0)),
            pl.BlockSpec((None, D, tn), lambda l, j: (l, 0, j)),
            pl.BlockSpec((None, 1, tn), lambda l, j: (l, 0, j)),
        ],
        out_specs=pl.BlockSpec((None, n, tn), lambda l, j: (l, 0, j)),
        out_shape=jax.ShapeDtypeStruct((DEPTH, n, 6 * D), f32),
        compiler_params=pltpu.CompilerParams(
            dimension_semantics=("arbitrary", "arbitrary"), vmem_limit_bytes=VMEM_LIMIT),
        name="adaln",
    )(c_all, w_ada, b_ada.reshape(DEPTH, 1, 6 * D))


IN_TN = 1024
IN_NJ = MAIN_W // IN_TN
IN_SUB = 256
IN_NSUB = IN_TN // IN_SUB
CONV_PAD = 8
KPREV = CONV_K - 1


def _inproj_kernel(*refs, seq3d, has_state, per_seq, T):
    if has_state:
        (x_ref, mod_ref, n1_ref, wm_ref, wt_ref, wg_ref, gb_ref, cw_ref, cb_ref, mb_ref, conv0_ref) = refs[:11]
        rest = refs[11:]
    else:
        (x_ref, mod_ref, n1_ref, wm_ref, wt_ref, wg_ref, gb_ref, cw_ref, cb_ref, mb_ref) = refs[:10]
        conv0_ref = None
        rest = refs[10:]
    (z4_ref, xs4_ref, bc_ref, qkvr_ref, gate_ref, tail_ref, lg_ref, convn_ref,
     h_scr, cscr, carry) = rest[-11:]
    i = pl.program_id(0)
    j = pl.program_id(1)
    tm = h_scr.shape[0]
    ns = cscr.shape[1]

    @pl.when(j == 0)
    def _():
        h = _norm_mod(x_ref[...], mod_ref, n1_ref, 0, 1, seq3d).astype(bf16)
        h_scr[...] = h
        t = _dot(h, wt_ref[...])
        tail_ref[...] = t
        pre = _dot(t.astype(bf16), wg_ref[...]) + gb_ref[...]
        lg_ref[...] = _log_sigmoid(pre) * (1.0 / GLA_TAU)

    def sub_dot(c):
        return _dot(h_scr[...], wm_ref[:, IN_SUB * c:IN_SUB * (c + 1)])

    def conv_silu(a, cj, c):
        cols = slice(IN_SUB * c, IN_SUB * (c + 1))
        a3 = a.reshape(ns, T, IN_SUB)
        if has_state:
            prev = conv0_ref[:, :, cols]
        else:
            prev = jnp.where(i % per_seq == 0, 0.0, carry[cj, c, CONV_PAD - KPREV:CONV_PAD, :])[None]
        cscr[c, :, CONV_PAD - KPREV:CONV_PAD, :] = prev
        cscr[c, :, CONV_PAD:CONV_PAD + T, :] = a3
        out = cb_ref[:, cols] + a3 * cw_ref[KPREV:KPREV + 1, cols]
        for k in range(KPREV):
            out = out + cscr[c, :, CONV_PAD - KPREV + k:CONV_PAD - KPREV + k + T, :] * cw_ref[k:k + 1, cols]
        new_tail = cscr[c, :, CONV_PAD + T - KPREV:CONV_PAD + T, :]
        convn_ref[:, :, IN_TN * cj + IN_SUB * c:IN_TN * cj + IN_SUB * (c + 1)] = new_tail
        if not has_state:
            carry[cj, c, CONV_PAD - KPREV:CONV_PAD, :] = new_tail[0]
        return _silu(out).reshape(tm, IN_SUB).astype(bf16)

    per_grp = SSD_GW // IN_SUB

    @pl.when(j < 2)
    def _():
        for c in range(IN_NSUB):
            lanes = slice(IN_SUB * (c % per_grp), IN_SUB * (c % per_grp + 1))
            z4_ref[c // per_grp, :, lanes] = _silu(sub_dot(c)).astype(bf16)

    for cj in range(2):
        @pl.when(j == 2 + cj)
        def _(cj=cj):
            for c in range(IN_NSUB):
                lanes = slice(IN_SUB * (c % per_grp), IN_SUB * (c % per_grp + 1))
                xs4_ref[c // per_grp, :, lanes] = conv_silu(sub_dot(c), cj, c)

    @pl.when(j == 4)
    def _():
        for c in range(IN_NSUB):
            xc = conv_silu(sub_dot(c), 2, c)
            for q in range(IN_SUB // SSD_NS):
                bc_ref[c * (IN_SUB // SSD_NS) + q] = xc[:, SSD_NS * q:SSD_NS * (q + 1)]

    @pl.when(jnp.logical_and(j >= 5, j < 7))
    def _():
        for c in range(IN_NSUB):
            qkvr_ref[:, IN_SUB * c:IN_SUB * (c + 1)] = sub_dot(c).astype(bf16)

    @pl.when(j == 7)
    def _():
        for c in range(IN_NSUB):
            qkvr_ref[:, IN_SUB * c:IN_SUB * (c + 1)] = _silu(sub_dot(c)).astype(bf16)

    @pl.when(j >= 8)
    def _():
        for c in range(IN_NSUB):
            cols = slice(IN_SUB * c, IN_SUB * (c + 1))
            gate_ref[:, cols] = _sigmoid(sub_dot(c) + mb_ref[:, cols]).astype(bf16)


def _inproj(l, x, mod, conv0, convn_prev, w, *, seq3d):
    (n1w, w_main, w_tail, gate_up, gate_b, conv_w, conv_b, merge_b) = w
    has_state = conv0 is not None
    if seq3d:
        nseq, T = x.shape[0], x.shape[1]
        ns = _seq_tile(nseq, 64)
        ntok = nseq * T
        tm = ns * T
        nt = nseq // ns
        per_seq = 1
        x_spec = pl.BlockSpec((ns, T, D), lambda i, j: (i, 0, 0))
        mod_spec = pl.BlockSpec((None, ns, 6, D), lambda i, j: (l, i, 0, 0))
        seq_blk = lambda i: i
    else:
        nseq = mod.shape[1]
        ntok = x.shape[0]
        tm = min(1024, ntok // nseq)
        T = tm
        ns = 1
        nt = ntok // tm
        per_seq = ntok // nseq // tm
        x_spec = pl.BlockSpec((tm, D), lambda i, j: (i, 0))
        mod_spec = pl.BlockSpec((None, None, 6, D), lambda i, j: (l, i // per_seq, 0, 0))
        seq_blk = lambda i: i // per_seq

    def cj(j):
        return jnp.clip(j - 2, 0, 2)

    in_specs = [
        x_spec,
        mod_spec,
        pl.BlockSpec((None, 1, D), lambda i, j: (l, 0, 0)),
        pl.BlockSpec((None, D, IN_TN), lambda i, j: (l, 0, j)),
        pl.BlockSpec((None, D, LANES), lambda i, j: (l, 0, 0)),
        pl.BlockSpec((None, LANES, GLA_KD), lambda i, j: (l, 0, 0)),
        pl.BlockSpec((None, 1, GLA_KD), lambda i, j: (l, 0, 0)),
        pl.BlockSpec((None, CONV_K, IN_TN), lambda i, j: (l, 0, cj(j))),
        pl.BlockSpec((None, 1, IN_TN), lambda i, j: (l, 0, cj(j))),
        pl.BlockSpec((None, 1, IN_TN), lambda i, j: (l, 0, jnp.clip(j - 8, 0, 1))),
    ]
    args = [x, mod, n1w, w_main, w_tail, gate_up, gate_b, conv_w, conv_b, merge_b]
    if has_state:
        in_specs.append(pl.BlockSpec((None, ns, KPREV, IN_TN), lambda i, j: (l, i, 0, cj(j))))
        args.append(conv0)
    aliases = {}
    _stacked_out(convn_prev, in_specs, args, aliases, 7)
    out_shapes = (
        jax.ShapeDtypeStruct((SSD_NG, ntok, SSD_GW), bf16),
        jax.ShapeDtypeStruct((SSD_NG, ntok, SSD_GW), bf16),
        jax.ShapeDtypeStruct((2 * SSD_NG, ntok, SSD_NS), bf16),
        jax.ShapeDtypeStruct((ntok, 2 * GLA_KD + 2 * GLA_VD), bf16),
        jax.ShapeDtypeStruct((ntok, 2 * D), bf16),
        jax.ShapeDtypeStruct((ntok, LANES), f32),
        jax.ShapeDtypeStruct((ntok, GLA_KD), f32),
        jax.ShapeDtypeStruct((DEPTH, nseq, KPREV, CONV_DIM), f32),
    )
    out_specs = (
        pl.BlockSpec((2, tm, SSD_GW), lambda i, j: (jnp.clip(j, 0, 1), i, 0)),
        pl.BlockSpec((2, tm, SSD_GW), lambda i, j: (jnp.clip(j - 2, 0, 1), i, 0)),
        pl.BlockSpec((2 * SSD_NG, tm, SSD_NS), lambda i, j: (0, i, 0)),
        pl.BlockSpec((tm, IN_TN), lambda i, j: (i, jnp.clip(j - 5, 0, 2))),
        pl.BlockSpec((tm, IN_TN), lambda i, j: (i, jnp.clip(j - 8, 0, 1))),
        pl.BlockSpec((tm, LANES), lambda i, j: (i, 0)),
        pl.BlockSpec((tm, GLA_KD), lambda i, j: (i, 0)),
        pl.BlockSpec((None, ns, KPREV, CONV_DIM), lambda i, j: (l, seq_blk(i), 0, 0)),
    )
    return pl.pallas_call(
        functools.partial(_inproj_kernel, seq3d=seq3d, has_state=has_state, per_seq=per_seq, T=T),
        grid=(nt, IN_NJ),
        in_specs=in_specs,
        out_specs=out_specs,
        out_shape=out_shapes,
        input_output_aliases=aliases,
        scratch_shapes=[
            pltpu.VMEM((tm, D), bf16),
            pltpu.VMEM((IN_NSUB, ns, CONV_PAD + T, IN_SUB), f32),
            pltpu.VMEM((3, IN_NSUB, CONV_PAD, IN_SUB), f32),
        ],
        compiler_params=pltpu.CompilerParams(
            dimension_semantics=("arbitrary", "arbitrary"), vmem_limit_bytes=VMEM_LIMIT),
        name="inproj",
    )(*args)


def _ssd_consts():
    bigsel = np.zeros((SSD_NG, 3 * LANES, SSD_HPG * LANES), np.float32)
    e8 = np.zeros((SSD_NG, 2 * LANES, SSD_GW), np.float32)
    for g in range(SSD_NG):
        for j in range(SSD_HPG):
            for k in range(3):
                bigsel[g, k * LANES + SSD_HPG * g + j, LANES * j:LANES * (j + 1)] = 1.0
            for k in range(2):
                e8[g, k * LANES + SSD_HPG * g + j, SSD_HD * j:SSD_HD * (j + 1)] = 1.0
    return jnp.asarray(bigsel, bf16), jnp.asarray(e8, bf16)


def _ssd_kernel(*refs, L, sb, nchunks, has_state):
    R = sb * L
    xs4_ref, bc_ref, z4_ref, tail_ref = refs[:4]
    k = 4
    ssm0_ref = None
    if has_state:
        ssm0_ref = refs[k]
        k += 1
    dtb_ref, alog_ref, dsk_ref, nw_ref, wbr_ref, bigsel_ref, e8_ref = refs[k:k + 7]
    br_ref, ssmn_ref, cst_scr = refs[-3:]
    c = pl.program_id(1)

    rowi = lax.broadcasted_iota(jnp.int32, (R, R), 0)
    coli = lax.broadcasted_iota(jnp.int32, (R, R), 1)
    mask = rowi >= coli
    if sb > 1:
        sh = L.bit_length() - 1
        mask = jnp.logical_and(mask, (rowi >> sh) == (coli >> sh))
    tri = mask.astype(bf16)
    lane = lax.broadcasted_iota(jnp.int32, (R, LANES), 1)
    lo_half = lane < SSD_HD

    def init():
        if has_state:
            ssmn_ref[...] = ssm0_ref[...]
        else:
            ssmn_ref[...] = jnp.zeros(ssmn_ref.shape, f32)

    if nchunks == 1:
        init()
    else:
        pl.when(c == 0)(init)

    dtp = _softplus(tail_ref[...] + dtb_ref[...])
    a = -jnp.exp(alog_ref[...])
    cs = _cumsum_rows(dtp * a, tri)
    cst_scr[...] = cs.T
    h3 = jnp.concatenate(_split3(cs), axis=1)
    d2 = jnp.concatenate(_split2(dtp), axis=1)
    br_ref[...] = jnp.zeros(br_ref.shape, f32)

    def seq_last(x):
        if sb == 1:
            return x[R - 1:R, :]
        x3 = x.reshape(sb, L, x.shape[-1])
        return jnp.broadcast_to(x3[:, L - 1:L, :], x3.shape).reshape(x.shape)

    def group_body(g, carry):
        cm = _dot(h3, bigsel_ref[g])
        dt_exp = _dot(d2, e8_ref[g])
        cs_exp = jnp.concatenate(
            [jnp.where(lo_half, cm[:, 2 * LANES * i:2 * LANES * i + LANES],
                       cm[:, 2 * LANES * i + LANES:2 * LANES * (i + 1)]) for i in range(SSD_HPG // 2)], axis=1)
        ecs = jnp.exp(cs_exp)
        ce = seq_last(cs_exp)
        xs = xs4_ref[g].astype(f32)
        xdt = xs * dt_exp
        xdt_b = xdt.astype(bf16)
        xse = xdt * jnp.exp(ce - cs_exp)
        bg = bc_ref[g]
        cg = bc_ref[SSD_NG + g]
        cb = _dot_nt(cg, bg)

        pairs = []
        for i in range(SSD_HPG // 2):
            ws = []
            for j in (2 * i, 2 * i + 1):
                row = cst_scr[pl.ds(SSD_HPG * g + j, 1), :]
                seg = cm[:, LANES * j:LANES * j + R] - row
                dec = jnp.exp(jnp.where(mask, seg, -jnp.inf))
                ws.append((cb * dec).astype(bf16))
            yy = _dot(jnp.concatenate(ws, axis=0), xdt_b[:, LANES * i:LANES * (i + 1)])
            pairs.append(jnp.where(lo_half, yy[:R], yy[R:]))
        y = jnp.concatenate(pairs, axis=1)

        if sb > 1:
            bgf = bg.astype(f32)
            cgf = cg.astype(f32)
        ys_parts = []
        for s in range(sb):
            rs = slice(s * L, (s + 1) * L)
            st = ssmn_ref[s, g]
            b_s = bg if sb == 1 else bgf[rs].astype(bf16)
            c_s = cg if sb == 1 else cgf[rs].astype(bf16)
            ys_parts.append(_dot_nt(c_s, st.astype(bf16)))
            upd = _dot_tn(xse[rs].astype(bf16), b_s)
            e_end = jnp.exp(ce[s * L:s * L + 1, :])
            ssmn_ref[s, g] = jnp.concatenate(
                [st[SSD_HD * j:SSD_HD * (j + 1)] * e_end[:, SSD_HD * j:SSD_HD * j + 1]
                 + upd[SSD_HD * j:SSD_HD * (j + 1)] for j in range(SSD_HPG)], axis=0)
        ys = ys_parts[0] if sb == 1 else jnp.concatenate(ys_parts, axis=0)
        y = y + ys * ecs + dsk_ref[g] * xs

        yg = y * z4_ref[g].astype(f32)
        yn = yg * lax.rsqrt(jnp.mean(yg * yg, axis=-1, keepdims=True) + EPS) * nw_ref[g]
        br_ref[...] += _dot(yn.astype(bf16), wbr_ref[g])
        return carry

    for g in range(SSD_NG):
        group_body(g, 0)


def _ssd(l, xs4, bc, z4, tail, ssm0, ssmn_prev, w, *, nseq, L, sb, nchunks):
    has_state = ssm0 is not None
    dtb, alog, dsk, nw, wbr, bigsel, e8 = w
    ntok = tail.shape[0]
    R = sb * L
    if nchunks == 1:
        rblk = lambda b, c: b
    else:
        rblk = lambda b, c: b * nchunks + c
    in_specs = [
        pl.BlockSpec((SSD_NG, R, SSD_GW), lambda b, c: (0, rblk(b, c), 0)),
        pl.BlockSpec((2 * SSD_NG, R, SSD_NS), lambda b, c: (0, rblk(b, c), 0)),
        pl.BlockSpec((SSD_NG, R, SSD_GW), lambda b, c: (0, rblk(b, c), 0)),
        pl.BlockSpec((R, LANES), lambda b, c: (rblk(b, c), 0)),
    ]
    args = [xs4, bc, z4, tail]
    if has_state:
        in_specs.append(pl.BlockSpec((None, sb, SSD_NG, SSD_GW, SSD_NS), lambda b, c: (l, b, 0, 0, 0)))
        args.append(ssm0)
    in_specs += [
        pl.BlockSpec((None, 1, LANES), lambda b, c: (l, 0, 0)),
        pl.BlockSpec((None, 1, LANES), lambda b, c: (l, 0, 0)),
        pl.BlockSpec((None, SSD_NG, 1, SSD_GW), lambda b, c: (l, 0, 0, 0)),
        pl.BlockSpec((None, SSD_NG, 1, SSD_GW), lambda b, c: (l, 0, 0, 0)),
        pl.BlockSpec((None, SSD_NG, SSD_GW, D), lambda b, c: (l, 0, 0, 0)),
        pl.BlockSpec((SSD_NG, 3 * LANES, SSD_HPG * LANES), lambda b, c: (0, 0, 0)),
        pl.BlockSpec((SSD_NG, 2 * LANES, SSD_GW), lambda b, c: (0, 0, 0)),
    ]
    args += [dtb, alog, dsk, nw, wbr, bigsel, e8]
    aliases = {}
    _stacked_out(ssmn_prev, in_specs, args, aliases, 1)
    return pl.pallas_call(
        functools.partial(_ssd_kernel, L=L, sb=sb, nchunks=nchunks, has_state=has_state),
        grid=(nseq // sb, nchunks),
        in_specs=in_specs,
        out_specs=(
            pl.BlockSpec((R, D), lambda b, c: (rblk(b, c), 0)),
            pl.BlockSpec((None, sb, SSD_NG, SSD_GW, SSD_NS), lambda b, c: (l, b, 0, 0, 0)),
        ),
        out_shape=(
            jax.ShapeDtypeStruct((ntok, D), f32),
            jax.ShapeDtypeStruct((DEPTH, nseq, SSD_NG, SSD_GW, SSD_NS), f32),
        ),
        input_output_aliases=aliases,
        scratch_shapes=[pltpu.VMEM((LANES, R), f32)],
        compiler_params=pltpu.CompilerParams(
            dimension_semantics=("arbitrary", "arbitrary"), vmem_limit_bytes=VMEM_LIMIT),
        name="ssd",
    )(*args)


GLA_C2 = 4


def _gla_consts(L, C1, sb):
    R = sb * L
    nb2 = C1 // GLA_C2
    i = np.arange(R)[:, None]
    s = np.arange(R)[None, :]
    same_seq = (i // L) == (s // L)
    mats = [(s <= i) & (s > i - d) & same_seq for d in range(1, GLA_C2)]
    mats.append((s <= i) & (s >= (i // GLA_C2) * GLA_C2))
    for r in range(1, nb2):
        mats.append((s > i) & (s <= (i // C1) * C1 + GLA_C2 * r - 1))
    shifts = [(s == i - d) & same_seq for d in range(1, GLA_C2)]
    return (jnp.asarray(np.concatenate(mats, 0), bf16), jnp.asarray(np.concatenate(shifts, 0), bf16))


def _gla_kernel(*refs, L, C1, sb, nchunks, has_state):
    R = sb * L
    nb1 = L // C1
    nb2 = C1 // GLA_C2
    qkvr_ref, lg_ref = refs[:2]
    k = 2
    gla0_ref = None
    if has_state:
        gla0_ref = refs[k]
        k += 1
    gnw_ref, wbr_ref, sums_ref, shift_ref = refs[k:k + 4]
    br_ref, glan_ref = refs[-2:]
    c = pl.program_id(1)

    rowi = lax.broadcasted_iota(jnp.int32, (R, R), 0)
    coli = lax.broadcasted_iota(jnp.int32, (R, R), 1)
    shl = L.bit_length() - 1
    same_seq = (rowi >> shl) == (coli >> shl)
    tri = jnp.logical_and(rowi >= coli, same_seq).astype(bf16)
    sh1 = C1.bit_length() - 1
    same_blk1 = (rowi >> sh1) == (coli >> sh1)
    sub_i = (rowi & (C1 - 1)) >> (GLA_C2.bit_length() - 1)
    off_i = rowi & (GLA_C2 - 1)
    rowl = lax.broadcasted_iota(jnp.int32, (R, 1), 0)
    sub_l = (rowl & (C1 - 1)) >> (GLA_C2.bit_length() - 1)
    blk_i = (rowi & (L - 1)) >> sh1
    rel_l = rowl & (L - 1)
    pad_rows = [jnp.zeros((LANES - R, GLA_DK), bf16)] if R < LANES else []

    def init():
        if has_state:
            glan_ref[...] = gla0_ref[...]
        else:
            glan_ref[...] = jnp.zeros(glan_ref.shape, f32)

    if nchunks == 1:
        init()
    else:
        pl.when(c == 0)(init)

    def seq_last(x):
        if sb == 1:
            return x[R - 1:R, :]
        x3 = x.reshape(sb, L, x.shape[-1])
        return jnp.broadcast_to(x3[:, L - 1:L, :], x3.shape).reshape(x.shape)

    def rows2d(x):
        return x.reshape(R, x.shape[-1]) if x.ndim == 3 else x

    lg = rows2d(lg_ref[...])
    g = _cumsum_rows(lg, tri)
    g_end = seq_last(g)
    lg_hi, lg_lo = _split2(lg)
    sums = _dot(sums_ref[...], lg_hi) + _dot(sums_ref[...], lg_lo)
    a_d = [sums[R * (d - 1):R * d] for d in range(1, GLA_C2)]
    a_sub = sums[R * (GLA_C2 - 1):R * GLA_C2]
    b_sub = [sums[R * (GLA_C2 - 1 + r):R * (GLA_C2 + r)] for r in range(1, nb2)]
    q = rows2d(qkvr_ref[..., 0:GLA_KD]).astype(f32)
    kb = rows2d(qkvr_ref[..., GLA_KD:2 * GLA_KD])
    k_ = kb.astype(f32)
    k_sh = [_dot(shift_ref[R * (d - 1):R * d, :], kb) for d in range(1, GLA_C2)]

    acc = jnp.zeros((R, D), f32)
    for h in range(GLA_NH):
        kl = slice(GLA_DK * h, GLA_DK * (h + 1))
        vl = slice(2 * GLA_KD + GLA_DV * h, 2 * GLA_KD + GLA_DV * (h + 1))
        rl = slice(2 * GLA_KD + GLA_VD + GLA_DV * h, 2 * GLA_KD + GLA_VD + GLA_DV * (h + 1))
        gh = g[:, kl]
        qh = q[:, kl]
        kh = k_[:, kl]
        vb = rows2d(qkvr_ref[..., vl])
        qg = qh * jnp.exp(gh)
        kd = kh * jnp.exp(g_end[:, kl] - gh)

        if sb == 1:
            st = glan_ref[0, h]
            o = _dot(qg.astype(bf16), st.astype(bf16))
            e_col = jnp.exp(jnp.broadcast_to(g_end[0:1, kl], (8, GLA_DK))).T[:, 0:1]
            glan_ref[0, h] = st * e_col + _dot_tn(kd.astype(bf16), vb)
        else:
            vf = vb.astype(f32)
            parts = []
            for s in range(sb):
                rs = slice(s * L, (s + 1) * L)
                st = glan_ref[s, h]
                parts.append(_dot(qg[rs].astype(bf16), st.astype(bf16)))
                e_col = jnp.exp(jnp.broadcast_to(g_end[s * L:s * L + 1, kl], (8, GLA_DK))).T[:, 0:1]
                glan_ref[s, h] = st * e_col + _dot_tn(kd[rs].astype(bf16), vf[rs].astype(bf16))
            o = jnp.concatenate(parts, axis=0)

        att = jnp.zeros((R, R), f32)
        if nb1 > 1:
            def bnd(s, i):
                if i == 0:
                    return jnp.zeros((1, GLA_DK), f32)
                return gh[s * L + C1 * i - 1:s * L + C1 * i, :]

            gblk = jnp.concatenate([jnp.broadcast_to(bnd(s, i), (C1, GLA_DK))
                                    for s in range(sb) for i in range(nb1)], axis=0)
            qt = (qh * jnp.exp(gh - gblk)).astype(bf16)
            kts = []
            for i in range(1, nb1):
                gb = jnp.concatenate([jnp.broadcast_to(bnd(s, i), (L, GLA_DK)) for s in range(sb)], axis=0)
                kt = kh * jnp.exp(jnp.where(rel_l < C1 * i, gb - gh, -jnp.inf))
                kts += [kt.astype(bf16)] + pad_rows
            out1 = _dot_nt(qt, jnp.concatenate(kts, axis=0))
            for i in range(1, nb1):
                att = att + jnp.where(jnp.logical_and(blk_i == i, same_seq),
                                      out1[:, LANES * (i - 1):LANES * (i - 1) + R], 0.0)

        qt2 = (qh * jnp.exp(a_sub[:, kl])).astype(bf16)
        kts = []
        for r in range(1, nb2):
            kt = kh * jnp.exp(jnp.where(sub_l < r, b_sub[r - 1][:, kl], -jnp.inf))
            kts += [kt.astype(bf16)] + pad_rows
        out2 = _dot_nt(qt2, jnp.concatenate(kts, axis=0))
        for r in range(1, nb2):
            att = att + jnp.where(jnp.logical_and(sub_i == r, same_blk1),
                                  out2[:, LANES * (r - 1):LANES * (r - 1) + R], 0.0)

        for d in range(GLA_C2):
            t = qh * kh if d == 0 else qh * jnp.exp(a_d[d - 1][:, kl]) * k_sh[d - 1][:, kl]
            band = jnp.sum(t, axis=-1, keepdims=True)
            hit = jnp.logical_and(off_i >= d, coli == rowi - d)
            att = att + jnp.where(hit, band, 0.0)
        o = o + _dot(att.astype(bf16), vb)

        on = o * lax.rsqrt(jnp.mean(o * o, axis=-1, keepdims=True) + EPS) * gnw_ref[...]
        og = on * rows2d(qkvr_ref[..., rl]).astype(f32)
        acc = acc + _dot(og.astype(bf16), wbr_ref[GLA_DV * h:GLA_DV * (h + 1), :])
    br_ref[...] = acc.reshape(br_ref.shape)


def _gla(l, qkvr, lg, gla0, glan_prev, w, *, nseq, L, C1, sb, nchunks):
    has_state = gla0 is not None
    gnw, wbr = w
    sums, shifts = _gla_consts(L, C1, sb)
    ntok = qkvr.shape[0]
    rb = sb * L
    wq = 2 * GLA_KD + 2 * GLA_VD
    if nchunks == 1:
        row = lambda b, c: (b, 0)
        blk = lambda w_: (rb, w_)
        view = lambda a: a
    else:
        row = lambda b, c: (b, c, 0)
        blk = lambda w_: (sb, L, w_)
        view = lambda a: a.reshape(nseq, ntok // nseq, a.shape[-1])
    in_specs = [
        pl.BlockSpec(blk(wq), row),
        pl.BlockSpec(blk(GLA_KD), row),
    ]
    args = [view(qkvr), view(lg)]
    if has_state:
        in_specs.append(pl.BlockSpec((None, sb, GLA_NH, GLA_DK, GLA_DV), lambda b, c: (l, b, 0, 0, 0)))
        args.append(gla0)
    in_specs += [
        pl.BlockSpec((None, 1, GLA_DV), lambda b, c: (l, 0, 0)),
        pl.BlockSpec((None, GLA_VD, D), lambda b, c: (l, 0, 0)),
        pl.BlockSpec(sums.shape, lambda b, c: (0, 0)),
        pl.BlockSpec(shifts.shape, lambda b, c: (0, 0)),
    ]
    args += [gnw, wbr, sums, shifts]
    aliases = {}
    _stacked_out(glan_prev, in_specs, args, aliases, 1)
    br_shape = (ntok, D) if nchunks == 1 else (nseq, ntok // nseq, D)
    br, glan = pl.pallas_call(
        functools.partial(_gla_kernel, L=L, C1=C1, sb=sb, nchunks=nchunks, has_state=has_state),
        grid=(nseq // sb, nchunks),
        in_specs=in_specs,
        out_specs=(
            pl.BlockSpec(blk(D), row),
            pl.BlockSpec((None, sb, GLA_NH, GLA_DK, GLA_DV), lambda b, c: (l, b, 0, 0, 0)),
        ),
        out_shape=(
            jax.ShapeDtypeStruct(br_shape, f32),
            jax.ShapeDtypeStruct((DEPTH, nseq, GLA_NH, GLA_DK, GLA_DV), f32),
        ),
        input_output_aliases=aliases,
        compiler_params=pltpu.CompilerParams(
            dimension_semantics=("arbitrary", "arbitrary"), vmem_limit_bytes=VMEM_LIMIT),
        name="gla",
    )(*args)
    return br.reshape(ntok, D), glan


def _top2_sum(a, b, c, d):
    hi1, lo1 = jnp.maximum(a, b), jnp.minimum(a, b)
    hi2, lo2 = jnp.maximum(c, d), jnp.minimum(c, d)
    return jnp.maximum(hi1, hi2) + jnp.maximum(jnp.minimum(hi1, hi2), jnp.maximum(lo1, lo2))


def _route_rows(sig, biased):
    gsc = [_top2_sum(*biased[EXP_PER_GRP * g:EXP_PER_GRP * (g + 1)]) for g in range(N_EGRP)]
    best = jnp.zeros_like(gsc[0], dtype=jnp.int32)
    m = gsc[0]
    for g in range(1, N_EGRP):
        better = gsc[g] > m
        best = jnp.where(better, g, best)
        m = jnp.where(better, gsc[g], m)
    masked = [jnp.where(best == (e // EXP_PER_GRP), biased[e], -jnp.inf) for e in range(N_EXP)]

    def first_argmax(vals):
        idx = jnp.zeros_like(best)
        mx = vals[0]
        for e in range(1, N_EXP):
            better = vals[e] > mx
            idx = jnp.where(better, e, idx)
            mx = jnp.where(better, vals[e], mx)
        return idx

    i1 = first_argmax(masked)
    i2 = first_argmax([jnp.where(i1 == e, -jnp.inf, masked[e]) for e in range(N_EXP)])
    w1 = sum(jnp.where(i1 == e, sig[e], 0.0) for e in range(N_EXP))
    w2 = sum(jnp.where(i2 == e, sig[e], 0.0) for e in range(N_EXP))
    den = w1 + w2
    return best, i1, i2, w1 / den, w2 / den


def _merge_kernel(*refs, seq3d, sparse):
    (x_ref, bra_ref, brb_ref, gate_ref, mod_ref, wo_ref, n2_ref, wr_hi_ref, wr_lo_ref, rb_ref) = refs[:10]
    mixed_in = (gate_ref[:, 0:D].astype(f32) * bra_ref[...]
                + gate_ref[:, D:2 * D].astype(f32) * brb_ref[...])
    mixed = _dot(mixed_in.astype(bf16), wo_ref[...])
    x = x_ref[...]
    if seq3d:
        x1 = x + mod_ref[:, 2:3, :] * mixed.reshape(x.shape)
    else:
        x1 = x + mod_ref[2:3, :] * mixed
    refs[10][...] = x1
    h2 = _norm_mod(x1, mod_ref, n2_ref, 3, 4, seq3d)
    hi, lo = _split2(h2)
    logits = _dot(hi, wr_hi_ref[...]) + (_dot(hi, wr_lo_ref[...]) + _dot(lo, wr_hi_ref[...]))
    tm = logits.shape[0]
    lt = logits.T
    sig_all = _sigmoid(lt[0:N_EXP, :])
    bias_all = sig_all + rb_ref[...]
    sig = [sig_all[e:e + 1, :] for e in range(N_EXP)]
    biased = [bias_all[e:e + 1, :] for e in range(N_EXP)]
    best, i1, i2, w1, w2 = _route_rows(sig, biased)
    if not sparse:
        h2_ref, comb_ref = refs[11:13]
        h2_ref[...] = h2.astype(bf16)
        comb = [jnp.where(i1 == e, w1, 0.0) + jnp.where(i2 == e, w2, 0.0) for e in range(N_EXP)]
        comb_t = jnp.concatenate(comb + [jnp.zeros((LANES - N_EXP, tm), f32)], axis=0)
        comb_ref[...] = comb_t.T
        return

    h2x_ref, plan_ref, counts_ref, base_scr = refs[11:15]
    swap = i1 > i2
    wa = jnp.where(swap, w2, w1)
    wb = jnp.where(swap, w1, w2)
    a = jnp.minimum(i1, i2) & (EXP_PER_GRP - 1)
    b = jnp.maximum(i1, i2) & (EXP_PER_GRP - 1)
    pair = jnp.where(a == 0, b - 1, jnp.where(a == 1, b + 1, 5))
    cls = best * N_PAIR + pair
    sub = lax.broadcasted_iota(jnp.int32, (CLS_PAD, tm), 0)
    onehot = (sub == cls).astype(f32)
    before = (lax.broadcasted_iota(jnp.int32, (tm, tm), 0)
              < lax.broadcasted_iota(jnp.int32, (tm, tm), 1)).astype(bf16)
    prefix = _dot(onehot.astype(bf16), before)

    @pl.when(pl.program_id(0) == 0)
    def _():
        base_scr[...] = jnp.zeros_like(base_scr)

    base = base_scr[...]
    rank = jnp.sum(onehot * (prefix + base[:, 0:1]), axis=0, keepdims=True)
    base = base + jnp.sum(onehot, axis=1, keepdims=True)
    base_scr[...] = base
    counts_ref[...] = base
    plan_ref[...] = jnp.concatenate(
        [cls, rank.astype(jnp.int32), jnp.zeros((6, tm), jnp.int32)], axis=0)
    ext_t = jnp.concatenate([wa, wb, jnp.zeros((MOE_EXT - 2, tm), f32)], axis=0)
    h2x_ref[:, 0:D] = h2
    h2x_ref[:, D:D + MOE_EXT] = ext_t.T


def _merge(l, x, bra, brb, gate, mod, wo, n2w, wr_hi, wr_lo, rb, *, seq3d, sparse):
    if seq3d:
        ns = _seq_tile(x.shape[0], 32)
        ntok = x.shape[0] * x.shape[1]
        tm = ns * x.shape[1]
        nt = x.shape[0] // ns
        x_spec = pl.BlockSpec((ns, x.shape[1], D), lambda i: (i, 0, 0))
        mod_spec = pl.BlockSpec((None, ns, 6, D), lambda i: (l, i, 0, 0))
    else:
        ntok = x.shape[0]
        nseq = mod.shape[1]
        tm = min(512, ntok // nseq)
        nt = ntok // tm
        per_seq = ntok // nseq // tm
        x_spec = pl.BlockSpec((tm, D), lambda i: (i, 0))
        mod_spec = pl.BlockSpec((None, None, 6, D), lambda i: (l, i // per_seq, 0, 0))
    row = lambda i: (i, 0)
    const = lambda i: (0, 0)
    if sparse:
        out_specs = (x_spec, pl.BlockSpec((tm, D + MOE_EXT), row), pl.BlockSpec((8, tm), lambda i: (0, i)),
                     pl.BlockSpec((CLS_PAD, LANES), const))
        out_shape = (jax.ShapeDtypeStruct(x.shape, f32), jax.ShapeDtypeStruct((ntok, D + MOE_EXT), f32),
                     jax.ShapeDtypeStruct((8, ntok), jnp.int32), jax.ShapeDtypeStruct((CLS_PAD, LANES), f32))
        scratch = [pltpu.VMEM((CLS_PAD, LANES), f32)]
    else:
        out_specs = (x_spec, pl.BlockSpec((tm, D), row), pl.BlockSpec((tm, LANES), row))
        out_shape = (jax.ShapeDtypeStruct(x.shape, f32), jax.ShapeDtypeStruct((ntok, D), bf16),
                     jax.ShapeDtypeStruct((ntok, LANES), f32))
        scratch = []
    return pl.pallas_call(
        functools.partial(_merge_kernel, seq3d=seq3d, sparse=sparse),
        grid=(nt,),
        in_specs=[
            x_spec,
            pl.BlockSpec((tm, D), row),
            pl.BlockSpec((tm, D), row),
            pl.BlockSpec((tm, 2 * D), row),
            mod_spec,
            pl.BlockSpec((None, D, D), lambda i: (l, 0, 0)),
            pl.BlockSpec((None, 1, D), lambda i: (l, 0, 0)),
            pl.BlockSpec((D, LANES), const),
            pl.BlockSpec((D, LANES), const),
            pl.BlockSpec((N_EXP, 1), const),
        ],
        out_specs=out_specs,
        out_shape=out_shape,
        scratch_shapes=scratch,
        compiler_params=pltpu.CompilerParams(
            dimension_semantics=("arbitrary",), vmem_limit_bytes=VMEM_LIMIT),
        name="merge",
    )(x, bra, brb, gate, mod, wo, n2w, wr_hi, wr_lo, rb)


def _dispatch_kernel(pos_ref, src_ref, buf_in, xs_hbm, sem):
    del buf_in
    tm = src_ref.shape[0]

    def issue(t, c):
        p = pos_ref[0, t]
        pltpu.make_async_copy(src_ref.at[pl.ds(t, 1), :], xs_hbm.at[pl.ds(p, 1), :], sem).start()
        return c

    lax.fori_loop(0, tm, issue, 0, unroll=8)

    def drain(t, c):
        pltpu.make_async_copy(src_ref.at[pl.ds(0, 1), :], xs_hbm.at[pl.ds(0, 1), :], sem).wait()
        return c

    lax.fori_loop(0, tm, drain, 0, unroll=8)


def _dispatch(pos3, h2x, xs_buf):
    nt, _, tm = pos3.shape
    return pl.pallas_call(
        _dispatch_kernel,
        grid=(nt,),
        in_specs=[
            pl.BlockSpec((None, 1, tm), lambda i: (i, 0, 0), memory_space=pltpu.SMEM),
            pl.BlockSpec((tm, D + MOE_EXT), lambda i: (i, 0)),
            pl.BlockSpec(memory_space=pl.ANY),
        ],
        out_specs=pl.BlockSpec(memory_space=pl.ANY),
        out_shape=jax.ShapeDtypeStruct(xs_buf.shape, f32),
        input_output_aliases={2: 0},
        scratch_shapes=[pltpu.SemaphoreType.DMA(())],
        compiler_params=pltpu.CompilerParams(dimension_semantics=("arbitrary",)),
        name="dispatch",
    )(pos3, h2x, xs_buf)


def _moe_sparse_kernel(src_ref, valid_ref, ea_ref, eb_ref, xs_ref,
                       wga_ref, wua_ref, wda_ref, wgb_ref, wub_ref, wdb_ref, ys_ref):
    del src_ref, ea_ref, eb_ref

    @pl.when(valid_ref[pl.program_id(0)] == 1)
    def _():
        x = xs_ref[...]
        h = x[:, 0:D].astype(bf16)

        def ffn(wg, wu, wd):
            t = (_silu(_dot(h, wg[...])) * _dot(h, wu[...])).astype(bf16)
            return _dot(t, wd[...])

        ys_ref[...] = (x[:, D:D + 1] * ffn(wga_ref, wua_ref, wda_ref)
                       + x[:, D + 1:D + 2] * ffn(wgb_ref, wub_ref, wdb_ref))

    @pl.when(valid_ref[pl.program_id(0)] == 0)
    def _():
        ys_ref[...] = jnp.zeros(ys_ref.shape, f32)


def _moe_sparse(l, tiles, xs, wg, wu, wd):
    src, valid, ea, eb = tiles
    nrow = xs.shape[0]
    ntile = nrow // MOE_R
    wa_map = lambda t, src, valid, ea, eb: (l, ea[t], 0, 0)
    wb_map = lambda t, src, valid, ea, eb: (l, eb[t], 0, 0)
    row_map = lambda t, src, valid, ea, eb: (src[t], 0)
    return pl.pallas_call(
        _moe_sparse_kernel,
        grid_spec=pltpu.PrefetchScalarGridSpec(
            num_scalar_prefetch=4,
            grid=(ntile,),
            in_specs=[
                pl.BlockSpec((MOE_R, D + MOE_EXT), row_map),
                pl.BlockSpec((None, None, D, EXP_FF), wa_map),
                pl.BlockSpec((None, None, D, EXP_FF), wa_map),
                pl.BlockSpec((None, None, EXP_FF, D), wa_map),
                pl.BlockSpec((None, None, D, EXP_FF), wb_map),
                pl.BlockSpec((None, None, D, EXP_FF), wb_map),
                pl.BlockSpec((None, None, EXP_FF, D), wb_map),
            ],
            out_specs=pl.BlockSpec((MOE_R, D), lambda t, src, valid, ea, eb: (t, 0)),
        ),
        out_shape=jax.ShapeDtypeStruct((nrow, D), f32),
        compiler_params=pltpu.CompilerParams(
            dimension_semantics=("arbitrary",), vmem_limit_bytes=VMEM_LIMIT),
        name="moe_sparse",
    )(src, valid, ea, eb, xs, wg, wu, wd, wg, wu, wd)


def _combine_kernel(pos_ref, ys_hbm, x1_ref, mod_ref, x2_ref, buf, sem):
    tm = x1_ref.shape[0]

    def issue(t, c):
        p = pos_ref[0, t]
        pltpu.make_async_copy(ys_hbm.at[pl.ds(p, 1), :], buf.at[pl.ds(t, 1), :], sem).start()
        return c

    lax.fori_loop(0, tm, issue, 0, unroll=8)

    def drain(t, c):
        pltpu.make_async_copy(ys_hbm.at[pl.ds(0, 1), :], buf.at[pl.ds(0, 1), :], sem).wait()
        return c

    lax.fori_loop(0, tm, drain, 0, unroll=8)
    x2_ref[...] = x1_ref[...] + mod_ref[5:6, :] * buf[...]


def _combine(l, pos3, ys, x1, mod):
    nt, _, tm = pos3.shape
    ntok = x1.shape[0]
    per_seq = ntok // mod.shape[1] // tm
    return pl.pallas_call(
        _combine_kernel,
        grid=(nt,),
        in_specs=[
            pl.BlockSpec((None, 1, tm), lambda i: (i, 0, 0), memory_space=pltpu.SMEM),
            pl.BlockSpec(memory_space=pl.ANY),
            pl.BlockSpec((tm, D), lambda i: (i, 0)),
            pl.BlockSpec((None, None, 6, D), lambda i: (l, i // per_seq, 0, 0)),
        ],
        out_specs=pl.BlockSpec((tm, D), lambda i: (i, 0)),
        out_shape=jax.ShapeDtypeStruct((ntok, D), f32),
        scratch_shapes=[pltpu.VMEM((tm, D), f32), pltpu.SemaphoreType.DMA(())],
        compiler_params=pltpu.CompilerParams(dimension_semantics=("arbitrary",)),
        name="combine",
    )(pos3, ys, x1, mod)


def _moe_plan(plan, counts, ntile):
    cnt = counts[:N_CLS, 0].astype(jnp.int32)
    tiles_per = (cnt + (MOE_R - 1)) // MOE_R
    tstart = jnp.cumsum(tiles_per) - tiles_per
    total = jnp.sum(tiles_per)
    pos = tstart[plan[0]] * MOE_R + plan[1]
    t = jnp.arange(ntile, dtype=jnp.int32)
    src = jnp.minimum(t, total - 1)
    tcls = jnp.sum((src[:, None] >= tstart[None, :]).astype(jnp.int32), axis=1) - 1
    grp = tcls // N_PAIR
    pair = tcls % N_PAIR
    pa = jnp.asarray(np.array([0, 0, 0, 1, 1, 2], np.int32))
    pb = jnp.asarray(np.array([1, 2, 3, 2, 3, 3], np.int32))
    ea = grp * EXP_PER_GRP + pa[pair]
    eb = grp * EXP_PER_GRP + pb[pair]
    valid = (t < total).astype(jnp.int32)
    return pos, (src, valid, ea, eb)


def _moe_kernel(h2_ref, comb_ref, wg_ref, wu_ref, wd_ref, x1_ref, mod_ref, x2_ref, acc_scr, *, seq3d):
    e = pl.program_id(1)

    @pl.when(e == 0)
    def _():
        acc_scr[...] = jnp.zeros_like(acc_scr)

    h = h2_ref[...]
    a = _dot(h, wg_ref[...])
    b = _dot(h, wu_ref[...])
    t = (_silu(a) * b).astype(bf16)
    ye = _dot(t, wd_ref[...])
    lane = lax.broadcasted_iota(jnp.int32, comb_ref.shape, 1)
    w = jnp.sum(jnp.where(lane == e, comb_ref[...], 0.0), axis=-1, keepdims=True)
    acc_scr[...] += w * ye

    @pl.when(e == N_EXP - 1)
    def _():
        x1 = x1_ref[...]
        if seq3d:
            x2_ref[...] = x1 + mod_ref[:, 5:6, :] * acc_scr[...].reshape(x1.shape)
        else:
            x2_ref[...] = x1 + mod_ref[5:6, :] * acc_scr[...]


def _moe(l, h2, comb, wg, wu, wd, x1, mod, *, seq3d):
    if seq3d:
        ns = x1.shape[0]
        ntok = ns * x1.shape[1]
        tm = ntok
        nt = 1
        x_spec = pl.BlockSpec((ns, x1.shape[1], D), lambda i, e: (0, 0, 0))
        mod_spec = pl.BlockSpec((None, ns, 6, D), lambda i, e: (l, 0, 0, 0))
    else:
        ntok = x1.shape[0]
        nseq = mod.shape[1]
        tm = min(1024, ntok // nseq)
        nt = ntok // tm
        per_seq = ntok // nseq // tm
        x_spec = pl.BlockSpec((tm, D), lambda i, e: (i, 0))
        mod_spec = pl.BlockSpec((None, None, 6, D), lambda i, e: (l, i // per_seq, 0, 0))
    row = lambda i, e: (i, 0)
    return pl.pallas_call(
        functools.partial(_moe_kernel, seq3d=seq3d),
        grid=(nt, N_EXP),
        in_specs=[
            pl.BlockSpec((tm, D), row),
            pl.BlockSpec((tm, LANES), row),
            pl.BlockSpec((None, None, D, EXP_FF), lambda i, e: (l, e, 0, 0)),
            pl.BlockSpec((None, None, D, EXP_FF), lambda i, e: (l, e, 0, 0)),
            pl.BlockSpec((None, None, EXP_FF, D), lambda i, e: (l, e, 0, 0)),
            x_spec,
            mod_spec,
        ],
        out_specs=x_spec,
        out_shape=jax.ShapeDtypeStruct(x1.shape, f32),
        scratch_shapes=[pltpu.VMEM((tm, D), f32)],
        compiler_params=pltpu.CompilerParams(
            dimension_semantics=("arbitrary", "arbitrary"), vmem_limit_bytes=VMEM_LIMIT),
        name="moe",
    )(h2, comb, wg, wu, wd, x1, mod)


def _final_norm_kernel(x_ref, w_ref, o_ref):
    x = x_ref[...]
    o_ref[...] = x * lax.rsqrt(jnp.mean(x * x, axis=-1, keepdims=True) + EPS) * w_ref[...]


def _final_norm(x2d, w):
    ntok = x2d.shape[0]
    tm = min(1024, ntok)
    return pl.pallas_call(
        _final_norm_kernel,
        grid=(ntok // tm,),
        in_specs=[pl.BlockSpec((tm, D), lambda i: (i, 0)), pl.BlockSpec((1, D), lambda i: (0, 0))],
        out_specs=pl.BlockSpec((tm, D), lambda i: (i, 0)),
        out_shape=jax.ShapeDtypeStruct((ntok, D), f32),
        compiler_params=pltpu.CompilerParams(dimension_semantics=("arbitrary",)),
        name="final_norm",
    )(x2d, w)


def _prep_in_weights(w_in):
    o = 0
    z = w_in[:, :, o:o + SSD_INNER]; o += SSD_INNER
    xbc = w_in[:, :, o:o + CONV_DIM]; o += CONV_DIM
    dt = w_in[:, :, o:o + SSD_NH]; o += SSD_NH
    q = w_in[:, :, o:o + GLA_KD] * (GLA_DK ** -0.5); o += GLA_KD
    kvr = w_in[:, :, o:o + GLA_KD + 2 * GLA_VD]; o += GLA_KD + 2 * GLA_VD
    glr = w_in[:, :, o:o + GLA_RANK]; o += GLA_RANK
    gates = w_in[:, :, o:o + 2 * D]
    main = jnp.concatenate([z, xbc, q, kvr, gates], axis=-1).astype(bf16)
    pad = jnp.zeros(w_in.shape[:2] + (LANES - SSD_NH - GLA_RANK,), w_in.dtype)
    tail = jnp.concatenate([dt, glr, pad], axis=-1).astype(bf16)
    return main, tail


def _pad_lanes(v, width=LANES):
    return jnp.concatenate([v, jnp.zeros(v.shape[:-1] + (width - v.shape[-1],), v.dtype)], axis=-1)


def kernel(x_prompt, x_sample, c_prompt, c_sample, state_conv, state_ssm, state_gla, w_ada, b_ada, norm1_w, w_in, conv_w, conv_b, dt_bias, a_log, d_skip, ssd_norm_w, w_ssd_br, gla_gate_up, gla_gate_b, gla_norm_w, w_gla_br, merge_b, w_out, norm2_w, w_router, router_bias, w_exp_gate, w_exp_up, w_exp_down, final_norm_w):
    bp, sp, _ = x_prompt.shape
    bs, ss, _ = x_sample.shape

    mod = _adaln(jnp.concatenate([c_prompt, c_sample], axis=0), w_ada, b_ada)
    mod_p = mod[:, :bp].reshape(DEPTH, bp, 6, D)
    mod_s = mod[:, bp:].reshape(DEPTH, bs, 6, D)

    w_main, w_tail = _prep_in_weights(w_in)
    gate_up = jnp.concatenate(
        [jnp.zeros((DEPTH, TAIL_GLR0, GLA_KD), f32), gla_gate_up,
         jnp.zeros((DEPTH, LANES - TAIL_GLR0 - GLA_RANK, GLA_KD), f32)], axis=1).astype(bf16)
    in_w = (norm1_w.reshape(DEPTH, 1, D), w_main, w_tail, gate_up, gla_gate_b.reshape(DEPTH, 1, GLA_KD),
            conv_w, conv_b.reshape(DEPTH, 1, CONV_DIM), merge_b.reshape(DEPTH, 1, 2 * D))
    bigsel, e8 = _ssd_consts()
    ssd_w = (_pad_lanes(dt_bias).reshape(DEPTH, 1, LANES), _pad_lanes(a_log).reshape(DEPTH, 1, LANES),
             jnp.repeat(d_skip, SSD_HD, axis=-1).reshape(DEPTH, SSD_NG, 1, SSD_GW),
             ssd_norm_w.reshape(DEPTH, SSD_NG, 1, SSD_GW),
             w_ssd_br.astype(bf16).reshape(DEPTH, SSD_NG, SSD_GW, D), bigsel, e8)
    gla_w = (gla_norm_w.reshape(DEPTH, 1, GLA_DV), w_gla_br.astype(bf16))
    w_out_b = w_out.astype(bf16)
    n2 = norm2_w.reshape(DEPTH, 1, D)
    wg_b = w_exp_gate.astype(bf16)
    wu_b = w_exp_up.astype(bf16)
    wd_b = w_exp_down.astype(bf16)
    wr = _pad_lanes(w_router)
    wr_hi = wr.astype(bf16)
    wr_lo = (wr - wr_hi.astype(f32)).astype(bf16)
    rb = router_bias.reshape(N_EXP, 1)

    xp = x_prompt.reshape(bp * sp, D)
    xs = x_sample
    ssm_s = state_ssm.reshape(DEPTH, bs, SSD_NG, SSD_GW, SSD_NS)
    ssd_lp = min(128, sp)
    gla_lp = min(64, sp)
    cp = sp_ = gp = cs_ = ss_ = gs = None
    moe_tm = min(MOE_TM, sp)
    moe_tiles = -(-(bp * sp) // MOE_R) + N_CLS
    xs_rows = jnp.zeros((moe_tiles * MOE_R, D + MOE_EXT), f32)
    for l in range(DEPTH):
        z4, xs4, bc, qkvr, gate, tail, lg, cp = _inproj(l, xp, mod_p, None, cp, in_w, seq3d=False)
        bra, sp_ = _ssd(l, xs4, bc, z4, tail, None, sp_, ssd_w, nseq=bp, L=ssd_lp, sb=1, nchunks=sp // ssd_lp)
        brb, gp = _gla(l, qkvr, lg, None, gp, gla_w, nseq=bp, L=gla_lp, C1=16, sb=2, nchunks=sp // gla_lp)
        x1, h2x, plan, counts = _merge(l, xp, bra, brb, gate, mod_p, w_out_b, n2, wr_hi, wr_lo, rb,
                                       seq3d=False, sparse=True)
        pos, tiles = _moe_plan(plan, counts, moe_tiles)
        pos3 = pos.reshape(bp * sp // moe_tm, 1, moe_tm)
        xs_rows = _dispatch(pos3, h2x, xs_rows)
        ys_rows = _moe_sparse(l, tiles, xs_rows, wg_b, wu_b, wd_b)
        xp = _combine(l, pos3, ys_rows, x1, mod_p)

        z4, xs4, bc, qkvr, gate, tail, lg, cs_ = _inproj(l, xs, mod_s, state_conv, cs_, in_w, seq3d=True)
        bra, ss_ = _ssd(l, xs4, bc, z4, tail, ssm_s, ss_, ssd_w, nseq=bs, L=ss, sb=4, nchunks=1)
        brb, gs = _gla(l, qkvr, lg, state_gla, gs, gla_w, nseq=bs, L=ss, C1=ss, sb=8, nchunks=1)
        x1, h2, comb = _merge(l, xs, bra, brb, gate, mod_s, w_out_b, n2, wr_hi, wr_lo, rb,
                              seq3d=True, sparse=False)
        xs = _moe(l, h2, comb, wg_b, wu_b, wd_b, x1, mod_s, seq3d=True)

    fw = final_norm_w.reshape(1, D)
    y_prompt = _final_norm(xp, fw).reshape(bp, sp, D)
    y_sample = _final_norm(xs.reshape(bs * ss, D), fw).reshape(bs, ss, D)
    return (y_prompt, y_sample, cp, sp_.reshape(DEPTH, bp, SSD_NH, SSD_HD, SSD_NS), gp,
            cs_, ss_.reshape(DEPTH, bs, SSD_NH, SSD_HD, SSD_NS), gs)
```

```python
import functools

import numpy as np
import jax
import jax.numpy as jnp
from jax import lax
from jax.experimental import pallas as pl
from jax.experimental.pallas import tpu as pltpu

f32 = jnp.float32
bf16 = jnp.bfloat16

D = 1024
DEPTH = 4
SSD_INNER = 2048
SSD_HD = 64
SSD_NH = 32
SSD_NS = 128
SSD_NG = 4
SSD_HPG = 8
SSD_GW = SSD_INNER // SSD_NG
CONV_K = 4
CONV_DIM = SSD_INNER + 2 * SSD_NG * SSD_NS
GLA_NH = 4
GLA_DK = 128
GLA_DV = 256
GLA_KD = GLA_NH * GLA_DK
GLA_VD = GLA_NH * GLA_DV
GLA_RANK = 16
GLA_TAU = 16.0
N_EXP = 16
N_EGRP = 4
EXP_PER_GRP = 4
EXP_FF = 512
N_PAIR = 6
N_CLS = N_EGRP * N_PAIR
CLS_PAD = 32
MOE_EXT = 128
MOE_R = 256
MOE_TM = 512
EPS = 1e-6
LANES = 128
TAIL_GLR0 = SSD_NH
MAIN_W = SSD_INNER + CONV_DIM + 2 * GLA_KD + 2 * GLA_VD + 2 * D
VMEM_LIMIT = 56 * 1024 * 1024


def _sigmoid(x):
    return 1.0 / (1.0 + jnp.exp(-x))


def _silu(x):
    return x * _sigmoid(x)


def _softplus(x):
    return jnp.maximum(x, 0.0) + jnp.log1p(jnp.exp(-jnp.abs(x)))


def _log_sigmoid(x):
    return jnp.minimum(x, 0.0) - jnp.log1p(jnp.exp(-jnp.abs(x)))


def _dot(a, b):
    return jnp.dot(a, b, preferred_element_type=f32)


def _dot_nt(a, b):
    return lax.dot_general(a, b, (((1,), (1,)), ((), ())), preferred_element_type=f32)


def _dot_tn(a, b):
    return lax.dot_general(a, b, (((0,), (0,)), ((), ())), preferred_element_type=f32)


def _split2(x):
    hi = x.astype(bf16)
    return hi, (x - hi.astype(f32)).astype(bf16)


def _split3(x):
    hi = x.astype(bf16)
    r1 = x - hi.astype(f32)
    mid = r1.astype(bf16)
    lo = (r1 - mid.astype(f32)).astype(bf16)
    return hi, mid, lo


def _cumsum_rows(x, tri):
    hi, mid, lo = _split3(x)
    return _dot(tri, hi) + _dot(tri, mid) + _dot(tri, lo)


def _tri(n):
    r = lax.broadcasted_iota(jnp.int32, (n, n), 0)
    c = lax.broadcasted_iota(jnp.int32, (n, n), 1)
    return r >= c


def _norm_mod(x, mod_ref, w_ref, i_shift, i_scale, seq3d):
    ms = jnp.mean(x * x, axis=-1, keepdims=True)
    y = x * lax.rsqrt(ms + EPS) * w_ref[...]
    if seq3d:
        sc = mod_ref[:, i_scale:i_scale + 1, :]
        sh = mod_ref[:, i_shift:i_shift + 1, :]
        h = y * (1.0 + sc) + sh
        return h.reshape(h.shape[0] * h.shape[1], h.shape[2])
    sc = mod_ref[i_scale:i_scale + 1, :]
    sh = mod_ref[i_shift:i_shift + 1, :]
    return y * (1.0 + sc) + sh


def _seq_tile(nseq, want):
    return want if nseq % want == 0 else nseq


def _stacked_out(prev, in_specs, args, aliases, out_index):
    if prev is not None:
        in_specs.append(pl.BlockSpec(memory_space=pl.ANY))
        args.append(prev)
        aliases[len(args) - 1] = out_index


def _adaln_kernel(c_ref, w_ref, b_ref, o_ref):
    s = _silu(c_ref[...]).astype(bf16)
    o_ref[...] = _dot(s, w_ref[...].astype(bf16)) + b_ref[...]


def _adaln(c_all, w_ada, b_ada):
    n = c_all.shape[0]
    tn = 1024
    return pl.pallas_call(
        _adaln_kernel,
        grid=(DEPTH, 6 * D // tn),
        in_specs=[
            pl.BlockSpec((n, D), lambda l, j: (0, 0)),
            pl.BlockSpec((None, D, tn), lambda l, j: (l, 0, j)),
            pl.BlockSpec((None, 1, tn), lambda l, j: (l, 0, j)),
        ],
        out_specs=pl.BlockSpec((None, n, tn), lambda l, j: (l, 0, j)),
        out_shape=jax.ShapeDtypeStruct((DEPTH, n, 6 * D), f32),
        compiler_params=pltpu.CompilerParams(
            dimension_semantics=("arbitrary", "arbitrary"), vmem_limit_bytes=VMEM_LIMIT),
        name="adaln",
    )(c_all, w_ada, b_ada.reshape(DEPTH, 1, 6 * D))


IN_TN = 1024
IN_NJ = MAIN_W // IN_TN
IN_SUB = 256
IN_NSUB = IN_TN // IN_SUB
CONV_PAD = 8
KPREV = CONV_K - 1


def _inproj_kernel(*refs, seq3d, has_state, per_seq, T):
    if has_state:
        (x_ref, mod_ref, n1_ref, wm_ref, wt_ref, wg_ref, gb_ref, cw_ref, cb_ref, mb_ref, conv0_ref) = refs[:11]
        rest = refs[11:]
    else:
        (x_ref, mod_ref, n1_ref, wm_ref, wt_ref, wg_ref, gb_ref, cw_ref, cb_ref, mb_ref) = refs[:10]
        conv0_ref = None
        rest = refs[10:]
    (z4_ref, xs4_ref, bc_ref, qkvr_ref, gate_ref, tail_ref, lg_ref, convn_ref,
     h_scr, cscr, carry) = rest[-11:]
    i = pl.program_id(0)
    j = pl.program_id(1)
    tm = h_scr.shape[0]
    ns = cscr.shape[1]

    @pl.when(j == 0)
    def _():
        h = _norm_mod(x_ref[...], mod_ref, n1_ref, 0, 1, seq3d).astype(bf16)
        h_scr[...] = h
        t = _dot(h, wt_ref[...])
        tail_ref[...] = t
        pre = _dot(t.astype(bf16), wg_ref[...]) + gb_ref[...]
        lg_ref[...] = _log_sigmoid(pre) * (1.0 / GLA_TAU)

    def sub_dot(c):
        return _dot(h_scr[...], wm_ref[:, IN_SUB * c:IN_SUB * (c + 1)])

    def conv_silu(a, cj, c):
        cols = slice(IN_SUB * c, IN_SUB * (c + 1))
        gcols = slice(IN_TN * cj + IN_SUB * c, IN_TN * cj + IN_SUB * (c + 1))
        if not has_state:
            prev = jnp.where(i % per_seq == 0, 0.0, carry[cj, c, CONV_PAD - KPREV:CONV_PAD, :])
            cscr[c, 0, CONV_PAD - KPREV:CONV_PAD, :] = prev
            cscr[c, 0, CONV_PAD:2 * CONV_PAD, :] = a[0:CONV_PAD]
            cscr[c, 0, 2 * CONV_PAD:3 * CONV_PAD, :] = a[tm - CONV_PAD:tm]
            w = [cw_ref[k:k + 1, cols] for k in range(CONV_K)]
            out = cb_ref[:, cols] + a * w[KPREV]
            head = cb_ref[:, cols] + a[0:CONV_PAD] * w[KPREV]
            for k in range(KPREV):
                out = out + pltpu.roll(a, KPREV - k, 0) * w[k]
                head = head + cscr[c, 0, CONV_PAD - KPREV + k:2 * CONV_PAD - KPREV + k, :] * w[k]
            out = jnp.concatenate([head, out[CONV_PAD:]], axis=0)
            new_tail = cscr[c, 0, 3 * CONV_PAD - KPREV:3 * CONV_PAD, :]
            convn_ref[0, :, gcols] = new_tail
            carry[cj, c, CONV_PAD - KPREV:CONV_PAD, :] = new_tail
            return _silu(out).astype(bf16)
        a3 = a.reshape(ns, T, IN_SUB)
        prev = conv0_ref[:, :, cols]
        cscr[c, :, CONV_PAD - KPREV:CONV_PAD, :] = prev
        cscr[c, :, CONV_PAD:CONV_PAD + T, :] = a3
        out = cb_ref[:, cols] + a3 * cw_ref[KPREV:KPREV + 1, cols]
        for k in range(KPREV):
            out = out + cscr[c, :, CONV_PAD - KPREV + k:CONV_PAD - KPREV + k + T, :] * cw_ref[k:k + 1, cols]
        new_tail = cscr[c, :, CONV_PAD + T - KPREV:CONV_PAD + T, :]
        convn_ref[:, :, gcols] = new_tail
        return _silu(out).reshape(tm, IN_SUB).astype(bf16)

    per_grp = SSD_GW // IN_SUB

    @pl.when(j < 2)
    def _():
        for c in range(IN_NSUB):
            lanes = slice(IN_SUB * (c % per_grp), IN_SUB * (c % per_grp + 1))
            z4_ref[c // per_grp, :, lanes] = _silu(sub_dot(c)).astype(bf16)

    for cj in range(2):
        @pl.when(j == 2 + cj)
        def _(cj=cj):
            for c in range(IN_NSUB):
                lanes = slice(IN_SUB * (c % per_grp), IN_SUB * (c % per_grp + 1))
                xs4_ref[c // per_grp, :, lanes] = conv_silu(sub_dot(c), cj, c)

    @pl.when(j == 4)
    def _():
        for c in range(IN_NSUB):
            xc = conv_silu(sub_dot(c), 2, c)
            for q in range(IN_SUB // SSD_NS):
                bc_ref[c * (IN_SUB // SSD_NS) + q] = xc[:, SSD_NS * q:SSD_NS * (q + 1)]

    @pl.when(jnp.logical_and(j >= 5, j < 7))
    def _():
        for c in range(IN_NSUB):
            qkvr_ref[:, IN_SUB * c:IN_SUB * (c + 1)] = sub_dot(c).astype(bf16)

    @pl.when(j == 7)
    def _():
        for c in range(IN_NSUB):
            qkvr_ref[:, IN_SUB * c:IN_SUB * (c + 1)] = _silu(sub_dot(c)).astype(bf16)

    @pl.when(j >= 8)
    def _():
        for c in range(IN_NSUB):
            cols = slice(IN_SUB * c, IN_SUB * (c + 1))
            gate_ref[:, cols] = _sigmoid(sub_dot(c) + mb_ref[:, cols]).astype(bf16)


def _inproj(l, x, mod, conv0, convn_prev, w, *, seq3d):
    (n1w, w_main, w_tail, gate_up, gate_b, conv_w, conv_b, merge_b) = w
    has_state = conv0 is not None
    if seq3d:
        nseq, T = x.shape[0], x.shape[1]
        ns = _seq_tile(nseq, 128)
        ntok = nseq * T
        tm = ns * T
        nt = nseq // ns
        per_seq = 1
        x_spec = pl.BlockSpec((ns, T, D), lambda i, j: (i, 0, 0))
        mod_spec = pl.BlockSpec((None, ns, 6, D), lambda i, j: (l, i, 0, 0))
        seq_blk = lambda i: i
    else:
        nseq = mod.shape[1]
        ntok = x.shape[0]
        tm = min(1024, ntok // nseq)
        T = tm
        ns = 1
        nt = ntok // tm
        per_seq = ntok // nseq // tm
        x_spec = pl.BlockSpec((tm, D), lambda i, j: (i, 0))
        mod_spec = pl.BlockSpec((None, None, 6, D), lambda i, j: (l, i // per_seq, 0, 0))
        seq_blk = lambda i: i // per_seq

    def cj(j):
        return jnp.clip(j - 2, 0, 2)

    in_specs = [
        x_spec,
        mod_spec,
        pl.BlockSpec((None, 1, D), lambda i, j: (l, 0, 0)),
        pl.BlockSpec((None, D, IN_TN), lambda i, j: (l, 0, j)),
        pl.BlockSpec((None, D, LANES), lambda i, j: (l, 0, 0)),
        pl.BlockSpec((None, LANES, GLA_KD), lambda i, j: (l, 0, 0)),
        pl.BlockSpec((None, 1, GLA_KD), lambda i, j: (l, 0, 0)),
        pl.BlockSpec((None, CONV_K, IN_TN), lambda i, j: (l, 0, cj(j))),
        pl.BlockSpec((None, 1, IN_TN), lambda i, j: (l, 0, cj(j))),
        pl.BlockSpec((None, 1, IN_TN), lambda i, j: (l, 0, jnp.clip(j - 8, 0, 1))),
    ]
    args = [x, mod, n1w, w_main, w_tail, gate_up, gate_b, conv_w, conv_b, merge_b]
    if has_state:
        in_specs.append(pl.BlockSpec((None, ns, KPREV, IN_TN), lambda i, j: (l, i, 0, cj(j))))
        args.append(conv0)
    aliases = {}
    _stacked_out(convn_prev, in_specs, args, aliases, 7)
    out_shapes = (
        jax.ShapeDtypeStruct((SSD_NG, ntok, SSD_GW), bf16),
        jax.ShapeDtypeStruct((SSD_NG, ntok, SSD_GW), bf16),
        jax.ShapeDtypeStruct((2 * SSD_NG, ntok, SSD_NS), bf16),
        jax.ShapeDtypeStruct((ntok, 2 * GLA_KD + 2 * GLA_VD), bf16),
        jax.ShapeDtypeStruct((ntok, 2 * D), bf16),
        jax.ShapeDtypeStruct((ntok, LANES), f32),
        jax.ShapeDtypeStruct((ntok, GLA_KD), f32),
        jax.ShapeDtypeStruct((DEPTH, nseq, KPREV, CONV_DIM), f32),
    )
    out_specs = (
        pl.BlockSpec((2, tm, SSD_GW), lambda i, j: (jnp.clip(j, 0, 1), i, 0)),
        pl.BlockSpec((2, tm, SSD_GW), lambda i, j: (jnp.clip(j - 2, 0, 1), i, 0)),
        pl.BlockSpec((2 * SSD_NG, tm, SSD_NS), lambda i, j: (0, i, 0)),
        pl.BlockSpec((tm, IN_TN), lambda i, j: (i, jnp.clip(j - 5, 0, 2))),
        pl.BlockSpec((tm, IN_TN), lambda i, j: (i, jnp.clip(j - 8, 0, 1))),
        pl.BlockSpec((tm, LANES), lambda i, j: (i, 0)),
        pl.BlockSpec((tm, GLA_KD), lambda i, j: (i, 0)),
        pl.BlockSpec((None, ns, KPREV, CONV_DIM), lambda i, j: (l, seq_blk(i), 0, 0)),
    )
    return pl.pallas_call(
        functools.partial(_inproj_kernel, seq3d=seq3d, has_state=has_state, per_seq=per_seq, T=T),
        grid=(nt, IN_NJ),
        in_specs=in_specs,
        out_specs=out_specs,
        out_shape=out_shapes,
        input_output_aliases=aliases,
        scratch_shapes=[
            pltpu.VMEM((tm, D), bf16),
            pltpu.VMEM((IN_NSUB, ns, CONV_PAD + T if has_state else 3 * CONV_PAD, IN_SUB), f32),
            pltpu.VMEM((3, IN_NSUB, CONV_PAD, IN_SUB), f32),
        ],
        compiler_params=pltpu.CompilerParams(
            dimension_semantics=("arbitrary", "arbitrary"), vmem_limit_bytes=VMEM_LIMIT),
        name="inproj",
    )(*args)


def _ssd_consts():
    bigsel = np.zeros((SSD_NG, 3 * LANES, SSD_HPG * LANES), np.float32)
    e8 = np.zeros((SSD_NG, 2 * LANES, SSD_GW), np.float32)
    for g in range(SSD_NG):
        for j in range(SSD_HPG):
            for k in range(3):
                bigsel[g, k * LANES + SSD_HPG * g + j, LANES * j:LANES * (j + 1)] = 1.0
            for k in range(2):
                e8[g, k * LANES + SSD_HPG * g + j, SSD_HD * j:SSD_HD * (j + 1)] = 1.0
    return jnp.asarray(bigsel, bf16), jnp.asarray(e8, bf16)


def _ssd_kernel(*refs, L, sb, nchunks, has_state):
    R = sb * L
    xs4_ref, bc_ref, z4_ref, tail_ref = refs[:4]
    k = 4
    ssm0_ref = None
    if has_state:
        ssm0_ref = refs[k]
        k += 1
    dtb_ref, alog_ref, dsk_ref, nw_ref, wbr_ref, bigsel_ref, e8_ref = refs[k:k + 7]
    br_ref, ssmn_ref, cst_scr = refs[-3:]
    c = pl.program_id(1)

    rowi = lax.broadcasted_iota(jnp.int32, (R, R), 0)
    coli = lax.broadcasted_iota(jnp.int32, (R, R), 1)
    mask = rowi >= coli
    if sb > 1:
        sh = L.bit_length() - 1
        mask = jnp.logical_and(mask, (rowi >> sh) == (coli >> sh))
    tri = mask.astype(bf16)
    lane = lax.broadcasted_iota(jnp.int32, (R, LANES), 1)
    lo_half = lane < SSD_HD

    def init():
        if has_state:
            ssmn_ref[...] = ssm0_ref[...]
        else:
            ssmn_ref[...] = jnp.zeros(ssmn_ref.shape, f32)

    if nchunks == 1:
        init()
    else:
        pl.when(c == 0)(init)

    dtp = _softplus(tail_ref[...] + dtb_ref[...])
    a = -jnp.exp(alog_ref[...])
    cs = _cumsum_rows(dtp * a, tri)
    cst_scr[...] = cs.T
    h3 = jnp.concatenate(_split3(cs), axis=1)
    d2 = jnp.concatenate(_split2(dtp), axis=1)
    br_ref[...] = jnp.zeros(br_ref.shape, f32)

    def seq_last(x):
        if sb == 1:
            return x[R - 1:R, :]
        x3 = x.reshape(sb, L, x.shape[-1])
        return jnp.broadcast_to(x3[:, L - 1:L, :], x3.shape).reshape(x.shape)

    def group_body(g, carry):
        cm = _dot(h3, bigsel_ref[g])
        dt_exp = _dot(d2, e8_ref[g])
        cs_exp = jnp.concatenate(
            [jnp.where(lo_half, cm[:, 2 * LANES * i:2 * LANES * i + LANES],
                       cm[:, 2 * LANES * i + LANES:2 * LANES * (i + 1)]) for i in range(SSD_HPG // 2)], axis=1)
        ecs = jnp.exp(cs_exp)
        ce = seq_last(cs_exp)
        xs = xs4_ref[g].astype(f32)
        xdt = xs * dt_exp
        xdt_b = xdt.astype(bf16)
        xse = xdt * jnp.exp(ce - cs_exp)
        bg = bc_ref[g]
        cg = bc_ref[SSD_NG + g]
        cb = _dot_nt(cg, bg)

        pairs = []
        for i in range(SSD_HPG // 2):
            ws = []
            for j in (2 * i, 2 * i + 1):
                row = cst_scr[pl.ds(SSD_HPG * g + j, 1), :]
                seg = cm[:, LANES * j:LANES * j + R] - row
                dec = jnp.exp(jnp.where(mask, seg, -jnp.inf))
                ws.append((cb * dec).astype(bf16))
            yy = _dot(jnp.concatenate(ws, axis=0), xdt_b[:, LANES * i:LANES * (i + 1)])
            pairs.append(jnp.where(lo_half, yy[:R], yy[R:]))
        y = jnp.concatenate(pairs, axis=1)

        if sb > 1:
            bgf = bg.astype(f32)
            cgf = cg.astype(f32)
        ys_parts = []
        for s in range(sb):
            rs = slice(s * L, (s + 1) * L)
            st = ssmn_ref[s, g]
            b_s = bg if sb == 1 else bgf[rs].astype(bf16)
            c_s = cg if sb == 1 else cgf[rs].astype(bf16)
            ys_parts.append(_dot_nt(c_s, st.astype(bf16)))
            upd = _dot_tn(xse[rs].astype(bf16), b_s)
            e_end = jnp.exp(ce[s * L:s * L + 1, :])
            ssmn_ref[s, g] = jnp.concatenate(
                [st[SSD_HD * j:SSD_HD * (j + 1)] * e_end[:, SSD_HD * j:SSD_HD * j + 1]
                 + upd[SSD_HD * j:SSD_HD * (j + 1)] for j in range(SSD_HPG)], axis=0)
        ys = ys_parts[0] if sb == 1 else jnp.concatenate(ys_parts, axis=0)
        y = y + ys * ecs + dsk_ref[g] * xs

        yg = y * z4_ref[g].astype(f32)
        yn = yg * lax.rsqrt(jnp.mean(yg * yg, axis=-1, keepdims=True) + EPS) * nw_ref[g]
        br_ref[...] += _dot(yn.astype(bf16), wbr_ref[g])
        return carry

    for g in range(SSD_NG):
        group_body(g, 0)


def _ssd(l, xs4, bc, z4, tail, ssm0, ssmn_prev, w, *, nseq, L, sb, nchunks):
    has_state = ssm0 is not None
    dtb, alog, dsk, nw, wbr, bigsel, e8 = w
    ntok = tail.shape[0]
    R = sb * L
    if nchunks == 1:
        rblk = lambda b, c: b
    else:
        rblk = lambda b, c: b * nchunks + c
    in_specs = [
        pl.BlockSpec((SSD_NG, R, SSD_GW), lambda b, c: (0, rblk(b, c), 0)),
        pl.BlockSpec((2 * SSD_NG, R, SSD_NS), lambda b, c: (0, rblk(b, c), 0)),
        pl.BlockSpec((SSD_NG, R, SSD_GW), lambda b, c: (0, rblk(b, c), 0)),
        pl.BlockSpec((R, LANES), lambda b, c: (rblk(b, c), 0)),
    ]
    args = [xs4, bc, z4, tail]
    if has_state:
        in_specs.append(pl.BlockSpec((None, sb, SSD_NG, SSD_GW, SSD_NS), lambda b, c: (l, b, 0, 0, 0)))
        args.append(ssm0)
    in_specs += [
        pl.BlockSpec((None, 1, LANES), lambda b, c: (l, 0, 0)),
        pl.BlockSpec((None, 1, LANES), lambda b, c: (l, 0, 0)),
        pl.BlockSpec((None, SSD_NG, 1, SSD_GW), lambda b, c: (l, 0, 0, 0)),
        pl.BlockSpec((None, SSD_NG, 1, SSD_GW), lambda b, c: (l, 0, 0, 0)),
        pl.BlockSpec((None, SSD_NG, SSD_GW, D), lambda b, c: (l, 0, 0, 0)),
        pl.BlockSpec((SSD_NG, 3 * LANES, SSD_HPG * LANES), lambda b, c: (0, 0, 0)),
        pl.BlockSpec((SSD_NG, 2 * LANES, SSD_GW), lambda b, c: (0, 0, 0)),
    ]
    args += [dtb, alog, dsk, nw, wbr, bigsel, e8]
    aliases = {}
    _stacked_out(ssmn_prev, in_specs, args, aliases, 1)
    return pl.pallas_call(
        functools.partial(_ssd_kernel, L=L, sb=sb, nchunks=nchunks, has_state=has_state),
        grid=(nseq // sb, nchunks),
        in_specs=in_specs,
        out_specs=(
            pl.BlockSpec((R, D), lambda b, c: (rblk(b, c), 0)),
            pl.BlockSpec((None, sb, SSD_NG, SSD_GW, SSD_NS), lambda b, c: (l, b, 0, 0, 0)),
        ),
        out_shape=(
            jax.ShapeDtypeStruct((ntok, D), f32),
            jax.ShapeDtypeStruct((DEPTH, nseq, SSD_NG, SSD_GW, SSD_NS), f32),
        ),
        input_output_aliases=aliases,
        scratch_shapes=[pltpu.VMEM((LANES, R), f32)],
        compiler_params=pltpu.CompilerParams(
            dimension_semantics=("arbitrary", "arbitrary"), vmem_limit_bytes=VMEM_LIMIT),
        name="ssd",
    )(*args)


GLA_C2 = 4


def _gla_consts(L, C1, sb):
    R = sb * L
    nb2 = C1 // GLA_C2
    i = np.arange(R)[:, None]
    s = np.arange(R)[None, :]
    same_seq = (i // L) == (s // L)
    mats = [(s <= i) & (s > i - d) & same_seq for d in range(1, GLA_C2)]
    mats.append((s <= i) & (s >= (i // GLA_C2) * GLA_C2))
    for r in range(1, nb2):
        mats.append((s > i) & (s <= (i // C1) * C1 + GLA_C2 * r - 1))
    shifts = [(s == i - d) & same_seq for d in range(1, GLA_C2)]
    return (jnp.asarray(np.concatenate(mats, 0), bf16), jnp.asarray(np.concatenate(shifts, 0), bf16))


def _gla_kernel(*refs, L, C1, sb, nchunks, has_state):
    R = sb * L
    nb1 = L // C1
    nb2 = C1 // GLA_C2
    qkvr_ref, lg_ref = refs[:2]
    k = 2
    gla0_ref = None
    if has_state:
        gla0_ref = refs[k]
        k += 1
    gnw_ref, wbr_ref, sums_ref, shift_ref = refs[k:k + 4]
    br_ref, glan_ref = refs[-2:]
    c = pl.program_id(1)

    rowi = lax.broadcasted_iota(jnp.int32, (R, R), 0)
    coli = lax.broadcasted_iota(jnp.int32, (R, R), 1)
    shl = L.bit_length() - 1
    same_seq = (rowi >> shl) == (coli >> shl)
    tri = jnp.logical_and(rowi >= coli, same_seq).astype(bf16)
    sh1 = C1.bit_length() - 1
    same_blk1 = (rowi >> sh1) == (coli >> sh1)
    sub_i = (rowi & (C1 - 1)) >> (GLA_C2.bit_length() - 1)
    off_i = rowi & (GLA_C2 - 1)
    rowl = lax.broadcasted_iota(jnp.int32, (R, 1), 0)
    sub_l = (rowl & (C1 - 1)) >> (GLA_C2.bit_length() - 1)
    blk_i = (rowi & (L - 1)) >> sh1
    rel_l = rowl & (L - 1)
    pad_rows = [jnp.zeros((LANES - R, GLA_DK), bf16)] if R < LANES else []

    def init():
        if has_state:
            glan_ref[...] = gla0_ref[...]
        else:
            glan_ref[...] = jnp.zeros(glan_ref.shape, f32)

    if nchunks == 1:
        init()
    else:
        pl.when(c == 0)(init)

    def seq_last(x):
        if sb == 1:
            return x[R - 1:R, :]
        x3 = x.reshape(sb, L, x.shape[-1])
        return jnp.broadcast_to(x3[:, L - 1:L, :], x3.shape).reshape(x.shape)

    def rows2d(x):
        return x.reshape(R, x.shape[-1]) if x.ndim == 3 else x

    lg = rows2d(lg_ref[...])
    g = _cumsum_rows(lg, tri)
    g_end = seq_last(g)
    lg_hi, lg_lo = _split2(lg)
    sums = _dot(sums_ref[...], lg_hi) + _dot(sums_ref[...], lg_lo)
    a_d = [sums[R * (d - 1):R * d] for d in range(1, GLA_C2)]
    a_sub = sums[R * (GLA_C2 - 1):R * GLA_C2]
    b_sub = [sums[R * (GLA_C2 - 1 + r):R * (GLA_C2 + r)] for r in range(1, nb2)]
    q = rows2d(qkvr_ref[..., 0:GLA_KD]).astype(f32)
    kb = rows2d(qkvr_ref[..., GLA_KD:2 * GLA_KD])
    k_ = kb.astype(f32)
    k_sh = [_dot(shift_ref[R * (d - 1):R * d, :], kb) for d in range(1, GLA_C2)]

    acc = jnp.zeros((R, D), f32)
    for h in range(GLA_NH):
        kl = slice(GLA_DK * h, GLA_DK * (h + 1))
        vl = slice(2 * GLA_KD + GLA_DV * h, 2 * GLA_KD + GLA_DV * (h + 1))
        rl = slice(2 * GLA_KD + GLA_VD + GLA_DV * h, 2 * GLA_KD + GLA_VD + GLA_DV * (h + 1))
        gh = g[:, kl]
        qh = q[:, kl]
        kh = k_[:, kl]
        vb = rows2d(qkvr_ref[..., vl])
        qg = qh * jnp.exp(gh)
        kd = kh * jnp.exp(g_end[:, kl] - gh)

        if sb == 1:
            st = glan_ref[0, h]
            o = _dot(qg.astype(bf16), st.astype(bf16))
            e_col = jnp.exp(jnp.broadcast_to(g_end[0:1, kl], (8, GLA_DK))).T[:, 0:1]
            glan_ref[0, h] = st * e_col + _dot_tn(kd.astype(bf16), vb)
        else:
            vf = vb.astype(f32)
            parts = []
            for s in range(sb):
                rs = slice(s * L, (s + 1) * L)
                st = glan_ref[s, h]
                parts.append(_dot(qg[rs].astype(bf16), st.astype(bf16)))
                e_col = jnp.exp(jnp.broadcast_to(g_end[s * L:s * L + 1, kl], (8, GLA_DK))).T[:, 0:1]
                glan_ref[s, h] = st * e_col + _dot_tn(kd[rs].astype(bf16), vf[rs].astype(bf16))
            o = jnp.concatenate(parts, axis=0)

        att = jnp.zeros((R, R), f32)
        if nb1 > 1:
            def bnd(s, i):
                if i == 0:
                    return jnp.zeros((1, GLA_DK), f32)
                return gh[s * L + C1 * i - 1:s * L + C1 * i, :]

            gblk = jnp.concatenate([jnp.broadcast_to(bnd(s, i), (C1, GLA_DK))
                                    for s in range(sb) for i in range(nb1)], axis=0)
            qt = (qh * jnp.exp(gh - gblk)).astype(bf16)
            kts = []
            for i in range(1, nb1):
                gb = jnp.concatenate([jnp.broadcast_to(bnd(s, i), (L, GLA_DK)) for s in range(sb)], axis=0)
                kt = kh * jnp.exp(jnp.where(rel_l < C1 * i, gb - gh, -jnp.inf))
                kts += [kt.astype(bf16)] + pad_rows
            out1 = _dot_nt(qt, jnp.concatenate(kts, axis=0))
            for i in range(1, nb1):
                att = att + jnp.where(jnp.logical_and(blk_i == i, same_seq),
                                      out1[:, LANES * (i - 1):LANES * (i - 1) + R], 0.0)

        qt2 = (qh * jnp.exp(a_sub[:, kl])).astype(bf16)
        kts = []
        for r in range(1, nb2):
            kt = kh * jnp.exp(jnp.where(sub_l < r, b_sub[r - 1][:, kl], -jnp.inf))
            kts += [kt.astype(bf16)] + pad_rows
        out2 = _dot_nt(qt2, jnp.concatenate(kts, axis=0))
        for r in range(1, nb2):
            att = att + jnp.where(jnp.logical_and(sub_i == r, same_blk1),
                                  out2[:, LANES * (r - 1):LANES * (r - 1) + R], 0.0)

        for d in range(GLA_C2):
            t = qh * kh if d == 0 else qh * jnp.exp(a_d[d - 1][:, kl]) * k_sh[d - 1][:, kl]
            band = jnp.sum(t, axis=-1, keepdims=True)
            hit = jnp.logical_and(off_i >= d, coli == rowi - d)
            att = att + jnp.where(hit, band, 0.0)
        o = o + _dot(att.astype(bf16), vb)

        on = o * lax.rsqrt(jnp.mean(o * o, axis=-1, keepdims=True) + EPS) * gnw_ref[...]
        og = on * rows2d(qkvr_ref[..., rl]).astype(f32)
        acc = acc + _dot(og.astype(bf16), wbr_ref[GLA_DV * h:GLA_DV * (h + 1), :])
    br_ref[...] = acc.reshape(br_ref.shape)


def _gla(l, qkvr, lg, gla0, glan_prev, w, *, nseq, L, C1, sb, nchunks):
    has_state = gla0 is not None
    gnw, wbr = w
    sums, shifts = _gla_consts(L, C1, sb)
    ntok = qkvr.shape[0]
    rb = sb * L
    wq = 2 * GLA_KD + 2 * GLA_VD
    if nchunks == 1:
        row = lambda b, c: (b, 0)
        blk = lambda w_: (rb, w_)
        view = lambda a: a
    else:
        row = lambda b, c: (b, c, 0)
        blk = lambda w_: (sb, L, w_)
        view = lambda a: a.reshape(nseq, ntok // nseq, a.shape[-1])
    in_specs = [
        pl.BlockSpec(blk(wq), row),
        pl.BlockSpec(blk(GLA_KD), row),
    ]
    args = [view(qkvr), view(lg)]
    if has_state:
        in_specs.append(pl.BlockSpec((None, sb, GLA_NH, GLA_DK, GLA_DV), lambda b, c: (l, b, 0, 0, 0)))
        args.append(gla0)
    in_specs += [
        pl.BlockSpec((None, 1, GLA_DV), lambda b, c: (l, 0, 0)),
        pl.BlockSpec((None, GLA_VD, D), lambda b, c: (l, 0, 0)),
        pl.BlockSpec(sums.shape, lambda b, c: (0, 0)),
        pl.BlockSpec(shifts.shape, lambda b, c: (0, 0)),
    ]
    args += [gnw, wbr, sums, shifts]
    aliases = {}
    _stacked_out(glan_prev, in_specs, args, aliases, 1)
    br_shape = (ntok, D) if nchunks == 1 else (nseq, ntok // nseq, D)
    br, glan = pl.pallas_call(
        functools.partial(_gla_kernel, L=L, C1=C1, sb=sb, nchunks=nchunks, has_state=has_state),
        grid=(nseq // sb, nchunks),
        in_specs=in_specs,
        out_specs=(
            pl.BlockSpec(blk(D), row),
            pl.BlockSpec((None, sb, GLA_NH, GLA_DK, GLA_DV), lambda b, c: (l, b, 0, 0, 0)),
        ),
        out_shape=(
            jax.ShapeDtypeStruct(br_shape, f32),
            jax.ShapeDtypeStruct((DEPTH, nseq, GLA_NH, GLA_DK, GLA_DV), f32),
        ),
        input_output_aliases=aliases,
        compiler_params=pltpu.CompilerParams(
            dimension_semantics=("arbitrary", "arbitrary"), vmem_limit_bytes=VMEM_LIMIT),
        name="gla",
    )(*args)
    return br.reshape(ntok, D), glan


def _top2_sum(a, b, c, d):
    hi1, lo1 = jnp.maximum(a, b), jnp.minimum(a, b)
    hi2, lo2 = jnp.maximum(c, d), jnp.minimum(c, d)
    return jnp.maximum(hi1, hi2) + jnp.maximum(jnp.minimum(hi1, hi2), jnp.maximum(lo1, lo2))


def _route_rows(sig, biased):
    gsc = [_top2_sum(*biased[EXP_PER_GRP * g:EXP_PER_GRP * (g + 1)]) for g in range(N_EGRP)]
    best = jnp.zeros_like(gsc[0], dtype=jnp.int32)
    m = gsc[0]
    for g in range(1, N_EGRP):
        better = gsc[g] > m
        best = jnp.where(better, g, best)
        m = jnp.where(better, gsc[g], m)
    masked = [jnp.where(best == (e // EXP_PER_GRP), biased[e], -jnp.inf) for e in range(N_EXP)]

    def first_argmax(vals):
        idx = jnp.zeros_like(best)
        mx = vals[0]
        for e in range(1, N_EXP):
            better = vals[e] > mx
            idx = jnp.where(better, e, idx)
            mx = jnp.where(better, vals[e], mx)
        return idx

    i1 = first_argmax(masked)
    i2 = first_argmax([jnp.where(i1 == e, -jnp.inf, masked[e]) for e in range(N_EXP)])
    w1 = sum(jnp.where(i1 == e, sig[e], 0.0) for e in range(N_EXP))
    w2 = sum(jnp.where(i2 == e, sig[e], 0.0) for e in range(N_EXP))
    den = w1 + w2
    return best, i1, i2, w1 / den, w2 / den


def _merge_kernel(*refs, seq3d, sparse):
    (x_ref, bra_ref, brb_ref, gate_ref, mod_ref, wo_ref, n2_ref, wr_hi_ref, wr_lo_ref, rb_ref) = refs[:10]
    mixed_in = (gate_ref[:, 0:D].astype(f32) * bra_ref[...]
                + gate_ref[:, D:2 * D].astype(f32) * brb_ref[...])
    mixed = _dot(mixed_in.astype(bf16), wo_ref[...])
    x = x_ref[...]
    if seq3d:
        x1 = x + mod_ref[:, 2:3, :] * mixed.reshape(x.shape)
    else:
        x1 = x + mod_ref[2:3, :] * mixed
    refs[10][...] = x1
    h2 = _norm_mod(x1, mod_ref, n2_ref, 3, 4, seq3d)
    hi, lo = _split2(h2)
    logits = _dot(hi, wr_hi_ref[...]) + (_dot(hi, wr_lo_ref[...]) + _dot(lo, wr_hi_ref[...]))
    tm = logits.shape[0]
    lt = logits.T
    sig_all = _sigmoid(lt[0:N_EXP, :])
    bias_all = sig_all + rb_ref[...]
    sig = [sig_all[e:e + 1, :] for e in range(N_EXP)]
    biased = [bias_all[e:e + 1, :] for e in range(N_EXP)]
    best, i1, i2, w1, w2 = _route_rows(sig, biased)
    if not sparse:
        h2_ref, comb_ref = refs[11:13]
        h2_ref[...] = h2.astype(bf16)
        comb = [jnp.where(i1 == e, w1, 0.0) + jnp.where(i2 == e, w2, 0.0) for e in range(N_EXP)]
        comb_t = jnp.concatenate(comb + [jnp.zeros((LANES - N_EXP, tm), f32)], axis=0)
        comb_ref[...] = comb_t.T
        return

    h2x_ref, plan_ref, counts_ref, base_scr = refs[11:15]
    swap = i1 > i2
    wa = jnp.where(swap, w2, w1)
    wb = jnp.where(swap, w1, w2)
    a = jnp.minimum(i1, i2) & (EXP_PER_GRP - 1)
    b = jnp.maximum(i1, i2) & (EXP_PER_GRP - 1)
    pair = jnp.where(a == 0, b - 1, jnp.where(a == 1, b + 1, 5))
    cls = best * N_PAIR + pair
    sub = lax.broadcasted_iota(jnp.int32, (CLS_PAD, tm), 0)
    onehot = (sub == cls).astype(f32)
    before = (lax.broadcasted_iota(jnp.int32, (tm, tm), 0)
              < lax.broadcasted_iota(jnp.int32, (tm, tm), 1)).astype(bf16)
    prefix = _dot(onehot.astype(bf16), before)

    @pl.when(pl.program_id(0) == 0)
    def _():
        base_scr[...] = jnp.zeros_like(base_scr)

    base = base_scr[...]
    rank = jnp.sum(onehot * (prefix + base[:, 0:1]), axis=0, keepdims=True)
    base = base + jnp.sum(onehot, axis=1, keepdims=True)
    base_scr[...] = base
    counts_ref[...] = base
    plan_ref[...] = jnp.concatenate(
        [cls, rank.astype(jnp.int32), jnp.zeros((6, tm), jnp.int32)], axis=0)
    ext_t = jnp.concatenate([wa, wb, jnp.zeros((MOE_EXT - 2, tm), f32)], axis=0)
    h2x_ref[:, 0:D] = h2
    h2x_ref[:, D:D + MOE_EXT] = ext_t.T


def _merge(l, x, bra, brb, gate, mod, wo, n2w, wr_hi, wr_lo, rb, *, seq3d, sparse):
    if seq3d:
        ns = _seq_tile(x.shape[0], 32)
        ntok = x.shape[0] * x.shape[1]
        tm = ns * x.shape[1]
        nt = x.shape[0] // ns
        x_spec = pl.BlockSpec((ns, x.shape[1], D), lambda i: (i, 0, 0))
        mod_spec = pl.BlockSpec((None, ns, 6, D), lambda i: (l, i, 0, 0))
    else:
        ntok = x.shape[0]
        nseq = mod.shape[1]
        tm = min(512, ntok // nseq)
        nt = ntok // tm
        per_seq = ntok // nseq // tm
        x_spec = pl.BlockSpec((tm, D), lambda i: (i, 0))
        mod_spec = pl.BlockSpec((None, None, 6, D), lambda i: (l, i // per_seq, 0, 0))
    row = lambda i: (i, 0)
    const = lambda i: (0, 0)
    if sparse:
        out_specs = (x_spec, pl.BlockSpec((tm, D + MOE_EXT), row), pl.BlockSpec((8, tm), lambda i: (0, i)),
                     pl.BlockSpec((CLS_PAD, LANES), const))
        out_shape = (jax.ShapeDtypeStruct(x.shape, f32), jax.ShapeDtypeStruct((ntok, D + MOE_EXT), f32),
                     jax.ShapeDtypeStruct((8, ntok), jnp.int32), jax.ShapeDtypeStruct((CLS_PAD, LANES), f32))
        scratch = [pltpu.VMEM((CLS_PAD, LANES), f32)]
    else:
        out_specs = (x_spec, pl.BlockSpec((tm, D), row), pl.BlockSpec((tm, LANES), row))
        out_shape = (jax.ShapeDtypeStruct(x.shape, f32), jax.ShapeDtypeStruct((ntok, D), bf16),
                     jax.ShapeDtypeStruct((ntok, LANES), f32))
        scratch = []
    return pl.pallas_call(
        functools.partial(_merge_kernel, seq3d=seq3d, sparse=sparse),
        grid=(nt,),
        in_specs=[
            x_spec,
            pl.BlockSpec((tm, D), row),
            pl.BlockSpec((tm, D), row),
            pl.BlockSpec((tm, 2 * D), row),
            mod_spec,
            pl.BlockSpec((None, D, D), lambda i: (l, 0, 0)),
            pl.BlockSpec((None, 1, D), lambda i: (l, 0, 0)),
            pl.BlockSpec((D, LANES), const),
            pl.BlockSpec((D, LANES), const),
            pl.BlockSpec((N_EXP, 1), const),
        ],
        out_specs=out_specs,
        out_shape=out_shape,
        scratch_shapes=scratch,
        compiler_params=pltpu.CompilerParams(
            dimension_semantics=("arbitrary",), vmem_limit_bytes=VMEM_LIMIT),
        name="merge",
    )(x, bra, brb, gate, mod, wo, n2w, wr_hi, wr_lo, rb)


def _dispatch_kernel(pos_ref, src_ref, buf_in, xs_hbm, sem):
    del buf_in
    tm = src_ref.shape[0]

    def issue(t, c):
        p = pos_ref[0, t]
        pltpu.make_async_copy(src_ref.at[pl.ds(t, 1), :], xs_hbm.at[pl.ds(p, 1), :], sem).start()
        return c

    lax.fori_loop(0, tm, issue, 0, unroll=8)

    def drain(t, c):
        pltpu.make_async_copy(src_ref.at[pl.ds(0, 1), :], xs_hbm.at[pl.ds(0, 1), :], sem).wait()
        return c

    lax.fori_loop(0, tm, drain, 0, unroll=8)


def _dispatch(pos3, h2x, xs_buf):
    nt, _, tm = pos3.shape
    return pl.pallas_call(
        _dispatch_kernel,
        grid=(nt,),
        in_specs=[
            pl.BlockSpec((None, 1, tm), lambda i: (i, 0, 0), memory_space=pltpu.SMEM),
            pl.BlockSpec((tm, D + MOE_EXT), lambda i: (i, 0)),
            pl.BlockSpec(memory_space=pl.ANY),
        ],
        out_specs=pl.BlockSpec(memory_space=pl.ANY),
        out_shape=jax.ShapeDtypeStruct(xs_buf.shape, f32),
        input_output_aliases={2: 0},
        scratch_shapes=[pltpu.SemaphoreType.DMA(())],
        compiler_params=pltpu.CompilerParams(dimension_semantics=("arbitrary",)),
        name="dispatch",
    )(pos3, h2x, xs_buf)


def _moe_sparse_kernel(src_ref, valid_ref, ea_ref, eb_ref, fresh_ref, xs_ref,
                       wga_ref, wua_ref, wda_ref, wgb_ref, wub_ref, wdb_ref, ys_ref, wup_scr, wdn_scr):
    del src_ref, ea_ref, eb_ref
    t = pl.program_id(0)

    @pl.when(fresh_ref[t] == 1)
    def _():
        for n, w in enumerate((wga_ref, wua_ref, wgb_ref, wub_ref)):
            wup_scr[n] = w[...].astype(bf16)
        for n, w in enumerate((wda_ref, wdb_ref)):
            wdn_scr[n] = w[...].astype(bf16)

    @pl.when(valid_ref[t] == 1)
    def _():
        x = xs_ref[...]
        h = x[:, 0:D].astype(bf16)

        def ffn(n):
            u = (_silu(_dot(h, wup_scr[2 * n])) * _dot(h, wup_scr[2 * n + 1])).astype(bf16)
            return _dot(u, wdn_scr[n])

        ys_ref[...] = x[:, D:D + 1] * ffn(0) + x[:, D + 1:D + 2] * ffn(1)

    @pl.when(valid_ref[pl.program_id(0)] == 0)
    def _():
        ys_ref[...] = jnp.zeros(ys_ref.shape, f32)


def _moe_sparse(l, tiles, xs, wg, wu, wd):
    src, valid, ea, eb, fresh = tiles
    nrow = xs.shape[0]
    ntile = nrow // MOE_R
    wa_map = lambda t, src, valid, ea, eb, fresh: (l, ea[t], 0, 0)
    wb_map = lambda t, src, valid, ea, eb, fresh: (l, eb[t], 0, 0)
    row_map = lambda t, src, valid, ea, eb, fresh: (src[t], 0)
    return pl.pallas_call(
        _moe_sparse_kernel,
        grid_spec=pltpu.PrefetchScalarGridSpec(
            num_scalar_prefetch=5,
            grid=(ntile,),
            in_specs=[
                pl.BlockSpec((MOE_R, D + MOE_EXT), row_map),
                pl.BlockSpec((None, None, D, EXP_FF), wa_map),
                pl.BlockSpec((None, None, D, EXP_FF), wa_map),
                pl.BlockSpec((None, None, EXP_FF, D), wa_map),
                pl.BlockSpec((None, None, D, EXP_FF), wb_map),
                pl.BlockSpec((None, None, D, EXP_FF), wb_map),
                pl.BlockSpec((None, None, EXP_FF, D), wb_map),
            ],
            out_specs=pl.BlockSpec((MOE_R, D), lambda t, src, valid, ea, eb, fresh: (t, 0)),
            scratch_shapes=[pltpu.VMEM((4, D, EXP_FF), bf16), pltpu.VMEM((2, EXP_FF, D), bf16)],
        ),
        out_shape=jax.ShapeDtypeStruct((nrow, D), f32),
        compiler_params=pltpu.CompilerParams(
            dimension_semantics=("arbitrary",), vmem_limit_bytes=VMEM_LIMIT),
        name="moe_sparse",
    )(src, valid, ea, eb, fresh, xs, wg, wu, wd, wg, wu, wd)


def _combine_kernel(pos_ref, ys_hbm, x1_ref, mod_ref, x2_ref, buf, sem):
    tm = x1_ref.shape[0]

    def issue(t, c):
        p = pos_ref[0, t]
        pltpu.make_async_copy(ys_hbm.at[pl.ds(p, 1), :], buf.at[pl.ds(t, 1), :], sem).start()
        return c

    lax.fori_loop(0, tm, issue, 0, unroll=8)

    def drain(t, c):
        pltpu.make_async_copy(ys_hbm.at[pl.ds(0, 1), :], buf.at[pl.ds(0, 1), :], sem).wait()
        return c

    lax.fori_loop(0, tm, drain, 0, unroll=8)
    x2_ref[...] = x1_ref[...] + mod_ref[5:6, :] * buf[...]


def _combine(l, pos3, ys, x1, mod):
    nt, _, tm = pos3.shape
    ntok = x1.shape[0]
    per_seq = ntok // mod.shape[1] // tm
    return pl.pallas_call(
        _combine_kernel,
        grid=(nt,),
        in_specs=[
            pl.BlockSpec((None, 1, tm), lambda i: (i, 0, 0), memory_space=pltpu.SMEM),
            pl.BlockSpec(memory_space=pl.ANY),
            pl.BlockSpec((tm, D), lambda i: (i, 0)),
            pl.BlockSpec((None, None, 6, D), lambda i: (l, i // per_seq, 0, 0)),
        ],
        out_specs=pl.BlockSpec((tm, D), lambda i: (i, 0)),
        out_shape=jax.ShapeDtypeStruct((ntok, D), f32),
        scratch_shapes=[pltpu.VMEM((tm, D), f32), pltpu.SemaphoreType.DMA(())],
        compiler_params=pltpu.CompilerParams(dimension_semantics=("arbitrary",)),
        name="combine",
    )(pos3, ys, x1, mod)


def _moe_plan(plan, counts, ntile):
    cnt = counts[:N_CLS, 0].astype(jnp.int32)
    tiles_per = (cnt + (MOE_R - 1)) // MOE_R
    tstart = jnp.cumsum(tiles_per) - tiles_per
    total = jnp.sum(tiles_per)
    pos = tstart[plan[0]] * MOE_R + plan[1]
    t = jnp.arange(ntile, dtype=jnp.int32)
    src = jnp.minimum(t, total - 1)
    tcls = jnp.sum((src[:, None] >= tstart[None, :]).astype(jnp.int32), axis=1) - 1
    grp = tcls // N_PAIR
    pair = tcls % N_PAIR
    pa = jnp.asarray(np.array([0, 0, 0, 1, 1, 2], np.int32))
    pb = jnp.asarray(np.array([1, 2, 3, 2, 3, 3], np.int32))
    ea = grp * EXP_PER_GRP + pa[pair]
    eb = grp * EXP_PER_GRP + pb[pair]
    valid = (t < total).astype(jnp.int32)
    prev_cls = jnp.concatenate([jnp.full((1,), -1, jnp.int32), tcls[:-1]])
    fresh = jnp.logical_and(tcls != prev_cls, valid == 1).astype(jnp.int32)
    return pos, (src, valid, ea, eb, fresh)


def _moe_kernel(h2_ref, comb_ref, wg_ref, wu_ref, wd_ref, x1_ref, mod_ref, x2_ref, acc_scr, *, seq3d):
    e = pl.program_id(1)

    @pl.when(e == 0)
    def _():
        acc_scr[...] = jnp.zeros_like(acc_scr)

    h = h2_ref[...]
    a = _dot(h, wg_ref[...].astype(bf16))
    b = _dot(h, wu_ref[...].astype(bf16))
    t = (_silu(a) * b).astype(bf16)
    ye = _dot(t, wd_ref[...].astype(bf16))
    lane = lax.broadcasted_iota(jnp.int32, comb_ref.shape, 1)
    w = jnp.sum(jnp.where(lane == e, comb_ref[...], 0.0), axis=-1, keepdims=True)
    acc_scr[...] += w * ye

    @pl.when(e == N_EXP - 1)
    def _():
        x1 = x1_ref[...]
        if seq3d:
            x2_ref[...] = x1 + mod_ref[:, 5:6, :] * acc_scr[...].reshape(x1.shape)
        else:
            x2_ref[...] = x1 + mod_ref[5:6, :] * acc_scr[...]


def _moe(l, h2, comb, wg, wu, wd, x1, mod, *, seq3d):
    if seq3d:
        ns = x1.shape[0]
        ntok = ns * x1.shape[1]
        tm = ntok
        nt = 1
        x_spec = pl.BlockSpec((ns, x1.shape[1], D), lambda i, e: (0, 0, 0))
        mod_spec = pl.BlockSpec((None, ns, 6, D), lambda i, e: (l, 0, 0, 0))
    else:
        ntok = x1.shape[0]
        nseq = mod.shape[1]
        tm = min(1024, ntok // nseq)
        nt = ntok // tm
        per_seq = ntok // nseq // tm
        x_spec = pl.BlockSpec((tm, D), lambda i, e: (i, 0))
        mod_spec = pl.BlockSpec((None, None, 6, D), lambda i, e: (l, i // per_seq, 0, 0))
    row = lambda i, e: (i, 0)
    return pl.pallas_call(
        functools.partial(_moe_kernel, seq3d=seq3d),
        grid=(nt, N_EXP),
        in_specs=[
            pl.BlockSpec((tm, D), row),
            pl.BlockSpec((tm, LANES), row),
            pl.BlockSpec((None, None, D, EXP_FF), lambda i, e: (l, e, 0, 0)),
            pl.BlockSpec((None, None, D, EXP_FF), lambda i, e: (l, e, 0, 0)),
            pl.BlockSpec((None, None, EXP_FF, D), lambda i, e: (l, e, 0, 0)),
            x_spec,
            mod_spec,
        ],
        out_specs=x_spec,
        out_shape=jax.ShapeDtypeStruct(x1.shape, f32),
        scratch_shapes=[pltpu.VMEM((tm, D), f32)],
        compiler_params=pltpu.CompilerParams(
            dimension_semantics=("arbitrary", "arbitrary"), vmem_limit_bytes=VMEM_LIMIT),
        name="moe",
    )(h2, comb, wg, wu, wd, x1, mod)


def _final_norm_kernel(x_ref, w_ref, o_ref):
    x = x_ref[...]
    o_ref[...] = x * lax.rsqrt(jnp.mean(x * x, axis=-1, keepdims=True) + EPS) * w_ref[...]


def _final_norm(x2d, w):
    ntok = x2d.shape[0]
    tm = min(1024, ntok)
    return pl.pallas_call(
        _final_norm_kernel,
        grid=(ntok // tm,),
        in_specs=[pl.BlockSpec((tm, D), lambda i: (i, 0)), pl.BlockSpec((1, D), lambda i: (0, 0))],
        out_specs=pl.BlockSpec((tm, D), lambda i: (i, 0)),
        out_shape=jax.ShapeDtypeStruct((ntok, D), f32),
        compiler_params=pltpu.CompilerParams(dimension_semantics=("arbitrary",)),
        name="final_norm",
    )(x2d, w)


def _prep_in_weights(w_in):
    o = 0
    z = w_in[:, :, o:o + SSD_INNER]; o += SSD_INNER
    xbc = w_in[:, :, o:o + CONV_DIM]; o += CONV_DIM
    dt = w_in[:, :, o:o + SSD_NH]; o += SSD_NH
    q = w_in[:, :, o:o + GLA_KD] * (GLA_DK ** -0.5); o += GLA_KD
    kvr = w_in[:, :, o:o + GLA_KD + 2 * GLA_VD]; o += GLA_KD + 2 * GLA_VD
    glr = w_in[:, :, o:o + GLA_RANK]; o += GLA_RANK
    gates = w_in[:, :, o:o + 2 * D]
    main = jnp.concatenate([z, xbc, q, kvr, gates], axis=-1).astype(bf16)
    pad = jnp.zeros(w_in.shape[:2] + (LANES - SSD_NH - GLA_RANK,), w_in.dtype)
    tail = jnp.concatenate([dt, glr, pad], axis=-1).astype(bf16)
    return main, tail


def _pad_lanes(v, width=LANES):
    return jnp.concatenate([v, jnp.zeros(v.shape[:-1] + (width - v.shape[-1],), v.dtype)], axis=-1)


def kernel(x_prompt, x_sample, c_prompt, c_sample, state_conv, state_ssm, state_gla, w_ada, b_ada, norm1_w, w_in, conv_w, conv_b, dt_bias, a_log, d_skip, ssd_norm_w, w_ssd_br, gla_gate_up, gla_gate_b, gla_norm_w, w_gla_br, merge_b, w_out, norm2_w, w_router, router_bias, w_exp_gate, w_exp_up, w_exp_down, final_norm_w):
    bp, sp, _ = x_prompt.shape
    bs, ss, _ = x_sample.shape

    mod = _adaln(jnp.concatenate([c_prompt, c_sample], axis=0), w_ada, b_ada)
    mod_p = mod[:, :bp].reshape(DEPTH, bp, 6, D)
    mod_s = mod[:, bp:].reshape(DEPTH, bs, 6, D)

    w_main, w_tail = _prep_in_weights(w_in)
    gate_up = jnp.concatenate(
        [jnp.zeros((DEPTH, TAIL_GLR0, GLA_KD), f32), gla_gate_up,
         jnp.zeros((DEPTH, LANES - TAIL_GLR0 - GLA_RANK, GLA_KD), f32)], axis=1).astype(bf16)
    in_w = (norm1_w.reshape(DEPTH, 1, D), w_main, w_tail, gate_up, gla_gate_b.reshape(DEPTH, 1, GLA_KD),
            conv_w, conv_b.reshape(DEPTH, 1, CONV_DIM), merge_b.reshape(DEPTH, 1, 2 * D))
    bigsel, e8 = _ssd_consts()
    ssd_w = (_pad_lanes(dt_bias).reshape(DEPTH, 1, LANES), _pad_lanes(a_log).reshape(DEPTH, 1, LANES),
             jnp.repeat(d_skip, SSD_HD, axis=-1).reshape(DEPTH, SSD_NG, 1, SSD_GW),
             ssd_norm_w.reshape(DEPTH, SSD_NG, 1, SSD_GW),
             w_ssd_br.astype(bf16).reshape(DEPTH, SSD_NG, SSD_GW, D), bigsel, e8)
    gla_w = (gla_norm_w.reshape(DEPTH, 1, GLA_DV), w_gla_br.astype(bf16))
    w_out_b = w_out.astype(bf16)
    n2 = norm2_w.reshape(DEPTH, 1, D)
    wg_b, wu_b, wd_b = w_exp_gate, w_exp_up, w_exp_down
    wr = _pad_lanes(w_router)
    wr_hi = wr.astype(bf16)
    wr_lo = (wr - wr_hi.astype(f32)).astype(bf16)
    rb = router_bias.reshape(N_EXP, 1)

    xp = x_prompt.reshape(bp * sp, D)
    xs = x_sample
    ssm_s = state_ssm.reshape(DEPTH, bs, SSD_NG, SSD_GW, SSD_NS)
    ssd_lp = min(128, sp)
    gla_lp = min(64, sp)
    cp = sp_ = gp = cs_ = ss_ = gs = None
    moe_tm = min(MOE_TM, sp)
    moe_tiles = -(-(bp * sp) // MOE_R) + N_CLS
    xs_rows = jnp.zeros((moe_tiles * MOE_R, D + MOE_EXT), f32)
    for l in range(DEPTH):
        z4, xs4, bc, qkvr, gate, tail, lg, cp = _inproj(l, xp, mod_p, None, cp, in_w, seq3d=False)
        bra, sp_ = _ssd(l, xs4, bc, z4, tail, None, sp_, ssd_w, nseq=bp, L=ssd_lp, sb=1, nchunks=sp // ssd_lp)
        brb, gp = _gla(l, qkvr, lg, None, gp, gla_w, nseq=bp, L=gla_lp, C1=16,
                       sb=2 if bp % 2 == 0 else 1, nchunks=sp // gla_lp)
        x1, h2x, plan, counts = _merge(l, xp, bra, brb, gate, mod_p, w_out_b, n2, wr_hi, wr_lo, rb,
                                       seq3d=False, sparse=True)
        pos, tiles = _moe_plan(plan, counts, moe_tiles)
        pos3 = pos.reshape(bp * sp // moe_tm, 1, moe_tm)
        xs_rows = _dispatch(pos3, h2x, xs_rows)
        ys_rows = _moe_sparse(l, tiles, xs_rows, wg_b, wu_b, wd_b)
        xp = _combine(l, pos3, ys_rows, x1, mod_p)

        z4, xs4, bc, qkvr, gate, tail, lg, cs_ = _inproj(l, xs, mod_s, state_conv, cs_, in_w, seq3d=True)
        bra, ss_ = _ssd(l, xs4, bc, z4, tail, ssm_s, ss_, ssd_w, nseq=bs, L=ss, sb=8, nchunks=1)
        brb, gs = _gla(l, qkvr, lg, state_gla, gs, gla_w, nseq=bs, L=ss, C1=ss, sb=8, nchunks=1)
        x1, h2, comb = _merge(l, xs, bra, brb, gate, mod_s, w_out_b, n2, wr_hi, wr_lo, rb,
                              seq3d=True, sparse=False)
        xs = _moe(l, h2, comb, wg_b, wu_b, wd_b, x1, mod_s, seq3d=True)

    fw = final_norm_w.reshape(1, D)
    y_prompt = _final_norm(xp, fw).reshape(bp, sp, D)
    y_sample = _final_norm(xs.reshape(bs * ss, D), fw).reshape(bs, ss, D)
    return (y_prompt, y_sample, cp, sp_.reshape(DEPTH, bp, SSD_NH, SSD_HD, SSD_NS), gp,
            cs_, ss_.reshape(DEPTH, bs, SSD_NH, SSD_HD, SSD_NS), gs)
```

```python
import functools

import numpy as np
import jax
import jax.numpy as jnp
from jax import lax
from jax.experimental import pallas as pl
from jax.experimental.pallas import tpu as pltpu

f32 = jnp.float32
bf16 = jnp.bfloat16

D = 1024
DEPTH = 4
SSD_INNER = 2048
SSD_HD = 64
SSD_NH = 32
SSD_NS = 128
SSD_NG = 4
SSD_HPG = 8
SSD_GW = SSD_INNER // SSD_NG
CONV_K = 4
CONV_DIM = SSD_INNER + 2 * SSD_NG * SSD_NS
GLA_NH = 4
GLA_DK = 128
GLA_DV = 256
GLA_KD = GLA_NH * GLA_DK
GLA_VD = GLA_NH * GLA_DV
GLA_RANK = 16
GLA_TAU = 16.0
N_EXP = 16
N_EGRP = 4
EXP_PER_GRP = 4
EXP_FF = 512
N_PAIR = 6
N_CLS = N_EGRP * N_PAIR
CLS_PAD = 32
MOE_EXT = 128
MOE_R = 256
MOE_TM = 512
EPS = 1e-6
LANES = 128
TAIL_GLR0 = SSD_NH
MAIN_W = SSD_INNER + CONV_DIM + 2 * GLA_KD + 2 * GLA_VD + 2 * D
VMEM_LIMIT = 56 * 1024 * 1024


def _sigmoid(x):
    return 1.0 / (1.0 + jnp.exp(-x))


def _silu(x):
    return x * _sigmoid(x)


def _softplus(x):
    return jnp.maximum(x, 0.0) + jnp.log1p(jnp.exp(-jnp.abs(x)))


def _log_sigmoid(x):
    return jnp.minimum(x, 0.0) - jnp.log1p(jnp.exp(-jnp.abs(x)))


def _dot(a, b):
    return jnp.dot(a, b, preferred_element_type=f32)


def _dot_nt(a, b):
    return lax.dot_general(a, b, (((1,), (1,)), ((), ())), preferred_element_type=f32)


def _dot_tn(a, b):
    return lax.dot_general(a, b, (((0,), (0,)), ((), ())), preferred_element_type=f32)


def _split2(x):
    hi = x.astype(bf16)
    return hi, (x - hi.astype(f32)).astype(bf16)


def _split3(x):
    hi = x.astype(bf16)
    r1 = x - hi.astype(f32)
    mid = r1.astype(bf16)
    lo = (r1 - mid.astype(f32)).astype(bf16)
    return hi, mid, lo


def _cumsum_rows(x, tri):
    hi, mid, lo = _split3(x)
    return _dot(tri, hi) + _dot(tri, mid) + _dot(tri, lo)


def _tri(n):
    r = lax.broadcasted_iota(jnp.int32, (n, n), 0)
    c = lax.broadcasted_iota(jnp.int32, (n, n), 1)
    return r >= c


def _norm_mod(x, mod_ref, w_ref, i_shift, i_scale, seq3d):
    ms = jnp.mean(x * x, axis=-1, keepdims=True)
    y = x * lax.rsqrt(ms + EPS) * w_ref[...]
    if seq3d:
        sc = mod_ref[:, i_scale:i_scale + 1, :]
        sh = mod_ref[:, i_shift:i_shift + 1, :]
        h = y * (1.0 + sc) + sh
        return h.reshape(h.shape[0] * h.shape[1], h.shape[2])
    sc = mod_ref[i_scale:i_scale + 1, :]
    sh = mod_ref[i_shift:i_shift + 1, :]
    return y * (1.0 + sc) + sh


def _seq_tile(nseq, want):
    return want if nseq % want == 0 else nseq


def _stacked_out(prev, in_specs, args, aliases, out_index):
    if prev is not None:
        in_specs.append(pl.BlockSpec(memory_space=pl.ANY))
        args.append(prev)
        aliases[len(args) - 1] = out_index


def _adaln_kernel(c_ref, w_ref, b_ref, o_ref):
    s = _silu(c_ref[...]).astype(bf16)
    o_ref[...] = _dot(s, w_ref[...].astype(bf16)) + b_ref[...]


def _adaln(c_all, w_ada, b_ada):
    n = c_all.shape[0]
    tn = 1024
    return pl.pallas_call(
        _adaln_kernel,
        grid=(DEPTH, 6 * D // tn),
        in_specs=[
            pl.BlockSpec((n, D), lambda l, j: (0, 0)),
            pl.BlockSpec((None, D, tn), lambda l, j: (l, 0, j)),
            pl.BlockSpec((None, 1, tn), lambda l, j: (l, 0, j)),
        ],
        out_specs=pl.BlockSpec((None, n, tn), lambda l, j: (l, 0, j)),
        out_shape=jax.ShapeDtypeStruct((DEPTH, n, 6 * D), f32),
        compiler_params=pltpu.CompilerParams(
            dimension_semantics=("arbitrary", "arbitrary"), vmem_limit_bytes=VMEM_LIMIT),
        name="adaln",
    )(c_all, w_ada, b_ada.reshape(DEPTH, 1, 6 * D))


IN_TN = 1024
IN_NJ = MAIN_W // IN_TN
IN_SUB = 256
IN_NSUB = IN_TN // IN_SUB
CONV_PAD = 8
KPREV = CONV_K - 1


def _inproj_kernel(*refs, seq3d, has_state, per_seq, T):
    if has_state:
        (x_ref, mod_ref, n1_ref, wm_ref, wt_ref, wg_ref, gb_ref, cw_ref, cb_ref, mb_ref, conv0_ref) = refs[:11]
        rest = refs[11:]
    else:
        (x_ref, mod_ref, n1_ref, wm_ref, wt_ref, wg_ref, gb_ref, cw_ref, cb_ref, mb_ref) = refs[:10]
        conv0_ref = None
        rest = refs[10:]
    (z4_ref, xs4_ref, bc_ref, qkvr_ref, gate_ref, tail_ref, lg_ref, convn_ref,
     h_scr, cscr, carry) = rest[-11:]
    i = pl.program_id(0)
    j = pl.program_id(1)
    tm = h_scr.shape[0]
    ns = cscr.shape[1]

    @pl.when(j == 0)
    def _():
        h = _norm_mod(x_ref[...], mod_ref, n1_ref, 0, 1, seq3d).astype(bf16)
        h_scr[...] = h
        t = _dot(h, wt_ref[...])
        tail_ref[...] = t
        pre = _dot(t.astype(bf16), wg_ref[...]) + gb_ref[...]
        lg_ref[...] = _log_sigmoid(pre) * (1.0 / GLA_TAU)

    def sub_dot(c):
        return _dot(h_scr[...], wm_ref[:, IN_SUB * c:IN_SUB * (c + 1)])

    def conv_silu(a, cj, c):
        cols = slice(IN_SUB * c, IN_SUB * (c + 1))
        gcols = slice(IN_TN * cj + IN_SUB * c, IN_TN * cj + IN_SUB * (c + 1))
        if not has_state:
            prev = jnp.where(i % per_seq == 0, 0.0, carry[cj, c, CONV_PAD - KPREV:CONV_PAD, :])
            cscr[c, 0, CONV_PAD - KPREV:CONV_PAD, :] = prev
            cscr[c, 0, CONV_PAD:2 * CONV_PAD, :] = a[0:CONV_PAD]
            cscr[c, 0, 2 * CONV_PAD:3 * CONV_PAD, :] = a[tm - CONV_PAD:tm]
            w = [cw_ref[k:k + 1, cols] for k in range(CONV_K)]
            out = cb_ref[:, cols] + a * w[KPREV]
            head = cb_ref[:, cols] + a[0:CONV_PAD] * w[KPREV]
            for k in range(KPREV):
                out = out + pltpu.roll(a, KPREV - k, 0) * w[k]
                head = head + cscr[c, 0, CONV_PAD - KPREV + k:2 * CONV_PAD - KPREV + k, :] * w[k]
            out = jnp.concatenate([head, out[CONV_PAD:]], axis=0)
            new_tail = cscr[c, 0, 3 * CONV_PAD - KPREV:3 * CONV_PAD, :]
            convn_ref[0, :, gcols] = new_tail
            carry[cj, c, CONV_PAD - KPREV:CONV_PAD, :] = new_tail
            return _silu(out).astype(bf16)
        a3 = a.reshape(ns, T, IN_SUB)
        prev = conv0_ref[:, :, cols]
        cscr[c, :, CONV_PAD - KPREV:CONV_PAD, :] = prev
        cscr[c, :, CONV_PAD:CONV_PAD + T, :] = a3
        out = cb_ref[:, cols] + a3 * cw_ref[KPREV:KPREV + 1, cols]
        for k in range(KPREV):
            out = out + cscr[c, :, CONV_PAD - KPREV + k:CONV_PAD - KPREV + k + T, :] * cw_ref[k:k + 1, cols]
        new_tail = cscr[c, :, CONV_PAD + T - KPREV:CONV_PAD + T, :]
        convn_ref[:, :, gcols] = new_tail
        return _silu(out).reshape(tm, IN_SUB).astype(bf16)

    per_grp = SSD_GW // IN_SUB

    @pl.when(j < 2)
    def _():
        for c in range(IN_NSUB):
            lanes = slice(IN_SUB * (c % per_grp), IN_SUB * (c % per_grp + 1))
            z4_ref[c // per_grp, :, lanes] = _silu(sub_dot(c)).astype(bf16)

    for cj in range(2):
        @pl.when(j == 2 + cj)
        def _(cj=cj):
            for c in range(IN_NSUB):
                lanes = slice(IN_SUB * (c % per_grp), IN_SUB * (c % per_grp + 1))
                xs4_ref[c // per_grp, :, lanes] = conv_silu(sub_dot(c), cj, c)

    @pl.when(j == 4)
    def _():
        for c in range(IN_NSUB):
            xc = conv_silu(sub_dot(c), 2, c)
            for q in range(IN_SUB // SSD_NS):
                bc_ref[c * (IN_SUB // SSD_NS) + q] = xc[:, SSD_NS * q:SSD_NS * (q + 1)]

    @pl.when(jnp.logical_and(j >= 5, j < 7))
    def _():
        for c in range(IN_NSUB):
            qkvr_ref[:, IN_SUB * c:IN_SUB * (c + 1)] = sub_dot(c).astype(bf16)

    @pl.when(j == 7)
    def _():
        for c in range(IN_NSUB):
            qkvr_ref[:, IN_SUB * c:IN_SUB * (c + 1)] = _silu(sub_dot(c)).astype(bf16)

    @pl.when(j >= 8)
    def _():
        for c in range(IN_NSUB):
            cols = slice(IN_SUB * c, IN_SUB * (c + 1))
            gate_ref[:, cols] = _sigmoid(sub_dot(c) + mb_ref[:, cols]).astype(bf16)


def _inproj(l, x, mod, conv0, convn_prev, w, *, seq3d):
    (n1w, w_main, w_tail, gate_up, gate_b, conv_w, conv_b, merge_b) = w
    has_state = conv0 is not None
    if seq3d:
        nseq, T = x.shape[0], x.shape[1]
        ns = _seq_tile(nseq, 128)
        ntok = nseq * T
        tm = ns * T
        nt = nseq // ns
        per_seq = 1
        x_spec = pl.BlockSpec((ns, T, D), lambda i, j: (i, 0, 0))
        mod_spec = pl.BlockSpec((None, ns, 6, D), lambda i, j: (l, i, 0, 0))
        seq_blk = lambda i: i
    else:
        nseq = mod.shape[1]
        ntok = x.shape[0]
        tm = min(1024, ntok // nseq)
        T = tm
        ns = 1
        nt = ntok // tm
        per_seq = ntok // nseq // tm
        x_spec = pl.BlockSpec((tm, D), lambda i, j: (i, 0))
        mod_spec = pl.BlockSpec((None, None, 6, D), lambda i, j: (l, i // per_seq, 0, 0))
        seq_blk = lambda i: i // per_seq

    def cj(j):
        return jnp.clip(j - 2, 0, 2)

    in_specs = [
        x_spec,
        mod_spec,
        pl.BlockSpec((None, 1, D), lambda i, j: (l, 0, 0)),
        pl.BlockSpec((None, D, IN_TN), lambda i, j: (l, 0, j)),
        pl.BlockSpec((None, D, LANES), lambda i, j: (l, 0, 0)),
        pl.BlockSpec((None, LANES, GLA_KD), lambda i, j: (l, 0, 0)),
        pl.BlockSpec((None, 1, GLA_KD), lambda i, j: (l, 0, 0)),
        pl.BlockSpec((None, CONV_K, IN_TN), lambda i, j: (l, 0, cj(j))),
        pl.BlockSpec((None, 1, IN_TN), lambda i, j: (l, 0, cj(j))),
        pl.BlockSpec((None, 1, IN_TN), lambda i, j: (l, 0, jnp.clip(j - 8, 0, 1))),
    ]
    args = [x, mod, n1w, w_main, w_tail, gate_up, gate_b, conv_w, conv_b, merge_b]
    if has_state:
        in_specs.append(pl.BlockSpec((None, ns, KPREV, IN_TN), lambda i, j: (l, i, 0, cj(j))))
        args.append(conv0)
    aliases = {}
    _stacked_out(convn_prev, in_specs, args, aliases, 7)
    out_shapes = (
        jax.ShapeDtypeStruct((SSD_NG, ntok, SSD_GW), bf16),
        jax.ShapeDtypeStruct((SSD_NG, ntok, SSD_GW), bf16),
        jax.ShapeDtypeStruct((2 * SSD_NG, ntok, SSD_NS), bf16),
        jax.ShapeDtypeStruct((ntok, 2 * GLA_KD + 2 * GLA_VD), bf16),
        jax.ShapeDtypeStruct((ntok, 2 * D), bf16),
        jax.ShapeDtypeStruct((ntok, LANES), f32),
        jax.ShapeDtypeStruct((ntok, GLA_KD), f32),
        jax.ShapeDtypeStruct((DEPTH, nseq, KPREV, CONV_DIM), f32),
    )
    out_specs = (
        pl.BlockSpec((2, tm, SSD_GW), lambda i, j: (jnp.clip(j, 0, 1), i, 0)),
        pl.BlockSpec((2, tm, SSD_GW), lambda i, j: (jnp.clip(j - 2, 0, 1), i, 0)),
        pl.BlockSpec((2 * SSD_NG, tm, SSD_NS), lambda i, j: (0, i, 0)),
        pl.BlockSpec((tm, IN_TN), lambda i, j: (i, jnp.clip(j - 5, 0, 2))),
        pl.BlockSpec((tm, IN_TN), lambda i, j: (i, jnp.clip(j - 8, 0, 1))),
        pl.BlockSpec((tm, LANES), lambda i, j: (i, 0)),
        pl.BlockSpec((tm, GLA_KD), lambda i, j: (i, 0)),
        pl.BlockSpec((None, ns, KPREV, CONV_DIM), lambda i, j: (l, seq_blk(i), 0, 0)),
    )
    return pl.pallas_call(
        functools.partial(_inproj_kernel, seq3d=seq3d, has_state=has_state, per_seq=per_seq, T=T),
        grid=(nt, IN_NJ),
        in_specs=in_specs,
        out_specs=out_specs,
        out_shape=out_shapes,
        input_output_aliases=aliases,
        scratch_shapes=[
            pltpu.VMEM((tm, D), bf16),
            pltpu.VMEM((IN_NSUB, ns, CONV_PAD + T if has_state else 3 * CONV_PAD, IN_SUB), f32),
            pltpu.VMEM((3, IN_NSUB, CONV_PAD, IN_SUB), f32),
        ],
        compiler_params=pltpu.CompilerParams(
            dimension_semantics=("arbitrary", "arbitrary"), vmem_limit_bytes=VMEM_LIMIT),
        name="inproj",
    )(*args)


def _ssd_consts():
    bigsel = np.zeros((SSD_NG, 3 * LANES, SSD_HPG * LANES), np.float32)
    e8 = np.zeros((SSD_NG, 2 * LANES, SSD_GW), np.float32)
    for g in range(SSD_NG):
        for j in range(SSD_HPG):
            for k in range(3):
                bigsel[g, k * LANES + SSD_HPG * g + j, LANES * j:LANES * (j + 1)] = 1.0
            for k in range(2):
                e8[g, k * LANES + SSD_HPG * g + j, SSD_HD * j:SSD_HD * (j + 1)] = 1.0
    return jnp.asarray(bigsel, bf16), jnp.asarray(e8, bf16)


def _ssd_kernel(*refs, L, sb, nchunks, has_state):
    R = sb * L
    xs4_ref, bc_ref, z4_ref, tail_ref = refs[:4]
    k = 4
    ssm0_ref = None
    if has_state:
        ssm0_ref = refs[k]
        k += 1
    dtb_ref, alog_ref, dsk_ref, nw_ref, wbr_ref, bigsel_ref, e8_ref = refs[k:k + 7]
    br_ref, ssmn_ref, cst_scr = refs[-3:]
    c = pl.program_id(1)

    rowi = lax.broadcasted_iota(jnp.int32, (R, R), 0)
    coli = lax.broadcasted_iota(jnp.int32, (R, R), 1)
    mask = rowi >= coli
    if sb > 1:
        sh = L.bit_length() - 1
        mask = jnp.logical_and(mask, (rowi >> sh) == (coli >> sh))
    tri = mask.astype(bf16)
    lane = lax.broadcasted_iota(jnp.int32, (R, LANES), 1)
    lo_half = lane < SSD_HD

    def init():
        if has_state:
            ssmn_ref[...] = ssm0_ref[...]
        else:
            ssmn_ref[...] = jnp.zeros(ssmn_ref.shape, f32)

    if nchunks == 1:
        init()
    else:
        pl.when(c == 0)(init)

    dtp = _softplus(tail_ref[...] + dtb_ref[...])
    a = -jnp.exp(alog_ref[...])
    cs = _cumsum_rows(dtp * a, tri)
    cst_scr[...] = cs.T
    h3 = jnp.concatenate(_split3(cs), axis=1)
    d2 = jnp.concatenate(_split2(dtp), axis=1)
    br_ref[...] = jnp.zeros(br_ref.shape, f32)

    def seq_last(x):
        if sb == 1:
            return x[R - 1:R, :]
        x3 = x.reshape(sb, L, x.shape[-1])
        return jnp.broadcast_to(x3[:, L - 1:L, :], x3.shape).reshape(x.shape)

    def group_body(g, carry):
        cm = _dot(h3, bigsel_ref[g])
        dt_exp = _dot(d2, e8_ref[g])
        cs_exp = jnp.concatenate(
            [jnp.where(lo_half, cm[:, 2 * LANES * i:2 * LANES * i + LANES],
                       cm[:, 2 * LANES * i + LANES:2 * LANES * (i + 1)]) for i in range(SSD_HPG // 2)], axis=1)
        ecs = jnp.exp(cs_exp)
        ce = seq_last(cs_exp)
        xs = xs4_ref[g].astype(f32)
        xdt = xs * dt_exp
        xdt_b = xdt.astype(bf16)
        xse = xdt * jnp.exp(ce - cs_exp)
        bg = bc_ref[g]
        cg = bc_ref[SSD_NG + g]
        cb = _dot_nt(cg, bg)

        pairs = []
        for i in range(SSD_HPG // 2):
            ws = []
            for j in (2 * i, 2 * i + 1):
                row = cst_scr[pl.ds(SSD_HPG * g + j, 1), :]
                seg = cm[:, LANES * j:LANES * j + R] - row
                dec = jnp.exp(jnp.where(mask, seg, -jnp.inf))
                ws.append((cb * dec).astype(bf16))
            yy = _dot(jnp.concatenate(ws, axis=0), xdt_b[:, LANES * i:LANES * (i + 1)])
            pairs.append(jnp.where(lo_half, yy[:R], yy[R:]))
        y = jnp.concatenate(pairs, axis=1)

        if sb > 1:
            bgf = bg.astype(f32)
            cgf = cg.astype(f32)
        ys_parts = []
        for s in range(sb):
            rs = slice(s * L, (s + 1) * L)
            st = ssmn_ref[s, g]
            b_s = bg if sb == 1 else bgf[rs].astype(bf16)
            c_s = cg if sb == 1 else cgf[rs].astype(bf16)
            ys_parts.append(_dot_nt(c_s, st.astype(bf16)))
            upd = _dot_tn(xse[rs].astype(bf16), b_s)
            e_end = jnp.exp(ce[s * L:s * L + 1, :])
            ssmn_ref[s, g] = jnp.concatenate(
                [st[SSD_HD * j:SSD_HD * (j + 1)] * e_end[:, SSD_HD * j:SSD_HD * j + 1]
                 + upd[SSD_HD * j:SSD_HD * (j + 1)] for j in range(SSD_HPG)], axis=0)
        ys = ys_parts[0] if sb == 1 else jnp.concatenate(ys_parts, axis=0)
        y = y + ys * ecs + dsk_ref[g] * xs

        yg = y * z4_ref[g].astype(f32)
        yn = yg * lax.rsqrt(jnp.mean(yg * yg, axis=-1, keepdims=True) + EPS) * nw_ref[g]
        br_ref[...] += _dot(yn.astype(bf16), wbr_ref[g])
        return carry

    for g in range(SSD_NG):
        group_body(g, 0)


def _ssd(l, xs4, bc, z4, tail, ssm0, ssmn_prev, w, *, nseq, L, sb, nchunks):
    has_state = ssm0 is not None
    dtb, alog, dsk, nw, wbr, bigsel, e8 = w
    ntok = tail.shape[0]
    R = sb * L
    if nchunks == 1:
        rblk = lambda b, c: b
    else:
        rblk = lambda b, c: b * nchunks + c
    in_specs = [
        pl.BlockSpec((SSD_NG, R, SSD_GW), lambda b, c: (0, rblk(b, c), 0)),
        pl.BlockSpec((2 * SSD_NG, R, SSD_NS), lambda b, c: (0, rblk(b, c), 0)),
        pl.BlockSpec((SSD_NG, R, SSD_GW), lambda b, c: (0, rblk(b, c), 0)),
        pl.BlockSpec((R, LANES), lambda b, c: (rblk(b, c), 0)),
    ]
    args = [xs4, bc, z4, tail]
    if has_state:
        in_specs.append(pl.BlockSpec((None, sb, SSD_NG, SSD_GW, SSD_NS), lambda b, c: (l, b, 0, 0, 0)))
        args.append(ssm0)
    in_specs += [
        pl.BlockSpec((None, 1, LANES), lambda b, c: (l, 0, 0)),
        pl.BlockSpec((None, 1, LANES), lambda b, c: (l, 0, 0)),
        pl.BlockSpec((None, SSD_NG, 1, SSD_GW), lambda b, c: (l, 0, 0, 0)),
        pl.BlockSpec((None, SSD_NG, 1, SSD_GW), lambda b, c: (l, 0, 0, 0)),
        pl.BlockSpec((None, SSD_NG, SSD_GW, D), lambda b, c: (l, 0, 0, 0)),
        pl.BlockSpec((SSD_NG, 3 * LANES, SSD_HPG * LANES), lambda b, c: (0, 0, 0)),
        pl.BlockSpec((SSD_NG, 2 * LANES, SSD_GW), lambda b, c: (0, 0, 0)),
    ]
    args += [dtb, alog, dsk, nw, wbr, bigsel, e8]
    aliases = {}
    _stacked_out(ssmn_prev, in_specs, args, aliases, 1)
    return pl.pallas_call(
        functools.partial(_ssd_kernel, L=L, sb=sb, nchunks=nchunks, has_state=has_state),
        grid=(nseq // sb, nchunks),
        in_specs=in_specs,
        out_specs=(
            pl.BlockSpec((R, D), lambda b, c: (rblk(b, c), 0)),
            pl.BlockSpec((None, sb, SSD_NG, SSD_GW, SSD_NS), lambda b, c: (l, b, 0, 0, 0)),
        ),
        out_shape=(
            jax.ShapeDtypeStruct((ntok, D), f32),
            jax.ShapeDtypeStruct((DEPTH, nseq, SSD_NG, SSD_GW, SSD_NS), f32),
        ),
        input_output_aliases=aliases,
        scratch_shapes=[pltpu.VMEM((LANES, R), f32)],
        compiler_params=pltpu.CompilerParams(
            dimension_semantics=("arbitrary", "arbitrary"), vmem_limit_bytes=VMEM_LIMIT),
        name="ssd",
    )(*args)


GLA_C2 = 4


def _gla_consts(L, C1, sb):
    R = sb * L
    nb2 = C1 // GLA_C2
    i = np.arange(R)[:, None]
    s = np.arange(R)[None, :]
    same_seq = (i // L) == (s // L)
    mats = [(s <= i) & (s > i - d) & same_seq for d in range(1, GLA_C2)]
    mats.append((s <= i) & (s >= (i // GLA_C2) * GLA_C2))
    for r in range(1, nb2):
        mats.append((s > i) & (s <= (i // C1) * C1 + GLA_C2 * r - 1))
    shifts = [(s == i - d) & same_seq for d in range(1, GLA_C2)]
    return (jnp.asarray(np.concatenate(mats, 0), bf16), jnp.asarray(np.concatenate(shifts, 0), bf16))


def _gla_kernel(*refs, L, C1, sb, nchunks, has_state):
    R = sb * L
    nb1 = L // C1
    nb2 = C1 // GLA_C2
    qkvr_ref, lg_ref = refs[:2]
    k = 2
    gla0_ref = None
    if has_state:
        gla0_ref = refs[k]
        k += 1
    gnw_ref, wbr_ref, sums_ref, shift_ref = refs[k:k + 4]
    br_ref, glan_ref = refs[-2:]
    c = pl.program_id(1)

    rowi = lax.broadcasted_iota(jnp.int32, (R, R), 0)
    coli = lax.broadcasted_iota(jnp.int32, (R, R), 1)
    shl = L.bit_length() - 1
    same_seq = (rowi >> shl) == (coli >> shl)
    tri = jnp.logical_and(rowi >= coli, same_seq).astype(bf16)
    sh1 = C1.bit_length() - 1
    same_blk1 = (rowi >> sh1) == (coli >> sh1)
    sub_i = (rowi & (C1 - 1)) >> (GLA_C2.bit_length() - 1)
    off_i = rowi & (GLA_C2 - 1)
    rowl = lax.broadcasted_iota(jnp.int32, (R, 1), 0)
    sub_l = (rowl & (C1 - 1)) >> (GLA_C2.bit_length() - 1)
    blk_i = (rowi & (L - 1)) >> sh1
    rel_l = rowl & (L - 1)
    pad_rows = [jnp.zeros((LANES - R, GLA_DK), bf16)] if R < LANES else []

    def init():
        if has_state:
            glan_ref[...] = gla0_ref[...]
        else:
            glan_ref[...] = jnp.zeros(glan_ref.shape, f32)

    if nchunks == 1:
        init()
    else:
        pl.when(c == 0)(init)

    def seq_last(x):
        if sb == 1:
            return x[R - 1:R, :]
        x3 = x.reshape(sb, L, x.shape[-1])
        return jnp.broadcast_to(x3[:, L - 1:L, :], x3.shape).reshape(x.shape)

    def rows2d(x):
        return x.reshape(R, x.shape[-1]) if x.ndim == 3 else x

    lg = rows2d(lg_ref[...])
    g = _cumsum_rows(lg, tri)
    g_end = seq_last(g)
    lg_hi, lg_lo = _split2(lg)
    sums = _dot(sums_ref[...], lg_hi) + _dot(sums_ref[...], lg_lo)
    a_d = [sums[R * (d - 1):R * d] for d in range(1, GLA_C2)]
    a_sub = sums[R * (GLA_C2 - 1):R * GLA_C2]
    b_sub = [sums[R * (GLA_C2 - 1 + r):R * (GLA_C2 + r)] for r in range(1, nb2)]
    q = rows2d(qkvr_ref[..., 0:GLA_KD]).astype(f32)
    kb = rows2d(qkvr_ref[..., GLA_KD:2 * GLA_KD])
    k_ = kb.astype(f32)
    k_sh = [_dot(shift_ref[R * (d - 1):R * d, :], kb) for d in range(1, GLA_C2)]

    acc = jnp.zeros((R, D), f32)
    for h in range(GLA_NH):
        kl = slice(GLA_DK * h, GLA_DK * (h + 1))
        vl = slice(2 * GLA_KD + GLA_DV * h, 2 * GLA_KD + GLA_DV * (h + 1))
        rl = slice(2 * GLA_KD + GLA_VD + GLA_DV * h, 2 * GLA_KD + GLA_VD + GLA_DV * (h + 1))
        gh = g[:, kl]
        qh = q[:, kl]
        kh = k_[:, kl]
        vb = rows2d(qkvr_ref[..., vl])
        qg = qh * jnp.exp(gh)
        kd = kh * jnp.exp(g_end[:, kl] - gh)

        if sb == 1:
            st = glan_ref[0, h]
            o = _dot(qg.astype(bf16), st.astype(bf16))
            e_col = jnp.exp(jnp.broadcast_to(g_end[0:1, kl], (8, GLA_DK))).T[:, 0:1]
            glan_ref[0, h] = st * e_col + _dot_tn(kd.astype(bf16), vb)
        else:
            vf = vb.astype(f32)
            parts = []
            for s in range(sb):
                rs = slice(s * L, (s + 1) * L)
                st = glan_ref[s, h]
                parts.append(_dot(qg[rs].astype(bf16), st.astype(bf16)))
                e_col = jnp.exp(jnp.broadcast_to(g_end[s * L:s * L + 1, kl], (8, GLA_DK))).T[:, 0:1]
                glan_ref[s, h] = st * e_col + _dot_tn(kd[rs].astype(bf16), vf[rs].astype(bf16))
            o = jnp.concatenate(parts, axis=0)

        att = jnp.zeros((R, R), f32)
        if nb1 > 1:
            def bnd(s, i):
                if i == 0:
                    return jnp.zeros((1, GLA_DK), f32)
                return gh[s * L + C1 * i - 1:s * L + C1 * i, :]

            gblk = jnp.concatenate([jnp.broadcast_to(bnd(s, i), (C1, GLA_DK))
                                    for s in range(sb) for i in range(nb1)], axis=0)
            qt = (qh * jnp.exp(gh - gblk)).astype(bf16)
            kts = []
            for i in range(1, nb1):
                gb = jnp.concatenate([jnp.broadcast_to(bnd(s, i), (L, GLA_DK)) for s in range(sb)], axis=0)
                kt = kh * jnp.exp(jnp.where(rel_l < C1 * i, gb - gh, -jnp.inf))
                kts += [kt.astype(bf16)] + pad_rows
            out1 = _dot_nt(qt, jnp.concatenate(kts, axis=0))
            for i in range(1, nb1):
                att = att + jnp.where(jnp.logical_and(blk_i == i, same_seq),
                                      out1[:, LANES * (i - 1):LANES * (i - 1) + R], 0.0)

        qt2 = (qh * jnp.exp(a_sub[:, kl])).astype(bf16)
        kts = []
        for r in range(1, nb2):
            kt = kh * jnp.exp(jnp.where(sub_l < r, b_sub[r - 1][:, kl], -jnp.inf))
            kts += [kt.astype(bf16)] + pad_rows
        out2 = _dot_nt(qt2, jnp.concatenate(kts, axis=0))
        for r in range(1, nb2):
            att = att + jnp.where(jnp.logical_and(sub_i == r, same_blk1),
                                  out2[:, LANES * (r - 1):LANES * (r - 1) + R], 0.0)

        for d in range(GLA_C2):
            t = qh * kh if d == 0 else qh * jnp.exp(a_d[d - 1][:, kl]) * k_sh[d - 1][:, kl]
            band = jnp.sum(t, axis=-1, keepdims=True)
            hit = jnp.logical_and(off_i >= d, coli == rowi - d)
            att = att + jnp.where(hit, band, 0.0)
        o = o + _dot(att.astype(bf16), vb)

        on = o * lax.rsqrt(jnp.mean(o * o, axis=-1, keepdims=True) + EPS) * gnw_ref[...]
        og = on * rows2d(qkvr_ref[..., rl]).astype(f32)
        acc = acc + _dot(og.astype(bf16), wbr_ref[GLA_DV * h:GLA_DV * (h + 1), :])
    br_ref[...] = acc.reshape(br_ref.shape)


def _gla(l, qkvr, lg, gla0, glan_prev, w, *, nseq, L, C1, sb, nchunks):
    has_state = gla0 is not None
    gnw, wbr = w
    sums, shifts = _gla_consts(L, C1, sb)
    ntok = qkvr.shape[0]
    rb = sb * L
    wq = 2 * GLA_KD + 2 * GLA_VD
    if nchunks == 1:
        row = lambda b, c: (b, 0)
        blk = lambda w_: (rb, w_)
        view = lambda a: a
    else:
        row = lambda b, c: (b, c, 0)
        blk = lambda w_: (sb, L, w_)
        view = lambda a: a.reshape(nseq, ntok // nseq, a.shape[-1])
    in_specs = [
        pl.BlockSpec(blk(wq), row),
        pl.BlockSpec(blk(GLA_KD), row),
    ]
    args = [view(qkvr), view(lg)]
    if has_state:
        in_specs.append(pl.BlockSpec((None, sb, GLA_NH, GLA_DK, GLA_DV), lambda b, c: (l, b, 0, 0, 0)))
        args.append(gla0)
    in_specs += [
        pl.BlockSpec((None, 1, GLA_DV), lambda b, c: (l, 0, 0)),
        pl.BlockSpec((None, GLA_VD, D), lambda b, c: (l, 0, 0)),
        pl.BlockSpec(sums.shape, lambda b, c: (0, 0)),
        pl.BlockSpec(shifts.shape, lambda b, c: (0, 0)),
    ]
    args += [gnw, wbr, sums, shifts]
    aliases = {}
    _stacked_out(glan_prev, in_specs, args, aliases, 1)
    br_shape = (ntok, D) if nchunks == 1 else (nseq, ntok // nseq, D)
    br, glan = pl.pallas_call(
        functools.partial(_gla_kernel, L=L, C1=C1, sb=sb, nchunks=nchunks, has_state=has_state),
        grid=(nseq // sb, nchunks),
        in_specs=in_specs,
        out_specs=(
            pl.BlockSpec(blk(D), row),
            pl.BlockSpec((None, sb, GLA_NH, GLA_DK, GLA_DV), lambda b, c: (l, b, 0, 0, 0)),
        ),
        out_shape=(
            jax.ShapeDtypeStruct(br_shape, f32),
            jax.ShapeDtypeStruct((DEPTH, nseq, GLA_NH, GLA_DK, GLA_DV), f32),
        ),
        input_output_aliases=aliases,
        compiler_params=pltpu.CompilerParams(
            dimension_semantics=("arbitrary", "arbitrary"), vmem_limit_bytes=VMEM_LIMIT),
        name="gla",
    )(*args)
    return br.reshape(ntok, D), glan


def _top2_sum(a, b, c, d):
    hi1, lo1 = jnp.maximum(a, b), jnp.minimum(a, b)
    hi2, lo2 = jnp.maximum(c, d), jnp.minimum(c, d)
    return jnp.maximum(hi1, hi2) + jnp.maximum(jnp.minimum(hi1, hi2), jnp.maximum(lo1, lo2))


def _route_rows(sig, biased):
    gsc = [_top2_sum(*biased[EXP_PER_GRP * g:EXP_PER_GRP * (g + 1)]) for g in range(N_EGRP)]
    best = jnp.zeros_like(gsc[0], dtype=jnp.int32)
    m = gsc[0]
    for g in range(1, N_EGRP):
        better = gsc[g] > m
        best = jnp.where(better, g, best)
        m = jnp.where(better, gsc[g], m)
    masked = [jnp.where(best == (e // EXP_PER_GRP), biased[e], -jnp.inf) for e in range(N_EXP)]

    def first_argmax(vals):
        idx = jnp.zeros_like(best)
        mx = vals[0]
        for e in range(1, N_EXP):
            better = vals[e] > mx
            idx = jnp.where(better, e, idx)
            mx = jnp.where(better, vals[e], mx)
        return idx

    i1 = first_argmax(masked)
    i2 = first_argmax([jnp.where(i1 == e, -jnp.inf, masked[e]) for e in range(N_EXP)])
    w1 = sum(jnp.where(i1 == e, sig[e], 0.0) for e in range(N_EXP))
    w2 = sum(jnp.where(i2 == e, sig[e], 0.0) for e in range(N_EXP))
    den = w1 + w2
    return best, i1, i2, w1 / den, w2 / den


def _merge_kernel(*refs, seq3d, sparse):
    (x_ref, bra_ref, brb_ref, gate_ref, mod_ref, wo_ref, n2_ref, wr_hi_ref, wr_lo_ref, rb_ref) = refs[:10]
    mixed_in = (gate_ref[:, 0:D].astype(f32) * bra_ref[...]
                + gate_ref[:, D:2 * D].astype(f32) * brb_ref[...])
    mixed = _dot(mixed_in.astype(bf16), wo_ref[...])
    x = x_ref[...]
    if seq3d:
        x1 = x + mod_ref[:, 2:3, :] * mixed.reshape(x.shape)
    else:
        x1 = x + mod_ref[2:3, :] * mixed
    refs[10][...] = x1
    h2 = _norm_mod(x1, mod_ref, n2_ref, 3, 4, seq3d)
    hi, lo = _split2(h2)
    logits = _dot(hi, wr_hi_ref[...]) + (_dot(hi, wr_lo_ref[...]) + _dot(lo, wr_hi_ref[...]))
    tm = logits.shape[0]
    lt = logits.T
    sig_all = _sigmoid(lt[0:N_EXP, :])
    bias_all = sig_all + rb_ref[...]
    sig = [sig_all[e:e + 1, :] for e in range(N_EXP)]
    biased = [bias_all[e:e + 1, :] for e in range(N_EXP)]
    best, i1, i2, w1, w2 = _route_rows(sig, biased)
    if not sparse:
        h2_ref, comb_ref = refs[11:13]
        h2_ref[...] = h2.astype(bf16)
        comb = [jnp.where(i1 == e, w1, 0.0) + jnp.where(i2 == e, w2, 0.0) for e in range(N_EXP)]
        comb_t = jnp.concatenate(comb + [jnp.zeros((LANES - N_EXP, tm), f32)], axis=0)
        comb_ref[...] = comb_t.T
        return

    h2x_ref, plan_ref, counts_ref, base_scr = refs[11:15]
    swap = i1 > i2
    wa = jnp.where(swap, w2, w1)
    wb = jnp.where(swap, w1, w2)
    a = jnp.minimum(i1, i2) & (EXP_PER_GRP - 1)
    b = jnp.maximum(i1, i2) & (EXP_PER_GRP - 1)
    pair = jnp.where(a == 0, b - 1, jnp.where(a == 1, b + 1, 5))
    cls = best * N_PAIR + pair
    sub = lax.broadcasted_iota(jnp.int32, (CLS_PAD, tm), 0)
    onehot = (sub == cls).astype(f32)
    before = (lax.broadcasted_iota(jnp.int32, (tm, tm), 0)
              < lax.broadcasted_iota(jnp.int32, (tm, tm), 1)).astype(bf16)
    prefix = _dot(onehot.astype(bf16), before)

    @pl.when(pl.program_id(0) == 0)
    def _():
        base_scr[...] = jnp.zeros_like(base_scr)

    base = base_scr[...]
    rank = jnp.sum(onehot * (prefix + base[:, 0:1]), axis=0, keepdims=True)
    base = base + jnp.sum(onehot, axis=1, keepdims=True)
    base_scr[...] = base
    counts_ref[...] = base
    plan_ref[...] = jnp.concatenate(
        [cls, rank.astype(jnp.int32), jnp.zeros((6, tm), jnp.int32)], axis=0)
    ext_t = jnp.concatenate([wa, wb, jnp.zeros((MOE_EXT - 2, tm), f32)], axis=0)
    h2x_ref[:, 0:D] = h2
    h2x_ref[:, D:D + MOE_EXT] = ext_t.T


def _merge(l, x, bra, brb, gate, mod, wo, n2w, wr_hi, wr_lo, rb, *, seq3d, sparse):
    if seq3d:
        ns = _seq_tile(x.shape[0], 32)
        ntok = x.shape[0] * x.shape[1]
        tm = ns * x.shape[1]
        nt = x.shape[0] // ns
        x_spec = pl.BlockSpec((ns, x.shape[1], D), lambda i: (i, 0, 0))
        mod_spec = pl.BlockSpec((None, ns, 6, D), lambda i: (l, i, 0, 0))
    else:
        ntok = x.shape[0]
        nseq = mod.shape[1]
        tm = min(512, ntok // nseq)
        nt = ntok // tm
        per_seq = ntok // nseq // tm
        x_spec = pl.BlockSpec((tm, D), lambda i: (i, 0))
        mod_spec = pl.BlockSpec((None, None, 6, D), lambda i: (l, i // per_seq, 0, 0))
    row = lambda i: (i, 0)
    const = lambda i: (0, 0)
    if sparse:
        out_specs = (x_spec, pl.BlockSpec((tm, D + MOE_EXT), row), pl.BlockSpec((8, tm), lambda i: (0, i)),
                     pl.BlockSpec((CLS_PAD, LANES), const))
        out_shape = (jax.ShapeDtypeStruct(x.shape, f32), jax.ShapeDtypeStruct((ntok, D + MOE_EXT), f32),
                     jax.ShapeDtypeStruct((8, ntok), jnp.int32), jax.ShapeDtypeStruct((CLS_PAD, LANES), f32))
        scratch = [pltpu.VMEM((CLS_PAD, LANES), f32)]
    else:
        out_specs = (x_spec, pl.BlockSpec((tm, D), row), pl.BlockSpec((tm, LANES), row))
        out_shape = (jax.ShapeDtypeStruct(x.shape, f32), jax.ShapeDtypeStruct((ntok, D), bf16),
                     jax.ShapeDtypeStruct((ntok, LANES), f32))
        scratch = []
    return pl.pallas_call(
        functools.partial(_merge_kernel, seq3d=seq3d, sparse=sparse),
        grid=(nt,),
        in_specs=[
            x_spec,
            pl.BlockSpec((tm, D), row),
            pl.BlockSpec((tm, D), row),
            pl.BlockSpec((tm, 2 * D), row),
            mod_spec,
            pl.BlockSpec((None, D, D), lambda i: (l, 0, 0)),
            pl.BlockSpec((None, 1, D), lambda i: (l, 0, 0)),
            pl.BlockSpec((D, LANES), const),
            pl.BlockSpec((D, LANES), const),
            pl.BlockSpec((N_EXP, 1), const),
        ],
        out_specs=out_specs,
        out_shape=out_shape,
        scratch_shapes=scratch,
        compiler_params=pltpu.CompilerParams(
            dimension_semantics=("arbitrary",), vmem_limit_bytes=VMEM_LIMIT),
        name="merge",
    )(x, bra, brb, gate, mod, wo, n2w, wr_hi, wr_lo, rb)


def _dispatch_kernel(pos_ref, src_ref, buf_in, xs_hbm, sem):
    del buf_in
    tm = src_ref.shape[0]

    def issue(t, c):
        p = pos_ref[0, t]
        pltpu.make_async_copy(src_ref.at[pl.ds(t, 1), :], xs_hbm.at[pl.ds(p, 1), :], sem).start()
        return c

    lax.fori_loop(0, tm, issue, 0, unroll=8)

    def drain(t, c):
        pltpu.make_async_copy(src_ref.at[pl.ds(0, 1), :], xs_hbm.at[pl.ds(0, 1), :], sem).wait()
        return c

    lax.fori_loop(0, tm, drain, 0, unroll=8)


def _dispatch(pos3, h2x, xs_buf):
    nt, _, tm = pos3.shape
    return pl.pallas_call(
        _dispatch_kernel,
        grid=(nt,),
        in_specs=[
            pl.BlockSpec((None, 1, tm), lambda i: (i, 0, 0), memory_space=pltpu.SMEM),
            pl.BlockSpec((tm, D + MOE_EXT), lambda i: (i, 0)),
            pl.BlockSpec(memory_space=pl.ANY),
        ],
        out_specs=pl.BlockSpec(memory_space=pl.ANY),
        out_shape=jax.ShapeDtypeStruct(xs_buf.shape, f32),
        input_output_aliases={2: 0},
        scratch_shapes=[pltpu.SemaphoreType.DMA(())],
        compiler_params=pltpu.CompilerParams(dimension_semantics=("arbitrary",)),
        name="dispatch",
    )(pos3, h2x, xs_buf)


def _moe_sparse_kernel(src_ref, valid_ref, ea_ref, eb_ref, xs_ref,
                       wga_ref, wua_ref, wda_ref, wgb_ref, wub_ref, wdb_ref, ys_ref):
    del src_ref, ea_ref, eb_ref
    t = pl.program_id(0)

    @pl.when(valid_ref[t] == 1)
    def _():
        x = xs_ref[...]
        h = x[:, 0:D].astype(bf16)

        def ffn(wg, wu, wd):
            u = (_silu(_dot(h, wg[...])) * _dot(h, wu[...])).astype(bf16)
            return _dot(u, wd[...])

        ys_ref[...] = (x[:, D:D + 1] * ffn(wga_ref, wua_ref, wda_ref)
                       + x[:, D + 1:D + 2] * ffn(wgb_ref, wub_ref, wdb_ref))

    @pl.when(valid_ref[pl.program_id(0)] == 0)
    def _():
        ys_ref[...] = jnp.zeros(ys_ref.shape, f32)


def _moe_sparse(tiles, xs, wg, wu, wd):
    src, valid, ea, eb = tiles
    nrow = xs.shape[0]
    ntile = nrow // MOE_R
    wa_map = lambda t, src, valid, ea, eb: (ea[t], 0, 0)
    wb_map = lambda t, src, valid, ea, eb: (eb[t], 0, 0)
    row_map = lambda t, src, valid, ea, eb: (src[t], 0)
    return pl.pallas_call(
        _moe_sparse_kernel,
        grid_spec=pltpu.PrefetchScalarGridSpec(
            num_scalar_prefetch=4,
            grid=(ntile,),
            in_specs=[
                pl.BlockSpec((MOE_R, D + MOE_EXT), row_map),
                pl.BlockSpec((None, D, EXP_FF), wa_map),
                pl.BlockSpec((None, D, EXP_FF), wa_map),
                pl.BlockSpec((None, EXP_FF, D), wa_map),
                pl.BlockSpec((None, D, EXP_FF), wb_map),
                pl.BlockSpec((None, D, EXP_FF), wb_map),
                pl.BlockSpec((None, EXP_FF, D), wb_map),
            ],
            out_specs=pl.BlockSpec((MOE_R, D), lambda t, src, valid, ea, eb: (t, 0)),
        ),
        out_shape=jax.ShapeDtypeStruct((nrow, D), f32),
        compiler_params=pltpu.CompilerParams(
            dimension_semantics=("arbitrary",), vmem_limit_bytes=VMEM_LIMIT),
        name="moe_sparse",
    )(src, valid, ea, eb, xs, wg, wu, wd, wg, wu, wd)


def _combine_kernel(pos_ref, ys_hbm, x1_ref, mod_ref, fw_ref, x2_ref, buf, sem, *, final):
    tm = x1_ref.shape[0]

    def issue(t, c):
        p = pos_ref[0, t]
        pltpu.make_async_copy(ys_hbm.at[pl.ds(p, 1), :], buf.at[pl.ds(t, 1), :], sem).start()
        return c

    lax.fori_loop(0, tm, issue, 0, unroll=8)

    def drain(t, c):
        pltpu.make_async_copy(ys_hbm.at[pl.ds(0, 1), :], buf.at[pl.ds(0, 1), :], sem).wait()
        return c

    lax.fori_loop(0, tm, drain, 0, unroll=8)
    x2 = x1_ref[...] + mod_ref[5:6, :] * buf[...]
    if final:
        x2 = x2 * lax.rsqrt(jnp.mean(x2 * x2, axis=-1, keepdims=True) + EPS) * fw_ref[...]
    x2_ref[...] = x2


def _combine(l, pos3, ys, x1, mod, fw, *, final):
    nt, _, tm = pos3.shape
    ntok = x1.shape[0]
    per_seq = ntok // mod.shape[1] // tm
    return pl.pallas_call(
        functools.partial(_combine_kernel, final=final),
        grid=(nt,),
        in_specs=[
            pl.BlockSpec((None, 1, tm), lambda i: (i, 0, 0), memory_space=pltpu.SMEM),
            pl.BlockSpec(memory_space=pl.ANY),
            pl.BlockSpec((tm, D), lambda i: (i, 0)),
            pl.BlockSpec((None, None, 6, D), lambda i: (l, i // per_seq, 0, 0)),
            pl.BlockSpec((1, D), lambda i: (0, 0)),
        ],
        out_specs=pl.BlockSpec((tm, D), lambda i: (i, 0)),
        out_shape=jax.ShapeDtypeStruct((ntok, D), f32),
        scratch_shapes=[pltpu.VMEM((tm, D), f32), pltpu.SemaphoreType.DMA(())],
        compiler_params=pltpu.CompilerParams(dimension_semantics=("arbitrary",)),
        name="combine",
    )(pos3, ys, x1, mod, fw)


def _moe_plan(plan, counts, ntile):
    cnt = counts[:N_CLS, 0].astype(jnp.int32)
    tiles_per = (cnt + (MOE_R - 1)) // MOE_R
    tstart = jnp.cumsum(tiles_per) - tiles_per
    total = jnp.sum(tiles_per)
    pos = tstart[plan[0]] * MOE_R + plan[1]
    t = jnp.arange(ntile, dtype=jnp.int32)
    src = jnp.minimum(t, total - 1)
    tcls = jnp.sum((src[:, None] >= tstart[None, :]).astype(jnp.int32), axis=1) - 1
    grp = tcls // N_PAIR
    pair = tcls % N_PAIR
    pa = jnp.asarray(np.array([0, 0, 0, 1, 1, 2], np.int32))
    pb = jnp.asarray(np.array([1, 2, 3, 2, 3, 3], np.int32))
    ea = grp * EXP_PER_GRP + pa[pair]
    eb = grp * EXP_PER_GRP + pb[pair]
    valid = (t < total).astype(jnp.int32)
    return pos, (src, valid, ea, eb)


def _moe_kernel(h2_ref, comb_ref, wg_ref, wu_ref, wd_ref, x1_ref, mod_ref,
                x2_ref, wgb_ref, wub_ref, wdb_ref, acc_scr, *, seq3d):
    e = pl.program_id(1)

    @pl.when(e == 0)
    def _():
        acc_scr[...] = jnp.zeros_like(acc_scr)

    wgb_ref[...] = wg_ref[...].astype(bf16)
    wub_ref[...] = wu_ref[...].astype(bf16)
    wdb_ref[...] = wd_ref[...].astype(bf16)
    h = h2_ref[...]
    a = _dot(h, wgb_ref[...])
    b = _dot(h, wub_ref[...])
    t = (_silu(a) * b).astype(bf16)
    ye = _dot(t, wdb_ref[...])
    lane = lax.broadcasted_iota(jnp.int32, comb_ref.shape, 1)
    w = jnp.sum(jnp.where(lane == e, comb_ref[...], 0.0), axis=-1, keepdims=True)
    acc_scr[...] += w * ye

    @pl.when(e == N_EXP - 1)
    def _():
        x1 = x1_ref[...]
        if seq3d:
            x2_ref[...] = x1 + mod_ref[:, 5:6, :] * acc_scr[...].reshape(x1.shape)
        else:
            x2_ref[...] = x1 + mod_ref[5:6, :] * acc_scr[...]


def _moe(l, h2, comb, wg, wu, wd, x1, mod, *, seq3d):
    if seq3d:
        ns = x1.shape[0]
        ntok = ns * x1.shape[1]
        tm = ntok
        nt = 1
        x_spec = pl.BlockSpec((ns, x1.shape[1], D), lambda i, e: (0, 0, 0))
        mod_spec = pl.BlockSpec((None, ns, 6, D), lambda i, e: (l, 0, 0, 0))
    else:
        ntok = x1.shape[0]
        nseq = mod.shape[1]
        tm = min(1024, ntok // nseq)
        nt = ntok // tm
        per_seq = ntok // nseq // tm
        x_spec = pl.BlockSpec((tm, D), lambda i, e: (i, 0))
        mod_spec = pl.BlockSpec((None, None, 6, D), lambda i, e: (l, i // per_seq, 0, 0))
    assert nt == 1, "the bf16 weight copies are written once per expert"
    row = lambda i, e: (i, 0)
    return pl.pallas_call(
        functools.partial(_moe_kernel, seq3d=seq3d),
        grid=(nt, N_EXP),
        in_specs=[
            pl.BlockSpec((tm, D), row),
            pl.BlockSpec((tm, LANES), row),
            pl.BlockSpec((None, None, D, EXP_FF), lambda i, e: (l, e, 0, 0)),
            pl.BlockSpec((None, None, D, EXP_FF), lambda i, e: (l, e, 0, 0)),
            pl.BlockSpec((None, None, EXP_FF, D), lambda i, e: (l, e, 0, 0)),
            x_spec,
            mod_spec,
        ],
        out_specs=(
            x_spec,
            pl.BlockSpec((None, D, EXP_FF), lambda i, e: (e, 0, 0)),
            pl.BlockSpec((None, D, EXP_FF), lambda i, e: (e, 0, 0)),
            pl.BlockSpec((None, EXP_FF, D), lambda i, e: (e, 0, 0)),
        ),
        out_shape=(
            jax.ShapeDtypeStruct(x1.shape, f32),
            jax.ShapeDtypeStruct((N_EXP, D, EXP_FF), bf16),
            jax.ShapeDtypeStruct((N_EXP, D, EXP_FF), bf16),
            jax.ShapeDtypeStruct((N_EXP, EXP_FF, D), bf16),
        ),
        scratch_shapes=[pltpu.VMEM((tm, D), f32)],
        compiler_params=pltpu.CompilerParams(
            dimension_semantics=("arbitrary", "arbitrary"), vmem_limit_bytes=VMEM_LIMIT),
        name="moe",
    )(h2, comb, wg, wu, wd, x1, mod)


def _final_norm_kernel(x_ref, w_ref, o_ref):
    x = x_ref[...]
    o_ref[...] = x * lax.rsqrt(jnp.mean(x * x, axis=-1, keepdims=True) + EPS) * w_ref[...]


def _final_norm(x2d, w):
    ntok = x2d.shape[0]
    tm = min(1024, ntok)
    return pl.pallas_call(
        _final_norm_kernel,
        grid=(ntok // tm,),
        in_specs=[pl.BlockSpec((tm, D), lambda i: (i, 0)), pl.BlockSpec((1, D), lambda i: (0, 0))],
        out_specs=pl.BlockSpec((tm, D), lambda i: (i, 0)),
        out_shape=jax.ShapeDtypeStruct((ntok, D), f32),
        compiler_params=pltpu.CompilerParams(dimension_semantics=("arbitrary",)),
        name="final_norm",
    )(x2d, w)


def _prep_in_weights(w_in):
    o = 0
    z = w_in[:, :, o:o + SSD_INNER]; o += SSD_INNER
    xbc = w_in[:, :, o:o + CONV_DIM]; o += CONV_DIM
    dt = w_in[:, :, o:o + SSD_NH]; o += SSD_NH
    q = w_in[:, :, o:o + GLA_KD] * (GLA_DK ** -0.5); o += GLA_KD
    kvr = w_in[:, :, o:o + GLA_KD + 2 * GLA_VD]; o += GLA_KD + 2 * GLA_VD
    glr = w_in[:, :, o:o + GLA_RANK]; o += GLA_RANK
    gates = w_in[:, :, o:o + 2 * D]
    main = jnp.concatenate([z, xbc, q, kvr, gates], axis=-1).astype(bf16)
    pad = jnp.zeros(w_in.shape[:2] + (LANES - SSD_NH - GLA_RANK,), w_in.dtype)
    tail = jnp.concatenate([dt, glr, pad], axis=-1).astype(bf16)
    return main, tail


def _pad_lanes(v, width=LANES):
    return jnp.concatenate([v, jnp.zeros(v.shape[:-1] + (width - v.shape[-1],), v.dtype)], axis=-1)


def kernel(x_prompt, x_sample, c_prompt, c_sample, state_conv, state_ssm, state_gla, w_ada, b_ada, norm1_w, w_in, conv_w, conv_b, dt_bias, a_log, d_skip, ssd_norm_w, w_ssd_br, gla_gate_up, gla_gate_b, gla_norm_w, w_gla_br, merge_b, w_out, norm2_w, w_router, router_bias, w_exp_gate, w_exp_up, w_exp_down, final_norm_w):
    bp, sp, _ = x_prompt.shape
    bs, ss, _ = x_sample.shape

    mod = _adaln(jnp.concatenate([c_prompt, c_sample], axis=0), w_ada, b_ada)
    mod_p = mod[:, :bp].reshape(DEPTH, bp, 6, D)
    mod_s = mod[:, bp:].reshape(DEPTH, bs, 6, D)

    w_main, w_tail = _prep_in_weights(w_in)
    gate_up = jnp.concatenate(
        [jnp.zeros((DEPTH, TAIL_GLR0, GLA_KD), f32), gla_gate_up,
         jnp.zeros((DEPTH, LANES - TAIL_GLR0 - GLA_RANK, GLA_KD), f32)], axis=1).astype(bf16)
    in_w = (norm1_w.reshape(DEPTH, 1, D), w_main, w_tail, gate_up, gla_gate_b.reshape(DEPTH, 1, GLA_KD),
            conv_w, conv_b.reshape(DEPTH, 1, CONV_DIM), merge_b.reshape(DEPTH, 1, 2 * D))
    bigsel, e8 = _ssd_consts()
    ssd_w = (_pad_lanes(dt_bias).reshape(DEPTH, 1, LANES), _pad_lanes(a_log).reshape(DEPTH, 1, LANES),
             jnp.repeat(d_skip, SSD_HD, axis=-1).reshape(DEPTH, SSD_NG, 1, SSD_GW),
             ssd_norm_w.reshape(DEPTH, SSD_NG, 1, SSD_GW),
             w_ssd_br.astype(bf16).reshape(DEPTH, SSD_NG, SSD_GW, D), bigsel, e8)
    gla_w = (gla_norm_w.reshape(DEPTH, 1, GLA_DV), w_gla_br.astype(bf16))
    w_out_b = w_out.astype(bf16)
    n2 = norm2_w.reshape(DEPTH, 1, D)
    wr = _pad_lanes(w_router)
    wr_hi = wr.astype(bf16)
    wr_lo = (wr - wr_hi.astype(f32)).astype(bf16)
    rb = router_bias.reshape(N_EXP, 1)

    xp = x_prompt.reshape(bp * sp, D)
    xs = x_sample
    ssm_s = state_ssm.reshape(DEPTH, bs, SSD_NG, SSD_GW, SSD_NS)
    ssd_lp = min(128, sp)
    gla_lp = min(64, sp)
    cp = sp_ = gp = cs_ = ss_ = gs = None
    moe_tm = min(MOE_TM, sp)
    moe_tiles = -(-(bp * sp) // MOE_R) + N_CLS
    xs_rows = jnp.zeros((moe_tiles * MOE_R, D + MOE_EXT), f32)
    fw = final_norm_w.reshape(1, D)
    for l in range(DEPTH):
        z4, xs4, bc, qkvr, gate, tail, lg, cs_ = _inproj(l, xs, mod_s, state_conv, cs_, in_w, seq3d=True)
        bra, ss_ = _ssd(l, xs4, bc, z4, tail, ssm_s, ss_, ssd_w, nseq=bs, L=ss, sb=8, nchunks=1)
        brb, gs = _gla(l, qkvr, lg, state_gla, gs, gla_w, nseq=bs, L=ss, C1=ss, sb=8, nchunks=1)
        x1, h2, comb = _merge(l, xs, bra, brb, gate, mod_s, w_out_b, n2, wr_hi, wr_lo, rb,
                              seq3d=True, sparse=False)
        xs, wg_b, wu_b, wd_b = _moe(l, h2, comb, w_exp_gate, w_exp_up, w_exp_down, x1, mod_s, seq3d=True)

        z4, xs4, bc, qkvr, gate, tail, lg, cp = _inproj(l, xp, mod_p, None, cp, in_w, seq3d=False)
        bra, sp_ = _ssd(l, xs4, bc, z4, tail, None, sp_, ssd_w, nseq=bp, L=ssd_lp, sb=1, nchunks=sp // ssd_lp)
        brb, gp = _gla(l, qkvr, lg, None, gp, gla_w, nseq=bp, L=gla_lp, C1=16,
                       sb=2 if bp % 2 == 0 else 1, nchunks=sp // gla_lp)
        x1, h2x, plan, counts = _merge(l, xp, bra, brb, gate, mod_p, w_out_b, n2, wr_hi, wr_lo, rb,
                                       seq3d=False, sparse=True)
        pos, tiles = _moe_plan(plan, counts, moe_tiles)
        pos3 = pos.reshape(bp * sp // moe_tm, 1, moe_tm)
        xs_rows = _dispatch(pos3, h2x, xs_rows)
        ys_rows = _moe_sparse(tiles, xs_rows, wg_b, wu_b, wd_b)
        xp = _combine(l, pos3, ys_rows, x1, mod_p, fw, final=(l == DEPTH - 1))

    y_prompt = xp.reshape(bp, sp, D)
    y_sample = _final_norm(xs.reshape(bs * ss, D), fw).reshape(bs, ss, D)
    return (y_prompt, y_sample, cp, sp_.reshape(DEPTH, bp, SSD_NH, SSD_HD, SSD_NS), gp,
            cs_, ss_.reshape(DEPTH, bs, SSD_NH, SSD_HD, SSD_NS), gs)
```

```python
import functools

import numpy as np
import jax
import jax.numpy as jnp
from jax import lax
from jax.experimental import pallas as pl
from jax.experimental.pallas import tpu as pltpu

f32 = jnp.float32
bf16 = jnp.bfloat16

D = 1024
DEPTH = 4
SSD_INNER = 2048
SSD_HD = 64
SSD_NH = 32
SSD_NS = 128
SSD_NG = 4
SSD_HPG = 8
SSD_GW = SSD_INNER // SSD_NG
CONV_K = 4
CONV_DIM = SSD_INNER + 2 * SSD_NG * SSD_NS
GLA_NH = 4
GLA_DK = 128
GLA_DV = 256
GLA_KD = GLA_NH * GLA_DK
GLA_VD = GLA_NH * GLA_DV
GLA_RANK = 16
GLA_TAU = 16.0
N_EXP = 16
N_EGRP = 4
EXP_PER_GRP = 4
EXP_FF = 512
N_PAIR = 6
N_CLS = N_EGRP * N_PAIR
CLS_PAD = 32
MOE_EXT = 128
MOE_R = 256
MOE_TM = 512
MOE_DMA_UNROLL = 32
EPS = 1e-6
LANES = 128
TAIL_GLR0 = SSD_NH
MAIN_W = SSD_INNER + CONV_DIM + 2 * GLA_KD + 2 * GLA_VD + 2 * D
VMEM_LIMIT = 56 * 1024 * 1024


def _sigmoid(x):
    return 1.0 / (1.0 + jnp.exp(-x))


def _silu(x):
    return x * _sigmoid(x)


def _softplus(x):
    return jnp.maximum(x, 0.0) + jnp.log1p(jnp.exp(-jnp.abs(x)))


def _log_sigmoid(x):
    return jnp.minimum(x, 0.0) - jnp.log1p(jnp.exp(-jnp.abs(x)))


def _dot(a, b):
    return jnp.dot(a, b, preferred_element_type=f32)


def _dot_nt(a, b):
    return lax.dot_general(a, b, (((1,), (1,)), ((), ())), preferred_element_type=f32)


def _dot_tn(a, b):
    return lax.dot_general(a, b, (((0,), (0,)), ((), ())), preferred_element_type=f32)


def _split2(x):
    hi = x.astype(bf16)
    return hi, (x - hi.astype(f32)).astype(bf16)


def _split3(x):
    hi = x.astype(bf16)
    r1 = x - hi.astype(f32)
    mid = r1.astype(bf16)
    lo = (r1 - mid.astype(f32)).astype(bf16)
    return hi, mid, lo


def _cumsum_rows(x, tri):
    hi, mid, lo = _split3(x)
    if tri.shape[1] % LANES == 0:
        return _dot(jnp.concatenate([tri, tri, tri], axis=1), jnp.concatenate([hi, mid, lo], axis=0))
    return _dot(tri, hi) + _dot(tri, mid) + _dot(tri, lo)


def _tri(n):
    r = lax.broadcasted_iota(jnp.int32, (n, n), 0)
    c = lax.broadcasted_iota(jnp.int32, (n, n), 1)
    return r >= c


def _norm_mod(x, mod_ref, w_ref, i_shift, i_scale, seq3d):
    ms = jnp.mean(x * x, axis=-1, keepdims=True)
    y = x * lax.rsqrt(ms + EPS) * w_ref[...]
    if seq3d:
        sc = mod_ref[:, i_scale:i_scale + 1, :]
        sh = mod_ref[:, i_shift:i_shift + 1, :]
        h = y * (1.0 + sc) + sh
        return h.reshape(h.shape[0] * h.shape[1], h.shape[2])
    sc = mod_ref[i_scale:i_scale + 1, :]
    sh = mod_ref[i_shift:i_shift + 1, :]
    return y * (1.0 + sc) + sh


def _seq_tile(nseq, want):
    return want if nseq % want == 0 else nseq


def _stacked_out(prev, in_specs, args, aliases, out_index):
    if prev is not None:
        in_specs.append(pl.BlockSpec(memory_space=pl.ANY))
        args.append(prev)
        aliases[len(args) - 1] = out_index


def _adaln_kernel(c_ref, w_ref, b_ref, o_ref):
    s = _silu(c_ref[...]).astype(bf16)
    o_ref[...] = _dot(s, w_ref[...].astype(bf16)) + b_ref[...]


def _adaln(c_all, w_ada, b_ada):
    n = c_all.shape[0]
    tn = 1024
    return pl.pallas_call(
        _adaln_kernel,
        grid=(DEPTH, 6 * D // tn),
        in_specs=[
            pl.BlockSpec((n, D), lambda l, j: (0, 0)),
            pl.BlockSpec((None, D, tn), lambda l, j: (l, 0, j)),
            pl.BlockSpec((None, 1, tn), lambda l, j: (l, 0, j)),
        ],
        out_specs=pl.BlockSpec((None, n, tn), lambda l, j: (l, 0, j)),
        out_shape=jax.ShapeDtypeStruct((DEPTH, n, 6 * D), f32),
        compiler_params=pltpu.CompilerParams(
            dimension_semantics=("arbitrary", "arbitrary"), vmem_limit_bytes=VMEM_LIMIT),
        name="adaln",
    )(c_all, w_ada, b_ada.reshape(DEPTH, 1, 6 * D))


IN_TN = 1024
IN_NJ = MAIN_W // IN_TN
IN_SUB = 256
IN_NSUB = IN_TN // IN_SUB
CONV_PAD = 8
KPREV = CONV_K - 1


def _inproj_kernel(*refs, seq3d, has_state, per_seq, T):
    if has_state:
        (x_ref, mod_ref, n1_ref, wm_ref, wt_ref, wg_ref, gb_ref, cw_ref, cb_ref, mb_ref, conv0_ref) = refs[:11]
        rest = refs[11:]
    else:
        (x_ref, mod_ref, n1_ref, wm_ref, wt_ref, wg_ref, gb_ref, cw_ref, cb_ref, mb_ref) = refs[:10]
        conv0_ref = None
        rest = refs[10:]
    (z4_ref, xs4_ref, bc_ref, qkvr_ref, gate_ref, tail_ref, lg_ref, convn_ref,
     h_scr, cscr, carry) = rest[-11:]
    i = pl.program_id(0)
    j = pl.program_id(1)
    tm = h_scr.shape[0]
    ns = cscr.shape[1]

    @pl.when(j == 0)
    def _():
        h = _norm_mod(x_ref[...], mod_ref, n1_ref, 0, 1, seq3d).astype(bf16)
        h_scr[...] = h
        t = _dot(h, wt_ref[...])
        tail_ref[...] = t
        pre = _dot(t.astype(bf16), wg_ref[...]) + gb_ref[...]
        lg_ref[...] = _log_sigmoid(pre) * (1.0 / GLA_TAU)

    def sub_dot(c):
        return _dot(h_scr[...], wm_ref[:, IN_SUB * c:IN_SUB * (c + 1)])

    def conv_silu(a, cj, c):
        cols = slice(IN_SUB * c, IN_SUB * (c + 1))
        gcols = slice(IN_TN * cj + IN_SUB * c, IN_TN * cj + IN_SUB * (c + 1))
        if not has_state:
            prev = jnp.where(i % per_seq == 0, 0.0, carry[cj, c, CONV_PAD - KPREV:CONV_PAD, :])
            cscr[c, 0, CONV_PAD - KPREV:CONV_PAD, :] = prev
            cscr[c, 0, CONV_PAD:2 * CONV_PAD, :] = a[0:CONV_PAD]
            cscr[c, 0, 2 * CONV_PAD:3 * CONV_PAD, :] = a[tm - CONV_PAD:tm]
            w = [cw_ref[k:k + 1, cols] for k in range(CONV_K)]
            out = cb_ref[:, cols] + a * w[KPREV]
            head = cb_ref[:, cols] + a[0:CONV_PAD] * w[KPREV]
            for k in range(KPREV):
                out = out + pltpu.roll(a, KPREV - k, 0) * w[k]
                head = head + cscr[c, 0, CONV_PAD - KPREV + k:2 * CONV_PAD - KPREV + k, :] * w[k]
            out = jnp.concatenate([head, out[CONV_PAD:]], axis=0)
            new_tail = cscr[c, 0, 3 * CONV_PAD - KPREV:3 * CONV_PAD, :]
            convn_ref[0, :, gcols] = new_tail
            carry[cj, c, CONV_PAD - KPREV:CONV_PAD, :] = new_tail
            return _silu(out).astype(bf16)
        a3 = a.reshape(ns, T, IN_SUB)
        prev = conv0_ref[:, :, cols]
        cscr[c, :, CONV_PAD - KPREV:CONV_PAD, :] = prev
        cscr[c, :, CONV_PAD:CONV_PAD + T, :] = a3
        out = cb_ref[:, cols] + a3 * cw_ref[KPREV:KPREV + 1, cols]
        for k in range(KPREV):
            out = out + cscr[c, :, CONV_PAD - KPREV + k:CONV_PAD - KPREV + k + T, :] * cw_ref[k:k + 1, cols]
        new_tail = cscr[c, :, CONV_PAD + T - KPREV:CONV_PAD + T, :]
        convn_ref[:, :, gcols] = new_tail
        return _silu(out).reshape(tm, IN_SUB).astype(bf16)

    per_grp = SSD_GW // IN_SUB

    @pl.when(j < 2)
    def _():
        for c in range(IN_NSUB):
            lanes = slice(IN_SUB * (c % per_grp), IN_SUB * (c % per_grp + 1))
            z4_ref[c // per_grp, :, lanes] = _silu(sub_dot(c)).astype(bf16)

    for cj in range(2):
        @pl.when(j == 2 + cj)
        def _(cj=cj):
            for c in range(IN_NSUB):
                lanes = slice(IN_SUB * (c % per_grp), IN_SUB * (c % per_grp + 1))
                xs4_ref[c // per_grp, :, lanes] = conv_silu(sub_dot(c), cj, c)

    @pl.when(j == 4)
    def _():
        for c in range(IN_NSUB):
            xc = conv_silu(sub_dot(c), 2, c)
            for q in range(IN_SUB // SSD_NS):
                bc_ref[c * (IN_SUB // SSD_NS) + q] = xc[:, SSD_NS * q:SSD_NS * (q + 1)]

    @pl.when(jnp.logical_and(j >= 5, j < 7))
    def _():
        for c in range(IN_NSUB):
            qkvr_ref[:, IN_SUB * c:IN_SUB * (c + 1)] = sub_dot(c).astype(bf16)

    @pl.when(j == 7)
    def _():
        for c in range(IN_NSUB):
            qkvr_ref[:, IN_SUB * c:IN_SUB * (c + 1)] = _silu(sub_dot(c)).astype(bf16)

    @pl.when(j >= 8)
    def _():
        for c in range(IN_NSUB):
            cols = slice(IN_SUB * c, IN_SUB * (c + 1))
            gate_ref[:, cols] = _sigmoid(sub_dot(c) + mb_ref[:, cols]).astype(bf16)


def _inproj(l, x, mod, conv0, convn_prev, w, *, seq3d):
    (n1w, w_main, w_tail, gate_up, gate_b, conv_w, conv_b, merge_b) = w
    has_state = conv0 is not None
    if seq3d:
        nseq, T = x.shape[0], x.shape[1]
        ns = _seq_tile(nseq, 128)
        ntok = nseq * T
        tm = ns * T
        nt = nseq // ns
        per_seq = 1
        x_spec = pl.BlockSpec((ns, T, D), lambda i, j: (i, 0, 0))
        mod_spec = pl.BlockSpec((None, ns, 6, D), lambda i, j: (l, i, 0, 0))
        seq_blk = lambda i: i
    else:
        nseq = mod.shape[1]
        ntok = x.shape[0]
        tm = min(1024, ntok // nseq)
        T = tm
        ns = 1
        nt = ntok // tm
        per_seq = ntok // nseq // tm
        x_spec = pl.BlockSpec((tm, D), lambda i, j: (i, 0))
        mod_spec = pl.BlockSpec((None, None, 6, D), lambda i, j: (l, i // per_seq, 0, 0))
        seq_blk = lambda i: i // per_seq

    def cj(j):
        return jnp.clip(j - 2, 0, 2)

    in_specs = [
        x_spec,
        mod_spec,
        pl.BlockSpec((None, 1, D), lambda i, j: (l, 0, 0)),
        pl.BlockSpec((None, D, IN_TN), lambda i, j: (l, 0, j)),
        pl.BlockSpec((None, D, LANES), lambda i, j: (l, 0, 0)),
        pl.BlockSpec((None, LANES, GLA_KD), lambda i, j: (l, 0, 0)),
        pl.BlockSpec((None, 1, GLA_KD), lambda i, j: (l, 0, 0)),
        pl.BlockSpec((None, CONV_K, IN_TN), lambda i, j: (l, 0, cj(j))),
        pl.BlockSpec((None, 1, IN_TN), lambda i, j: (l, 0, cj(j))),
        pl.BlockSpec((None, 1, IN_TN), lambda i, j: (l, 0, jnp.clip(j - 8, 0, 1))),
    ]
    args = [x, mod, n1w, w_main, w_tail, gate_up, gate_b, conv_w, conv_b, merge_b]
    if has_state:
        in_specs.append(pl.BlockSpec((None, ns, KPREV, IN_TN), lambda i, j: (l, i, 0, cj(j))))
        args.append(conv0)
    aliases = {}
    _stacked_out(convn_prev, in_specs, args, aliases, 7)
    out_shapes = (
        jax.ShapeDtypeStruct((SSD_NG, ntok, SSD_GW), bf16),
        jax.ShapeDtypeStruct((SSD_NG, ntok, SSD_GW), bf16),
        jax.ShapeDtypeStruct((2 * SSD_NG, ntok, SSD_NS), bf16),
        jax.ShapeDtypeStruct((ntok, 2 * GLA_KD + 2 * GLA_VD), bf16),
        jax.ShapeDtypeStruct((ntok, 2 * D), bf16),
        jax.ShapeDtypeStruct((ntok, LANES), f32),
        jax.ShapeDtypeStruct((ntok, GLA_KD), f32),
        jax.ShapeDtypeStruct((DEPTH, nseq, KPREV, CONV_DIM), f32),
    )
    out_specs = (
        pl.BlockSpec((2, tm, SSD_GW), lambda i, j: (jnp.clip(j, 0, 1), i, 0)),
        pl.BlockSpec((2, tm, SSD_GW), lambda i, j: (jnp.clip(j - 2, 0, 1), i, 0)),
        pl.BlockSpec((2 * SSD_NG, tm, SSD_NS), lambda i, j: (0, i, 0)),
        pl.BlockSpec((tm, IN_TN), lambda i, j: (i, jnp.clip(j - 5, 0, 2))),
        pl.BlockSpec((tm, IN_TN), lambda i, j: (i, jnp.clip(j - 8, 0, 1))),
        pl.BlockSpec((tm, LANES), lambda i, j: (i, 0)),
        pl.BlockSpec((tm, GLA_KD), lambda i, j: (i, 0)),
        pl.BlockSpec((None, ns, KPREV, CONV_DIM), lambda i, j: (l, seq_blk(i), 0, 0)),
    )
    return pl.pallas_call(
        functools.partial(_inproj_kernel, seq3d=seq3d, has_state=has_state, per_seq=per_seq, T=T),
        grid=(nt, IN_NJ),
        in_specs=in_specs,
        out_specs=out_specs,
        out_shape=out_shapes,
        input_output_aliases=aliases,
        scratch_shapes=[
            pltpu.VMEM((tm, D), bf16),
            pltpu.VMEM((IN_NSUB, ns, CONV_PAD + T if has_state else 3 * CONV_PAD, IN_SUB), f32),
            pltpu.VMEM((3, IN_NSUB, CONV_PAD, IN_SUB), f32),
        ],
        compiler_params=pltpu.CompilerParams(
            dimension_semantics=("arbitrary", "arbitrary"), vmem_limit_bytes=VMEM_LIMIT),
        name="inproj",
    )(*args)


SSD_CS_PIECES = 2


def _ssd_consts():
    bigsel = np.zeros((SSD_NG, SSD_CS_PIECES * LANES, SSD_HPG * LANES), np.float32)
    e8 = np.zeros((SSD_NG, 2 * LANES, SSD_GW), np.float32)
    for g in range(SSD_NG):
        for j in range(SSD_HPG):
            for k in range(SSD_CS_PIECES):
                bigsel[g, k * LANES + SSD_HPG * g + j, LANES * j:LANES * (j + 1)] = 1.0
            for k in range(2):
                e8[g, k * LANES + SSD_HPG * g + j, SSD_HD * j:SSD_HD * (j + 1)] = 1.0
    return jnp.asarray(bigsel, bf16), jnp.asarray(e8, bf16)


def _ssd_kernel(*refs, L, sb, nchunks, has_state):
    R = sb * L
    xs4_ref, bc_ref, z4_ref, tail_ref = refs[:4]
    k = 4
    ssm0_ref = None
    if has_state:
        ssm0_ref = refs[k]
        k += 1
    dtb_ref, alog_ref, dsk_ref, nw_ref, wbr_ref, bigsel_ref, e8_ref = refs[k:k + 7]
    br_ref, ssmn_ref, cst_scr = refs[-3:]
    c = pl.program_id(1)

    rowi = lax.broadcasted_iota(jnp.int32, (R, R), 0)
    coli = lax.broadcasted_iota(jnp.int32, (R, R), 1)
    mask = rowi >= coli
    if sb > 1:
        sh = L.bit_length() - 1
        mask = jnp.logical_and(mask, (rowi >> sh) == (coli >> sh))
    tri = mask.astype(bf16)
    lane = lax.broadcasted_iota(jnp.int32, (R, LANES), 1)
    lo_half = lane < SSD_HD

    def init():
        if has_state:
            ssmn_ref[...] = ssm0_ref[...]
        else:
            ssmn_ref[...] = jnp.zeros(ssmn_ref.shape, f32)

    if nchunks == 1:
        init()
    else:
        pl.when(c == 0)(init)

    dtp = _softplus(tail_ref[...] + dtb_ref[...])
    a = -jnp.exp(alog_ref[...])
    cs = _cumsum_rows(dtp * a, tri)
    cst_scr[...] = cs.T
    h3 = jnp.concatenate(_split3(cs)[:SSD_CS_PIECES], axis=1)
    d2 = jnp.concatenate(_split2(dtp), axis=1)
    br_ref[...] = jnp.zeros(br_ref.shape, f32)

    def seq_last(x):
        if sb == 1:
            return x[R - 1:R, :]
        x3 = x.reshape(sb, L, x.shape[-1])
        return jnp.broadcast_to(x3[:, L - 1:L, :], x3.shape).reshape(x.shape)

    def group_body(g, carry):
        cm = _dot(h3, bigsel_ref[g])
        dt_exp = _dot(d2, e8_ref[g])
        cs_exp = jnp.concatenate(
            [jnp.where(lo_half, cm[:, 2 * LANES * i:2 * LANES * i + LANES],
                       cm[:, 2 * LANES * i + LANES:2 * LANES * (i + 1)]) for i in range(SSD_HPG // 2)], axis=1)
        ecs = jnp.exp(cs_exp)
        ce = seq_last(cs_exp)
        xs = xs4_ref[g].astype(f32)
        xdt = xs * dt_exp
        xdt_b = xdt.astype(bf16)
        xse = xdt * jnp.exp(ce - cs_exp)
        bg = bc_ref[g]
        cg = bc_ref[SSD_NG + g]
        cb = _dot_nt(cg, bg)

        pairs = []
        for i in range(SSD_HPG // 2):
            ws = []
            for j in (2 * i, 2 * i + 1):
                row = cst_scr[pl.ds(SSD_HPG * g + j, 1), :]
                seg = cm[:, LANES * j:LANES * j + R] - row
                dec = jnp.exp(jnp.where(mask, seg, -jnp.inf))
                ws.append((cb * dec).astype(bf16))
            yy = _dot(jnp.concatenate(ws, axis=0), xdt_b[:, LANES * i:LANES * (i + 1)])
            pairs.append(jnp.where(lo_half, yy[:R], yy[R:]))
        y = jnp.concatenate(pairs, axis=1)

        if sb > 1:
            bgf = bg.astype(f32)
            cgf = cg.astype(f32)
        ys_parts = []
        for s in range(sb):
            rs = slice(s * L, (s + 1) * L)
            st = ssmn_ref[s, g]
            b_s = bg if sb == 1 else bgf[rs].astype(bf16)
            c_s = cg if sb == 1 else cgf[rs].astype(bf16)
            ys_parts.append(_dot_nt(c_s, st.astype(bf16)))
            upd = _dot_tn(xse[rs].astype(bf16), b_s)
            e_end = jnp.exp(ce[s * L:s * L + 1, :])
            ssmn_ref[s, g] = jnp.concatenate(
                [st[SSD_HD * j:SSD_HD * (j + 1)] * e_end[:, SSD_HD * j:SSD_HD * j + 1]
                 + upd[SSD_HD * j:SSD_HD * (j + 1)] for j in range(SSD_HPG)], axis=0)
        ys = ys_parts[0] if sb == 1 else jnp.concatenate(ys_parts, axis=0)
        y = y + ys * ecs + dsk_ref[g] * xs

        yg = y * z4_ref[g].astype(f32)
        yn = yg * lax.rsqrt(jnp.mean(yg * yg, axis=-1, keepdims=True) + EPS) * nw_ref[g]
        br_ref[...] += _dot(yn.astype(bf16), wbr_ref[g])
        return carry

    for g in range(SSD_NG):
        group_body(g, 0)


def _ssd(l, xs4, bc, z4, tail, ssm0, ssmn_prev, w, *, nseq, L, sb, nchunks):
    has_state = ssm0 is not None
    dtb, alog, dsk, nw, wbr, bigsel, e8 = w
    ntok = tail.shape[0]
    R = sb * L
    if nchunks == 1:
        rblk = lambda b, c: b
    else:
        rblk = lambda b, c: b * nchunks + c
    in_specs = [
        pl.BlockSpec((SSD_NG, R, SSD_GW), lambda b, c: (0, rblk(b, c), 0)),
        pl.BlockSpec((2 * SSD_NG, R, SSD_NS), lambda b, c: (0, rblk(b, c), 0)),
        pl.BlockSpec((SSD_NG, R, SSD_GW), lambda b, c: (0, rblk(b, c), 0)),
        pl.BlockSpec((R, LANES), lambda b, c: (rblk(b, c), 0)),
    ]
    args = [xs4, bc, z4, tail]
    if has_state:
        in_specs.append(pl.BlockSpec((None, sb, SSD_NG, SSD_GW, SSD_NS), lambda b, c: (l, b, 0, 0, 0)))
        args.append(ssm0)
    in_specs += [
        pl.BlockSpec((None, 1, LANES), lambda b, c: (l, 0, 0)),
        pl.BlockSpec((None, 1, LANES), lambda b, c: (l, 0, 0)),
        pl.BlockSpec((None, SSD_NG, 1, SSD_GW), lambda b, c: (l, 0, 0, 0)),
        pl.BlockSpec((None, SSD_NG, 1, SSD_GW), lambda b, c: (l, 0, 0, 0)),
        pl.BlockSpec((None, SSD_NG, SSD_GW, D), lambda b, c: (l, 0, 0, 0)),
        pl.BlockSpec((SSD_NG, SSD_CS_PIECES * LANES, SSD_HPG * LANES), lambda b, c: (0, 0, 0)),
        pl.BlockSpec((SSD_NG, 2 * LANES, SSD_GW), lambda b, c: (0, 0, 0)),
    ]
    args += [dtb, alog, dsk, nw, wbr, bigsel, e8]
    aliases = {}
    _stacked_out(ssmn_prev, in_specs, args, aliases, 1)
    return pl.pallas_call(
        functools.partial(_ssd_kernel, L=L, sb=sb, nchunks=nchunks, has_state=has_state),
        grid=(nseq // sb, nchunks),
        in_specs=in_specs,
        out_specs=(
            pl.BlockSpec((R, D), lambda b, c: (rblk(b, c), 0)),
            pl.BlockSpec((None, sb, SSD_NG, SSD_GW, SSD_NS), lambda b, c: (l, b, 0, 0, 0)),
        ),
        out_shape=(
            jax.ShapeDtypeStruct((ntok, D), f32),
            jax.ShapeDtypeStruct((DEPTH, nseq, SSD_NG, SSD_GW, SSD_NS), f32),
        ),
        input_output_aliases=aliases,
        scratch_shapes=[pltpu.VMEM((LANES, R), f32)],
        compiler_params=pltpu.CompilerParams(
            dimension_semantics=("arbitrary", "arbitrary"), vmem_limit_bytes=VMEM_LIMIT),
        name="ssd",
    )(*args)


GLA_C2 = 4


def _gla_consts(L, C1, sb):
    R = sb * L
    nb2 = C1 // GLA_C2
    i = np.arange(R)[:, None]
    s = np.arange(R)[None, :]
    same_seq = (i // L) == (s // L)
    mats = [(s <= i) & (s > i - d) & same_seq for d in range(1, GLA_C2)]
    mats.append((s <= i) & (s >= (i // GLA_C2) * GLA_C2))
    for r in range(1, nb2):
        mats.append((s > i) & (s <= (i // C1) * C1 + GLA_C2 * r - 1))
    shifts = [(s == i - d) & same_seq for d in range(1, GLA_C2)]
    return (jnp.asarray(np.concatenate(mats, 0), bf16), jnp.asarray(np.concatenate(shifts, 0), bf16))


def _gla_kernel(*refs, L, C1, sb, nchunks, has_state):
    R = sb * L
    nb1 = L // C1
    nb2 = C1 // GLA_C2
    qkvr_ref, lg_ref = refs[:2]
    k = 2
    gla0_ref = None
    if has_state:
        gla0_ref = refs[k]
        k += 1
    gnw_ref, wbr_ref, sums_ref, shift_ref = refs[k:k + 4]
    br_ref, glan_ref = refs[-2:]
    c = pl.program_id(1)

    rowi = lax.broadcasted_iota(jnp.int32, (R, R), 0)
    coli = lax.broadcasted_iota(jnp.int32, (R, R), 1)
    shl = L.bit_length() - 1
    same_seq = (rowi >> shl) == (coli >> shl)
    tri = jnp.logical_and(rowi >= coli, same_seq).astype(bf16)
    sh1 = C1.bit_length() - 1
    same_blk1 = (rowi >> sh1) == (coli >> sh1)
    sub_i = (rowi & (C1 - 1)) >> (GLA_C2.bit_length() - 1)
    off_i = rowi & (GLA_C2 - 1)
    rowl = lax.broadcasted_iota(jnp.int32, (R, 1), 0)
    sub_l = (rowl & (C1 - 1)) >> (GLA_C2.bit_length() - 1)
    blk_i = (rowi & (L - 1)) >> sh1
    rel_l = rowl & (L - 1)
    pad_rows = [jnp.zeros((LANES - R, GLA_DK), bf16)] if R < LANES else []

    def init():
        if has_state:
            glan_ref[...] = gla0_ref[...]
        else:
            glan_ref[...] = jnp.zeros(glan_ref.shape, f32)

    if nchunks == 1:
        init()
    else:
        pl.when(c == 0)(init)

    def seq_last(x):
        if sb == 1:
            return x[R - 1:R, :]
        x3 = x.reshape(sb, L, x.shape[-1])
        return jnp.broadcast_to(x3[:, L - 1:L, :], x3.shape).reshape(x.shape)

    def rows2d(x):
        return x.reshape(R, x.shape[-1]) if x.ndim == 3 else x

    lg = rows2d(lg_ref[...])
    g = _cumsum_rows(lg, tri)
    g_end = seq_last(g)
    lg_hi, lg_lo = _split2(lg)
    if R % LANES == 0:
        sums = _dot(jnp.concatenate([sums_ref[...], sums_ref[...]], axis=1),
                    jnp.concatenate([lg_hi, lg_lo], axis=0))
    else:
        sums = _dot(sums_ref[...], lg_hi) + _dot(sums_ref[...], lg_lo)
    a_d = [sums[R * (d - 1):R * d] for d in range(1, GLA_C2)]
    a_sub = sums[R * (GLA_C2 - 1):R * GLA_C2]
    b_sub = [sums[R * (GLA_C2 - 1 + r):R * (GLA_C2 + r)] for r in range(1, nb2)]
    q = rows2d(qkvr_ref[..., 0:GLA_KD]).astype(f32)
    kb = rows2d(qkvr_ref[..., GLA_KD:2 * GLA_KD])
    k_ = kb.astype(f32)
    k_sh = [_dot(shift_ref[R * (d - 1):R * d, :], kb) for d in range(1, GLA_C2)]

    acc = jnp.zeros((R, D), f32)
    for h in range(GLA_NH):
        kl = slice(GLA_DK * h, GLA_DK * (h + 1))
        vl = slice(2 * GLA_KD + GLA_DV * h, 2 * GLA_KD + GLA_DV * (h + 1))
        rl = slice(2 * GLA_KD + GLA_VD + GLA_DV * h, 2 * GLA_KD + GLA_VD + GLA_DV * (h + 1))
        gh = g[:, kl]
        qh = q[:, kl]
        kh = k_[:, kl]
        vb = rows2d(qkvr_ref[..., vl])
        qg = qh * jnp.exp(gh)
        kd = kh * jnp.exp(g_end[:, kl] - gh)

        if sb == 1:
            st = glan_ref[0, h]
            o = _dot(qg.astype(bf16), st.astype(bf16))
            e_col = jnp.exp(jnp.broadcast_to(g_end[0:1, kl], (8, GLA_DK))).T[:, 0:1]
            glan_ref[0, h] = st * e_col + _dot_tn(kd.astype(bf16), vb)
        else:
            vf = vb.astype(f32)
            parts = []
            for s in range(sb):
                rs = slice(s * L, (s + 1) * L)
                st = glan_ref[s, h]
                parts.append(_dot(qg[rs].astype(bf16), st.astype(bf16)))
                e_col = jnp.exp(jnp.broadcast_to(g_end[s * L:s * L + 1, kl], (8, GLA_DK))).T[:, 0:1]
                glan_ref[s, h] = st * e_col + _dot_tn(kd[rs].astype(bf16), vf[rs].astype(bf16))
            o = jnp.concatenate(parts, axis=0)

        att = jnp.zeros((R, R), f32)
        if nb1 > 1:
            def bnd(s, i):
                if i == 0:
                    return jnp.zeros((1, GLA_DK), f32)
                return gh[s * L + C1 * i - 1:s * L + C1 * i, :]

            gblk = jnp.concatenate([jnp.broadcast_to(bnd(s, i), (C1, GLA_DK))
                                    for s in range(sb) for i in range(nb1)], axis=0)
            qt = (qh * jnp.exp(gh - gblk)).astype(bf16)
            kts = []
            for i in range(1, nb1):
                gb = jnp.concatenate([jnp.broadcast_to(bnd(s, i), (L, GLA_DK)) for s in range(sb)], axis=0)
                kt = kh * jnp.exp(jnp.where(rel_l < C1 * i, gb - gh, -jnp.inf))
                kts += [kt.astype(bf16)] + pad_rows
            out1 = _dot_nt(qt, jnp.concatenate(kts, axis=0))
            for i in range(1, nb1):
                att = att + jnp.where(jnp.logical_and(blk_i == i, same_seq),
                                      out1[:, LANES * (i - 1):LANES * (i - 1) + R], 0.0)

        qt2 = (qh * jnp.exp(a_sub[:, kl])).astype(bf16)
        kts = []
        for r in range(1, nb2):
            kt = kh * jnp.exp(jnp.where(sub_l < r, b_sub[r - 1][:, kl], -jnp.inf))
            kts += [kt.astype(bf16)] + pad_rows
        out2 = _dot_nt(qt2, jnp.concatenate(kts, axis=0))
        for r in range(1, nb2):
            att = att + jnp.where(jnp.logical_and(sub_i == r, same_blk1),
                                  out2[:, LANES * (r - 1):LANES * (r - 1) + R], 0.0)

        for d in range(GLA_C2):
            t = qh * kh if d == 0 else qh * jnp.exp(a_d[d - 1][:, kl]) * k_sh[d - 1][:, kl]
            band = jnp.sum(t, axis=-1, keepdims=True)
            hit = jnp.logical_and(off_i >= d, coli == rowi - d)
            att = att + jnp.where(hit, band, 0.0)
        o = o + _dot(att.astype(bf16), vb)

        on = o * lax.rsqrt(jnp.mean(o * o, axis=-1, keepdims=True) + EPS) * gnw_ref[...]
        og = on * rows2d(qkvr_ref[..., rl]).astype(f32)
        acc = acc + _dot(og.astype(bf16), wbr_ref[GLA_DV * h:GLA_DV * (h + 1), :])
    br_ref[...] = acc.reshape(br_ref.shape)


def _gla(l, qkvr, lg, gla0, glan_prev, w, *, nseq, L, C1, sb, nchunks):
    has_state = gla0 is not None
    gnw, wbr = w
    sums, shifts = _gla_consts(L, C1, sb)
    ntok = qkvr.shape[0]
    rb = sb * L
    wq = 2 * GLA_KD + 2 * GLA_VD
    if nchunks == 1:
        row = lambda b, c: (b, 0)
        blk = lambda w_: (rb, w_)
        view = lambda a: a
    else:
        row = lambda b, c: (b, c, 0)
        blk = lambda w_: (sb, L, w_)
        view = lambda a: a.reshape(nseq, ntok // nseq, a.shape[-1])
    in_specs = [
        pl.BlockSpec(blk(wq), row),
        pl.BlockSpec(blk(GLA_KD), row),
    ]
    args = [view(qkvr), view(lg)]
    if has_state:
        in_specs.append(pl.BlockSpec((None, sb, GLA_NH, GLA_DK, GLA_DV), lambda b, c: (l, b, 0, 0, 0)))
        args.append(gla0)
    in_specs += [
        pl.BlockSpec((None, 1, GLA_DV), lambda b, c: (l, 0, 0)),
        pl.BlockSpec((None, GLA_VD, D), lambda b, c: (l, 0, 0)),
        pl.BlockSpec(sums.shape, lambda b, c: (0, 0)),
        pl.BlockSpec(shifts.shape, lambda b, c: (0, 0)),
    ]
    args += [gnw, wbr, sums, shifts]
    aliases = {}
    _stacked_out(glan_prev, in_specs, args, aliases, 1)
    br_shape = (ntok, D) if nchunks == 1 else (nseq, ntok // nseq, D)
    br, glan = pl.pallas_call(
        functools.partial(_gla_kernel, L=L, C1=C1, sb=sb, nchunks=nchunks, has_state=has_state),
        grid=(nseq // sb, nchunks),
        in_specs=in_specs,
        out_specs=(
            pl.BlockSpec(blk(D), row),
            pl.BlockSpec((None, sb, GLA_NH, GLA_DK, GLA_DV), lambda b, c: (l, b, 0, 0, 0)),
        ),
        out_shape=(
            jax.ShapeDtypeStruct(br_shape, f32),
            jax.ShapeDtypeStruct((DEPTH, nseq, GLA_NH, GLA_DK, GLA_DV), f32),
        ),
        input_output_aliases=aliases,
        compiler_params=pltpu.CompilerParams(
            dimension_semantics=("arbitrary", "arbitrary"), vmem_limit_bytes=VMEM_LIMIT),
        name="gla",
    )(*args)
    return br.reshape(ntok, D), glan


def _top2_sum(a, b, c, d):
    hi1, lo1 = jnp.maximum(a, b), jnp.minimum(a, b)
    hi2, lo2 = jnp.maximum(c, d), jnp.minimum(c, d)
    return jnp.maximum(hi1, hi2) + jnp.maximum(jnp.minimum(hi1, hi2), jnp.maximum(lo1, lo2))


def _route_rows(sig, biased):
    gsc = [_top2_sum(*biased[EXP_PER_GRP * g:EXP_PER_GRP * (g + 1)]) for g in range(N_EGRP)]
    best = jnp.zeros_like(gsc[0], dtype=jnp.int32)
    m = gsc[0]
    for g in range(1, N_EGRP):
        better = gsc[g] > m
        best = jnp.where(better, g, best)
        m = jnp.where(better, gsc[g], m)
    masked = [jnp.where(best == (e // EXP_PER_GRP), biased[e], -jnp.inf) for e in range(N_EXP)]

    def first_argmax(vals):
        idx = jnp.zeros_like(best)
        mx = vals[0]
        for e in range(1, N_EXP):
            better = vals[e] > mx
            idx = jnp.where(better, e, idx)
            mx = jnp.where(better, vals[e], mx)
        return idx

    i1 = first_argmax(masked)
    i2 = first_argmax([jnp.where(i1 == e, -jnp.inf, masked[e]) for e in range(N_EXP)])
    w1 = sum(jnp.where(i1 == e, sig[e], 0.0) for e in range(N_EXP))
    w2 = sum(jnp.where(i2 == e, sig[e], 0.0) for e in range(N_EXP))
    den = w1 + w2
    return best, i1, i2, w1 / den, w2 / den


def _merge_kernel(*refs, seq3d, sparse):
    (x_ref, bra_ref, brb_ref, gate_ref, mod_ref, wo_ref, n2_ref, wr_hi_ref, wr_lo_ref, rb_ref) = refs[:10]
    mixed_in = (gate_ref[:, 0:D].astype(f32) * bra_ref[...]
                + gate_ref[:, D:2 * D].astype(f32) * brb_ref[...])
    mixed = _dot(mixed_in.astype(bf16), wo_ref[...])
    x = x_ref[...]
    if seq3d:
        x1 = x + mod_ref[:, 2:3, :] * mixed.reshape(x.shape)
    else:
        x1 = x + mod_ref[2:3, :] * mixed
    refs[10][...] = x1
    h2 = _norm_mod(x1, mod_ref, n2_ref, 3, 4, seq3d)
    hi, lo = _split2(h2)
    logits = _dot(hi, wr_hi_ref[...]) + (_dot(hi, wr_lo_ref[...]) + _dot(lo, wr_hi_ref[...]))
    tm = logits.shape[0]
    lt = logits.T
    sig_all = _sigmoid(lt[0:N_EXP, :])
    bias_all = sig_all + rb_ref[...]
    sig = [sig_all[e:e + 1, :] for e in range(N_EXP)]
    biased = [bias_all[e:e + 1, :] for e in range(N_EXP)]
    best, i1, i2, w1, w2 = _route_rows(sig, biased)
    if not sparse:
        h2_ref, comb_ref = refs[11:13]
        h2_ref[...] = h2.astype(bf16)
        comb = [jnp.where(i1 == e, w1, 0.0) + jnp.where(i2 == e, w2, 0.0) for e in range(N_EXP)]
        comb_t = jnp.concatenate(comb + [jnp.zeros((LANES - N_EXP, tm), f32)], axis=0)
        comb_ref[...] = comb_t.T
        return

    h2x_ref, plan_ref, counts_ref, base_scr = refs[11:15]
    swap = i1 > i2
    wa = jnp.where(swap, w2, w1)
    wb = jnp.where(swap, w1, w2)
    a = jnp.minimum(i1, i2) & (EXP_PER_GRP - 1)
    b = jnp.maximum(i1, i2) & (EXP_PER_GRP - 1)
    pair = jnp.where(a == 0, b - 1, jnp.where(a == 1, b + 1, 5))
    cls = best * N_PAIR + pair
    sub = lax.broadcasted_iota(jnp.int32, (CLS_PAD, tm), 0)
    onehot = (sub == cls).astype(f32)
    before = (lax.broadcasted_iota(jnp.int32, (tm, tm), 0)
              < lax.broadcasted_iota(jnp.int32, (tm, tm), 1)).astype(bf16)
    prefix = _dot(onehot.astype(bf16), before)

    @pl.when(pl.program_id(0) == 0)
    def _():
        base_scr[...] = jnp.zeros_like(base_scr)

    base = base_scr[...]
    rank = jnp.sum(onehot * (prefix + base[:, 0:1]), axis=0, keepdims=True)
    base = base + jnp.sum(onehot, axis=1, keepdims=True)
    base_scr[...] = base
    counts_ref[...] = base
    plan_ref[...] = jnp.concatenate(
        [cls, rank.astype(jnp.int32), jnp.zeros((6, tm), jnp.int32)], axis=0)
    ext_t = jnp.concatenate([wa, wb, jnp.zeros((MOE_EXT - 2, tm), f32)], axis=0)
    h2x_ref[:, 0:D] = h2
    h2x_ref[:, D:D + MOE_EXT] = ext_t.T


def _merge(l, x, bra, brb, gate, mod, wo, n2w, wr_hi, wr_lo, rb, *, seq3d, sparse):
    if seq3d:
        ns = _seq_tile(x.shape[0], 32)
        ntok = x.shape[0] * x.shape[1]
        tm = ns * x.shape[1]
        nt = x.shape[0] // ns
        x_spec = pl.BlockSpec((ns, x.shape[1], D), lambda i: (i, 0, 0))
        mod_spec = pl.BlockSpec((None, ns, 6, D), lambda i: (l, i, 0, 0))
    else:
        ntok = x.shape[0]
        nseq = mod.shape[1]
        tm = min(512, ntok // nseq)
        nt = ntok // tm
        per_seq = ntok // nseq // tm
        x_spec = pl.BlockSpec((tm, D), lambda i: (i, 0))
        mod_spec = pl.BlockSpec((None, None, 6, D), lambda i: (l, i // per_seq, 0, 0))
    row = lambda i: (i, 0)
    const = lambda i: (0, 0)
    if sparse:
        out_specs = (x_spec, pl.BlockSpec((tm, D + MOE_EXT), row), pl.BlockSpec((8, tm), lambda i: (0, i)),
                     pl.BlockSpec((CLS_PAD, LANES), const))
        out_shape = (jax.ShapeDtypeStruct(x.shape, f32), jax.ShapeDtypeStruct((ntok, D + MOE_EXT), f32),
                     jax.ShapeDtypeStruct((8, ntok), jnp.int32), jax.ShapeDtypeStruct((CLS_PAD, LANES), f32))
        scratch = [pltpu.VMEM((CLS_PAD, LANES), f32)]
    else:
        out_specs = (x_spec, pl.BlockSpec((tm, D), row), pl.BlockSpec((tm, LANES), row))
        out_shape = (jax.ShapeDtypeStruct(x.shape, f32), jax.ShapeDtypeStruct((ntok, D), bf16),
                     jax.ShapeDtypeStruct((ntok, LANES), f32))
        scratch = []
    return pl.pallas_call(
        functools.partial(_merge_kernel, seq3d=seq3d, sparse=sparse),
        grid=(nt,),
        in_specs=[
            x_spec,
            pl.BlockSpec((tm, D), row),
            pl.BlockSpec((tm, D), row),
            pl.BlockSpec((tm, 2 * D), row),
            mod_spec,
            pl.BlockSpec((None, D, D), lambda i: (l, 0, 0)),
            pl.BlockSpec((None, 1, D), lambda i: (l, 0, 0)),
            pl.BlockSpec((D, LANES), const),
            pl.BlockSpec((D, LANES), const),
            pl.BlockSpec((N_EXP, 1), const),
        ],
        out_specs=out_specs,
        out_shape=out_shape,
        scratch_shapes=scratch,
        compiler_params=pltpu.CompilerParams(
            dimension_semantics=("arbitrary",), vmem_limit_bytes=VMEM_LIMIT),
        name="merge",
    )(x, bra, brb, gate, mod, wo, n2w, wr_hi, wr_lo, rb)


def _dispatch_kernel(pos_ref, src_ref, buf_in, xs_hbm, sem):
    del buf_in
    tm = src_ref.shape[0]

    def issue(t, c):
        p = pos_ref[0, t]
        pltpu.make_async_copy(src_ref.at[pl.ds(t, 1), :], xs_hbm.at[pl.ds(p, 1), :], sem).start()
        return c

    lax.fori_loop(0, tm, issue, 0, unroll=MOE_DMA_UNROLL)

    def drain(t, c):
        pltpu.make_async_copy(src_ref.at[pl.ds(0, 1), :], xs_hbm.at[pl.ds(0, 1), :], sem).wait()
        return c

    lax.fori_loop(0, tm, drain, 0, unroll=MOE_DMA_UNROLL)


def _dispatch(pos3, h2x, xs_buf):
    nt, _, tm = pos3.shape
    return pl.pallas_call(
        _dispatch_kernel,
        grid=(nt,),
        in_specs=[
            pl.BlockSpec((None, 1, tm), lambda i: (i, 0, 0), memory_space=pltpu.SMEM),
            pl.BlockSpec((tm, D + MOE_EXT), lambda i: (i, 0)),
            pl.BlockSpec(memory_space=pl.ANY),
        ],
        out_specs=pl.BlockSpec(memory_space=pl.ANY),
        out_shape=jax.ShapeDtypeStruct(xs_buf.shape, f32),
        input_output_aliases={2: 0},
        scratch_shapes=[pltpu.SemaphoreType.DMA(())],
        compiler_params=pltpu.CompilerParams(dimension_semantics=("arbitrary",)),
        name="dispatch",
    )(pos3, h2x, xs_buf)


def _moe_sparse_kernel(src_ref, valid_ref, ea_ref, eb_ref, xs_ref,
                       wga_ref, wua_ref, wda_ref, wgb_ref, wub_ref, wdb_ref, ys_ref):
    del src_ref, ea_ref, eb_ref
    t = pl.program_id(0)

    @pl.when(valid_ref[t] == 1)
    def _():
        x = xs_ref[...]
        h = x[:, 0:D].astype(bf16)

        def ffn(wg, wu, wd):
            u = (_silu(_dot(h, wg[...])) * _dot(h, wu[...])).astype(bf16)
            return _dot(u, wd[...])

        ys_ref[...] = (x[:, D:D + 1] * ffn(wga_ref, wua_ref, wda_ref)
                       + x[:, D + 1:D + 2] * ffn(wgb_ref, wub_ref, wdb_ref))

    @pl.when(valid_ref[pl.program_id(0)] == 0)
    def _():
        ys_ref[...] = jnp.zeros(ys_ref.shape, f32)


def _moe_sparse(tiles, xs, wg, wu, wd):
    src, valid, ea, eb = tiles
    nrow = xs.shape[0]
    ntile = nrow // MOE_R
    wa_map = lambda t, src, valid, ea, eb: (ea[t], 0, 0)
    wb_map = lambda t, src, valid, ea, eb: (eb[t], 0, 0)
    row_map = lambda t, src, valid, ea, eb: (src[t], 0)
    return pl.pallas_call(
        _moe_sparse_kernel,
        grid_spec=pltpu.PrefetchScalarGridSpec(
            num_scalar_prefetch=4,
            grid=(ntile,),
            in_specs=[
                pl.BlockSpec((MOE_R, D + MOE_EXT), row_map),
                pl.BlockSpec((None, D, EXP_FF), wa_map),
                pl.BlockSpec((None, D, EXP_FF), wa_map),
                pl.BlockSpec((None, EXP_FF, D), wa_map),
                pl.BlockSpec((None, D, EXP_FF), wb_map),
                pl.BlockSpec((None, D, EXP_FF), wb_map),
                pl.BlockSpec((None, EXP_FF, D), wb_map),
            ],
            out_specs=pl.BlockSpec((MOE_R, D), lambda t, src, valid, ea, eb: (t, 0)),
        ),
        out_shape=jax.ShapeDtypeStruct((nrow, D), f32),
        compiler_params=pltpu.CompilerParams(
            dimension_semantics=("arbitrary",), vmem_limit_bytes=VMEM_LIMIT),
        name="moe_sparse",
    )(src, valid, ea, eb, xs, wg, wu, wd, wg, wu, wd)


def _combine_kernel(pos_ref, ys_hbm, x1_ref, mod_ref, fw_ref, x2_ref, buf, sem, *, final):
    tm = x1_ref.shape[0]

    def issue(t, c):
        p = pos_ref[0, t]
        pltpu.make_async_copy(ys_hbm.at[pl.ds(p, 1), :], buf.at[pl.ds(t, 1), :], sem).start()
        return c

    lax.fori_loop(0, tm, issue, 0, unroll=MOE_DMA_UNROLL)

    def drain(t, c):
        pltpu.make_async_copy(ys_hbm.at[pl.ds(0, 1), :], buf.at[pl.ds(0, 1), :], sem).wait()
        return c

    lax.fori_loop(0, tm, drain, 0, unroll=MOE_DMA_UNROLL)
    x2 = x1_ref[...] + mod_ref[5:6, :] * buf[...]
    if final:
        x2 = x2 * lax.rsqrt(jnp.mean(x2 * x2, axis=-1, keepdims=True) + EPS) * fw_ref[...]
    x2_ref[...] = x2


def _combine(l, pos3, ys, x1, mod, fw, *, final):
    nt, _, tm = pos3.shape
    ntok = x1.shape[0]
    per_seq = ntok // mod.shape[1] // tm
    return pl.pallas_call(
        functools.partial(_combine_kernel, final=final),
        grid=(nt,),
        in_specs=[
            pl.BlockSpec((None, 1, tm), lambda i: (i, 0, 0), memory_space=pltpu.SMEM),
            pl.BlockSpec(memory_space=pl.ANY),
            pl.BlockSpec((tm, D), lambda i: (i, 0)),
            pl.BlockSpec((None, None, 6, D), lambda i: (l, i // per_seq, 0, 0)),
            pl.BlockSpec((1, D), lambda i: (0, 0)),
        ],
        out_specs=pl.BlockSpec((tm, D), lambda i: (i, 0)),
        out_shape=jax.ShapeDtypeStruct((ntok, D), f32),
        scratch_shapes=[pltpu.VMEM((tm, D), f32), pltpu.SemaphoreType.DMA(())],
        compiler_params=pltpu.CompilerParams(dimension_semantics=("arbitrary",)),
        name="combine",
    )(pos3, ys, x1, mod, fw)


def _moe_plan(plan, counts, ntile):
    cnt = counts[:N_CLS, 0].astype(jnp.int32)
    tiles_per = (cnt + (MOE_R - 1)) // MOE_R
    tstart = jnp.cumsum(tiles_per) - tiles_per
    total = jnp.sum(tiles_per)
    pos = tstart[plan[0]] * MOE_R + plan[1]
    t = jnp.arange(ntile, dtype=jnp.int32)
    src = jnp.minimum(t, total - 1)
    tcls = jnp.sum((src[:, None] >= tstart[None, :]).astype(jnp.int32), axis=1) - 1
    grp = tcls // N_PAIR
    pair = tcls % N_PAIR
    pa = jnp.asarray(np.array([0, 0, 0, 1, 1, 2], np.int32))
    pb = jnp.asarray(np.array([1, 2, 3, 2, 3, 3], np.int32))
    ea = grp * EXP_PER_GRP + pa[pair]
    eb = grp * EXP_PER_GRP + pb[pair]
    valid = (t < total).astype(jnp.int32)
    return pos, (src, valid, ea, eb)


def _moe_kernel(h2_ref, comb_ref, wg_ref, wu_ref, wd_ref, x1_ref, mod_ref,
                x2_ref, wgb_ref, wub_ref, wdb_ref, acc_scr, *, seq3d):
    e = pl.program_id(1)

    @pl.when(e == 0)
    def _():
        acc_scr[...] = jnp.zeros_like(acc_scr)

    wgb_ref[...] = wg_ref[...].astype(bf16)
    wub_ref[...] = wu_ref[...].astype(bf16)
    wdb_ref[...] = wd_ref[...].astype(bf16)
    h = h2_ref[...]
    a = _dot(h, wgb_ref[...])
    b = _dot(h, wub_ref[...])
    t = (_silu(a) * b).astype(bf16)
    ye = _dot(t, wdb_ref[...])
    lane = lax.broadcasted_iota(jnp.int32, comb_ref.shape, 1)
    w = jnp.sum(jnp.where(lane == e, comb_ref[...], 0.0), axis=-1, keepdims=True)
    acc_scr[...] += w * ye

    @pl.when(e == N_EXP - 1)
    def _():
        x1 = x1_ref[...]
        if seq3d:
            x2_ref[...] = x1 + mod_ref[:, 5:6, :] * acc_scr[...].reshape(x1.shape)
        else:
            x2_ref[...] = x1 + mod_ref[5:6, :] * acc_scr[...]


def _moe(l, h2, comb, wg, wu, wd, x1, mod, *, seq3d):
    if seq3d:
        ns = x1.shape[0]
        ntok = ns * x1.shape[1]
        tm = ntok
        nt = 1
        x_spec = pl.BlockSpec((ns, x1.shape[1], D), lambda i, e: (0, 0, 0))
        mod_spec = pl.BlockSpec((None, ns, 6, D), lambda i, e: (l, 0, 0, 0))
    else:
        ntok = x1.shape[0]
        nseq = mod.shape[1]
        tm = min(1024, ntok // nseq)
        nt = ntok // tm
        per_seq = ntok // nseq // tm
        x_spec = pl.BlockSpec((tm, D), lambda i, e: (i, 0))
        mod_spec = pl.BlockSpec((None, None, 6, D), lambda i, e: (l, i // per_seq, 0, 0))
    assert nt == 1, "the bf16 weight copies are written once per expert"
    row = lambda i, e: (i, 0)
    return pl.pallas_call(
        functools.partial(_moe_kernel, seq3d=seq3d),
        grid=(nt, N_EXP),
        in_specs=[
            pl.BlockSpec((tm, D), row),
            pl.BlockSpec((tm, LANES), row),
            pl.BlockSpec((None, None, D, EXP_FF), lambda i, e: (l, e, 0, 0)),
            pl.BlockSpec((None, None, D, EXP_FF), lambda i, e: (l, e, 0, 0)),
            pl.BlockSpec((None, None, EXP_FF, D), lambda i, e: (l, e, 0, 0)),
            x_spec,
            mod_spec,
        ],
        out_specs=(
            x_spec,
            pl.BlockSpec((None, D, EXP_FF), lambda i, e: (e, 0, 0)),
            pl.BlockSpec((None, D, EXP_FF), lambda i, e: (e, 0, 0)),
            pl.BlockSpec((None, EXP_FF, D), lambda i, e: (e, 0, 0)),
        ),
        out_shape=(
            jax.ShapeDtypeStruct(x1.shape, f32),
            jax.ShapeDtypeStruct((N_EXP, D, EXP_FF), bf16),
            jax.ShapeDtypeStruct((N_EXP, D, EXP_FF), bf16),
            jax.ShapeDtypeStruct((N_EXP, EXP_FF, D), bf16),
        ),
        scratch_shapes=[pltpu.VMEM((tm, D), f32)],
        compiler_params=pltpu.CompilerParams(
            dimension_semantics=("arbitrary", "arbitrary"), vmem_limit_bytes=VMEM_LIMIT),
        name="moe",
    )(h2, comb, wg, wu, wd, x1, mod)


def _final_norm_kernel(x_ref, w_ref, o_ref):
    x = x_ref[...]
    o_ref[...] = x * lax.rsqrt(jnp.mean(x * x, axis=-1, keepdims=True) + EPS) * w_ref[...]


def _final_norm(x2d, w):
    ntok = x2d.shape[0]
    tm = min(1024, ntok)
    return pl.pallas_call(
        _final_norm_kernel,
        grid=(ntok // tm,),
        in_specs=[pl.BlockSpec((tm, D), lambda i: (i, 0)), pl.BlockSpec((1, D), lambda i: (0, 0))],
        out_specs=pl.BlockSpec((tm, D), lambda i: (i, 0)),
        out_shape=jax.ShapeDtypeStruct((ntok, D), f32),
        compiler_params=pltpu.CompilerParams(dimension_semantics=("arbitrary",)),
        name="final_norm",
    )(x2d, w)


def _prep_in_weights(w_in):
    o = 0
    z = w_in[:, :, o:o + SSD_INNER]; o += SSD_INNER
    xbc = w_in[:, :, o:o + CONV_DIM]; o += CONV_DIM
    dt = w_in[:, :, o:o + SSD_NH]; o += SSD_NH
    q = w_in[:, :, o:o + GLA_KD] * (GLA_DK ** -0.5); o += GLA_KD
    kvr = w_in[:, :, o:o + GLA_KD + 2 * GLA_VD]; o += GLA_KD + 2 * GLA_VD
    glr = w_in[:, :, o:o + GLA_RANK]; o += GLA_RANK
    gates = w_in[:, :, o:o + 2 * D]
    main = jnp.concatenate([z, xbc, q, kvr, gates], axis=-1).astype(bf16)
    pad = jnp.zeros(w_in.shape[:2] + (LANES - SSD_NH - GLA_RANK,), w_in.dtype)
    tail = jnp.concatenate([dt, glr, pad], axis=-1).astype(bf16)
    return main, tail


def _pad_lanes(v, width=LANES):
    return jnp.concatenate([v, jnp.zeros(v.shape[:-1] + (width - v.shape[-1],), v.dtype)], axis=-1)


def kernel(x_prompt, x_sample, c_prompt, c_sample, state_conv, state_ssm, state_gla, w_ada, b_ada, norm1_w, w_in, conv_w, conv_b, dt_bias, a_log, d_skip, ssd_norm_w, w_ssd_br, gla_gate_up, gla_gate_b, gla_norm_w, w_gla_br, merge_b, w_out, norm2_w, w_router, router_bias, w_exp_gate, w_exp_up, w_exp_down, final_norm_w):
    bp, sp, _ = x_prompt.shape
    bs, ss, _ = x_sample.shape

    mod = _adaln(jnp.concatenate([c_prompt, c_sample], axis=0), w_ada, b_ada)
    mod_p = mod[:, :bp].reshape(DEPTH, bp, 6, D)
    mod_s = mod[:, bp:].reshape(DEPTH, bs, 6, D)

    w_main, w_tail = _prep_in_weights(w_in)
    gate_up = jnp.concatenate(
        [jnp.zeros((DEPTH, TAIL_GLR0, GLA_KD), f32), gla_gate_up,
         jnp.zeros((DEPTH, LANES - TAIL_GLR0 - GLA_RANK, GLA_KD), f32)], axis=1).astype(bf16)
    in_w = (norm1_w.reshape(DEPTH, 1, D), w_main, w_tail, gate_up, gla_gate_b.reshape(DEPTH, 1, GLA_KD),
            conv_w, conv_b.reshape(DEPTH, 1, CONV_DIM), merge_b.reshape(DEPTH, 1, 2 * D))
    bigsel, e8 = _ssd_consts()
    ssd_w = (_pad_lanes(dt_bias).reshape(DEPTH, 1, LANES), _pad_lanes(a_log).reshape(DEPTH, 1, LANES),
             jnp.repeat(d_skip, SSD_HD, axis=-1).reshape(DEPTH, SSD_NG, 1, SSD_GW),
             ssd_norm_w.reshape(DEPTH, SSD_NG, 1, SSD_GW),
             w_ssd_br.astype(bf16).reshape(DEPTH, SSD_NG, SSD_GW, D), bigsel, e8)
    gla_w = (gla_norm_w.reshape(DEPTH, 1, GLA_DV), w_gla_br.astype(bf16))
    w_out_b = w_out.astype(bf16)
    n2 = norm2_w.reshape(DEPTH, 1, D)
    wr = _pad_lanes(w_router)
    wr_hi = wr.astype(bf16)
    wr_lo = (wr - wr_hi.astype(f32)).astype(bf16)
    rb = router_bias.reshape(N_EXP, 1)

    xp = x_prompt.reshape(bp * sp, D)
    xs = x_sample
    ssm_s = state_ssm.reshape(DEPTH, bs, SSD_NG, SSD_GW, SSD_NS)
    ssd_lp = min(128, sp)
    gla_lp = min(64, sp)
    cp = sp_ = gp = cs_ = ss_ = gs = None
    moe_tm = min(MOE_TM, sp)
    moe_tiles = -(-(bp * sp) // MOE_R) + N_CLS
    xs_rows = jnp.zeros((moe_tiles * MOE_R, D + MOE_EXT), f32)
    fw = final_norm_w.reshape(1, D)
    for l in range(DEPTH):
        z4, xs4, bc, qkvr, gate, tail, lg, cs_ = _inproj(l, xs, mod_s, state_conv, cs_, in_w, seq3d=True)
        bra, ss_ = _ssd(l, xs4, bc, z4, tail, ssm_s, ss_, ssd_w, nseq=bs, L=ss, sb=8, nchunks=1)
        brb, gs = _gla(l, qkvr, lg, state_gla, gs, gla_w, nseq=bs, L=ss, C1=ss, sb=8, nchunks=1)
        x1, h2, comb = _merge(l, xs, bra, brb, gate, mod_s, w_out_b, n2, wr_hi, wr_lo, rb,
                              seq3d=True, sparse=False)
        xs, wg_b, wu_b, wd_b = _moe(l, h2, comb, w_exp_gate, w_exp_up, w_exp_down, x1, mod_s, seq3d=True)

        z4, xs4, bc, qkvr, gate, tail, lg, cp = _inproj(l, xp, mod_p, None, cp, in_w, seq3d=False)
        bra, sp_ = _ssd(l, xs4, bc, z4, tail, None, sp_, ssd_w, nseq=bp, L=ssd_lp, sb=1, nchunks=sp // ssd_lp)
        brb, gp = _gla(l, qkvr, lg, None, gp, gla_w, nseq=bp, L=gla_lp, C1=16,
                       sb=2 if bp % 2 == 0 else 1, nchunks=sp // gla_lp)
        x1, h2x, plan, counts = _merge(l, xp, bra, brb, gate, mod_p, w_out_b, n2, wr_hi, wr_lo, rb,
                                       seq3d=False, sparse=True)
        pos, tiles = _moe_plan(plan, counts, moe_tiles)
        pos3 = pos.reshape(bp * sp // moe_tm, 1, moe_tm)
        xs_rows = _dispatch(pos3, h2x, xs_rows)
        ys_rows = _moe_sparse(tiles, xs_rows, wg_b, wu_b, wd_b)
        xp = _combine(l, pos3, ys_rows, x1, mod_p, fw, final=(l == DEPTH - 1))

    y_prompt = xp.reshape(bp, sp, D)
    y_sample = _final_norm(xs.reshape(bs * ss, D), fw).reshape(bs, ss, D)
    return (y_prompt, y_sample, cp, sp_.reshape(DEPTH, bp, SSD_NH, SSD_HD, SSD_NS), gp,
            cs_, ss_.reshape(DEPTH, bs, SSD_NH, SSD_HD, SSD_NS), gs)
```

```python
import functools

import numpy as np
import jax
import jax.numpy as jnp
from jax import lax
from jax.experimental import pallas as pl
from jax.experimental.pallas import tpu as pltpu

f32 = jnp.float32
bf16 = jnp.bfloat16

D = 1024
DEPTH = 4
SSD_INNER = 2048
SSD_HD = 64
SSD_NH = 32
SSD_NS = 128
SSD_NG = 4
SSD_HPG = 8
SSD_GW = SSD_INNER // SSD_NG
CONV_K = 4
CONV_DIM = SSD_INNER + 2 * SSD_NG * SSD_NS
GLA_NH = 4
GLA_DK = 128
GLA_DV = 256
GLA_KD = GLA_NH * GLA_DK
GLA_VD = GLA_NH * GLA_DV
GLA_RANK = 16
GLA_TAU = 16.0
N_EXP = 16
N_EGRP = 4
EXP_PER_GRP = 4
EXP_FF = 512
N_PAIR = 6
N_CLS = N_EGRP * N_PAIR
CLS_PAD = 32
MOE_EXT = 128
MOE_R = 256
MOE_TM = 512
MOE_DMA_UNROLL = 32
EPS = 1e-6
LANES = 128
TAIL_GLR0 = SSD_NH
MAIN_W = SSD_INNER + CONV_DIM + 2 * GLA_KD + 2 * GLA_VD + 2 * D
VMEM_LIMIT = 56 * 1024 * 1024


def _sigmoid(x):
    return 1.0 / (1.0 + jnp.exp(-x))


def _silu(x):
    return x * _sigmoid(x)


def _softplus(x):
    return jnp.maximum(x, 0.0) + jnp.log1p(jnp.exp(-jnp.abs(x)))


def _log_sigmoid(x):
    return jnp.minimum(x, 0.0) - jnp.log1p(jnp.exp(-jnp.abs(x)))


def _dot(a, b):
    return jnp.dot(a, b, preferred_element_type=f32)


def _dot_nt(a, b):
    return lax.dot_general(a, b, (((1,), (1,)), ((), ())), preferred_element_type=f32)


def _dot_tn(a, b):
    return lax.dot_general(a, b, (((0,), (0,)), ((), ())), preferred_element_type=f32)


def _split2(x):
    hi = x.astype(bf16)
    return hi, (x - hi.astype(f32)).astype(bf16)


def _split3(x):
    hi = x.astype(bf16)
    r1 = x - hi.astype(f32)
    mid = r1.astype(bf16)
    lo = (r1 - mid.astype(f32)).astype(bf16)
    return hi, mid, lo


def _cumsum_rows(x, tri):
    hi, mid, lo = _split3(x)
    if tri.shape[1] % LANES == 0:
        return _dot(jnp.concatenate([tri, tri, tri], axis=1), jnp.concatenate([hi, mid, lo], axis=0))
    return _dot(tri, hi) + _dot(tri, mid) + _dot(tri, lo)


def _tri(n):
    r = lax.broadcasted_iota(jnp.int32, (n, n), 0)
    c = lax.broadcasted_iota(jnp.int32, (n, n), 1)
    return r >= c


def _norm_mod(x, mod_ref, w_ref, i_shift, i_scale, seq3d):
    ms = jnp.mean(x * x, axis=-1, keepdims=True)
    y = x * lax.rsqrt(ms + EPS) * w_ref[...]
    if seq3d:
        sc = mod_ref[:, i_scale:i_scale + 1, :]
        sh = mod_ref[:, i_shift:i_shift + 1, :]
        h = y * (1.0 + sc) + sh
        return h.reshape(h.shape[0] * h.shape[1], h.shape[2])
    sc = mod_ref[i_scale:i_scale + 1, :]
    sh = mod_ref[i_shift:i_shift + 1, :]
    return y * (1.0 + sc) + sh


def _seq_tile(nseq, want):
    return want if nseq % want == 0 else nseq


def _stacked_out(prev, in_specs, args, aliases, out_index):
    if prev is not None:
        in_specs.append(pl.BlockSpec(memory_space=pl.ANY))
        args.append(prev)
        aliases[len(args) - 1] = out_index


def _adaln_kernel(c_ref, w_ref, b_ref, o_ref):
    s = _silu(c_ref[...]).astype(bf16)
    o_ref[...] = _dot(s, w_ref[...].astype(bf16)) + b_ref[...]


def _adaln(c_all, w_ada, b_ada):
    n = c_all.shape[0]
    tn = 1024
    return pl.pallas_call(
        _adaln_kernel,
        grid=(DEPTH, 6 * D // tn),
        in_specs=[
            pl.BlockSpec((n, D), lambda l, j: (0, 0)),
            pl.BlockSpec((None, D, tn), lambda l, j: (l, 0, j)),
            pl.BlockSpec((None, 1, tn), lambda l, j: (l, 0, j)),
        ],
        out_specs=pl.BlockSpec((None, n, tn), lambda l, j: (l, 0, j)),
        out_shape=jax.ShapeDtypeStruct((DEPTH, n, 6 * D), f32),
        compiler_params=pltpu.CompilerParams(
            dimension_semantics=("arbitrary", "arbitrary"), vmem_limit_bytes=VMEM_LIMIT),
        name="adaln",
    )(c_all, w_ada, b_ada.reshape(DEPTH, 1, 6 * D))


IN_TN = 1024
IN_NJ = MAIN_W // IN_TN
IN_SUB = 256
IN_NSUB = IN_TN // IN_SUB
CONV_PAD = 8
KPREV = CONV_K - 1


def _inproj_kernel(*refs, seq3d, has_state, per_seq, T):
    if has_state:
        (x_ref, mod_ref, n1_ref, wm_ref, wt_ref, wg_ref, gb_ref, cw_ref, cb_ref, mb_ref, conv0_ref) = refs[:11]
        rest = refs[11:]
    else:
        (x_ref, mod_ref, n1_ref, wm_ref, wt_ref, wg_ref, gb_ref, cw_ref, cb_ref, mb_ref) = refs[:10]
        conv0_ref = None
        rest = refs[10:]
    (z4_ref, xs4_ref, bc_ref, qkvr_ref, gate_ref, tail_ref, lg_ref, convn_ref,
     h_scr, cscr, carry) = rest[-11:]
    i = pl.program_id(0)
    j = pl.program_id(1)
    tm = h_scr.shape[0]
    ns = cscr.shape[1]

    @pl.when(j == 0)
    def _():
        h = _norm_mod(x_ref[...], mod_ref, n1_ref, 0, 1, seq3d).astype(bf16)
        h_scr[...] = h
        t = _dot(h, wt_ref[...])
        tail_ref[...] = t
        pre = _dot(t.astype(bf16), wg_ref[...]) + gb_ref[...]
        lg_ref[...] = _log_sigmoid(pre) * (1.0 / GLA_TAU)

    def sub_dot(c):
        return _dot(h_scr[...], wm_ref[:, IN_SUB * c:IN_SUB * (c + 1)])

    def conv_silu(a, cj, c):
        cols = slice(IN_SUB * c, IN_SUB * (c + 1))
        gcols = slice(IN_TN * cj + IN_SUB * c, IN_TN * cj + IN_SUB * (c + 1))
        if not has_state:
            prev = jnp.where(i % per_seq == 0, 0.0, carry[cj, c, CONV_PAD - KPREV:CONV_PAD, :])
            cscr[c, 0, CONV_PAD - KPREV:CONV_PAD, :] = prev
            cscr[c, 0, CONV_PAD:2 * CONV_PAD, :] = a[0:CONV_PAD]
            cscr[c, 0, 2 * CONV_PAD:3 * CONV_PAD, :] = a[tm - CONV_PAD:tm]
            w = [cw_ref[k:k + 1, cols] for k in range(CONV_K)]
            out = cb_ref[:, cols] + a * w[KPREV]
            head = cb_ref[:, cols] + a[0:CONV_PAD] * w[KPREV]
            for k in range(KPREV):
                out = out + pltpu.roll(a, KPREV - k, 0) * w[k]
                head = head + cscr[c, 0, CONV_PAD - KPREV + k:2 * CONV_PAD - KPREV + k, :] * w[k]
            out = jnp.concatenate([head, out[CONV_PAD:]], axis=0)
            new_tail = cscr[c, 0, 3 * CONV_PAD - KPREV:3 * CONV_PAD, :]
            convn_ref[0, :, gcols] = new_tail
            carry[cj, c, CONV_PAD - KPREV:CONV_PAD, :] = new_tail
            return _silu(out).astype(bf16)
        a3 = a.reshape(ns, T, IN_SUB)
        prev = conv0_ref[:, :, cols]
        cscr[c, :, CONV_PAD - KPREV:CONV_PAD, :] = prev
        cscr[c, :, CONV_PAD:CONV_PAD + T, :] = a3
        out = cb_ref[:, cols] + a3 * cw_ref[KPREV:KPREV + 1, cols]
        for k in range(KPREV):
            out = out + cscr[c, :, CONV_PAD - KPREV + k:CONV_PAD - KPREV + k + T, :] * cw_ref[k:k + 1, cols]
        new_tail = cscr[c, :, CONV_PAD + T - KPREV:CONV_PAD + T, :]
        convn_ref[:, :, gcols] = new_tail
        return _silu(out).reshape(tm, IN_SUB).astype(bf16)

    per_grp = SSD_GW // IN_SUB

    @pl.when(j < 2)
    def _():
        for c in range(IN_NSUB):
            lanes = slice(IN_SUB * (c % per_grp), IN_SUB * (c % per_grp + 1))
            z4_ref[c // per_grp, :, lanes] = _silu(sub_dot(c)).astype(bf16)

    for cj in range(2):
        @pl.when(j == 2 + cj)
        def _(cj=cj):
            for c in range(IN_NSUB):
                lanes = slice(IN_SUB * (c % per_grp), IN_SUB * (c % per_grp + 1))
                xs4_ref[c // per_grp, :, lanes] = conv_silu(sub_dot(c), cj, c)

    @pl.when(j == 4)
    def _():
        for c in range(IN_NSUB):
            xc = conv_silu(sub_dot(c), 2, c)
            for q in range(IN_SUB // SSD_NS):
                bc_ref[c * (IN_SUB // SSD_NS) + q] = xc[:, SSD_NS * q:SSD_NS * (q + 1)]

    @pl.when(jnp.logical_and(j >= 5, j < 7))
    def _():
        for c in range(IN_NSUB):
            qkvr_ref[:, IN_SUB * c:IN_SUB * (c + 1)] = sub_dot(c).astype(bf16)

    @pl.when(j == 7)
    def _():
        for c in range(IN_NSUB):
            qkvr_ref[:, IN_SUB * c:IN_SUB * (c + 1)] = _silu(sub_dot(c)).astype(bf16)

    @pl.when(j >= 8)
    def _():
        for c in range(IN_NSUB):
            cols = slice(IN_SUB * c, IN_SUB * (c + 1))
            gate_ref[:, cols] = _sigmoid(sub_dot(c) + mb_ref[:, cols]).astype(bf16)


def _inproj(l, x, mod, conv0, convn_prev, w, *, seq3d):
    (n1w, w_main, w_tail, gate_up, gate_b, conv_w, conv_b, merge_b) = w
    has_state = conv0 is not None
    if seq3d:
        nseq, T = x.shape[0], x.shape[1]
        ns = _seq_tile(nseq, 128)
        ntok = nseq * T
        tm = ns * T
        nt = nseq // ns
        per_seq = 1
        x_spec = pl.BlockSpec((ns, T, D), lambda i, j: (i, 0, 0))
        mod_spec = pl.BlockSpec((None, ns, 6, D), lambda i, j: (l, i, 0, 0))
        seq_blk = lambda i: i
    else:
        nseq = mod.shape[1]
        ntok = x.shape[0]
        tm = min(1024, ntok // nseq)
        T = tm
        ns = 1
        nt = ntok // tm
        per_seq = ntok // nseq // tm
        x_spec = pl.BlockSpec((tm, D), lambda i, j: (i, 0))
        mod_spec = pl.BlockSpec((None, None, 6, D), lambda i, j: (l, i // per_seq, 0, 0))
        seq_blk = lambda i: i // per_seq

    def cj(j):
        return jnp.clip(j - 2, 0, 2)

    in_specs = [
        x_spec,
        mod_spec,
        pl.BlockSpec((None, 1, D), lambda i, j: (l, 0, 0)),
        pl.BlockSpec((None, D, IN_TN), lambda i, j: (l, 0, j)),
        pl.BlockSpec((None, D, LANES), lambda i, j: (l, 0, 0)),
        pl.BlockSpec((None, LANES, GLA_KD), lambda i, j: (l, 0, 0)),
        pl.BlockSpec((None, 1, GLA_KD), lambda i, j: (l, 0, 0)),
        pl.BlockSpec((None, CONV_K, IN_TN), lambda i, j: (l, 0, cj(j))),
        pl.BlockSpec((None, 1, IN_TN), lambda i, j: (l, 0, cj(j))),
        pl.BlockSpec((None, 1, IN_TN), lambda i, j: (l, 0, jnp.clip(j - 8, 0, 1))),
    ]
    args = [x, mod, n1w, w_main, w_tail, gate_up, gate_b, conv_w, conv_b, merge_b]
    if has_state:
        in_specs.append(pl.BlockSpec((None, ns, KPREV, IN_TN), lambda i, j: (l, i, 0, cj(j))))
        args.append(conv0)
    aliases = {}
    _stacked_out(convn_prev, in_specs, args, aliases, 7)
    out_shapes = (
        jax.ShapeDtypeStruct((SSD_NG, ntok, SSD_GW), bf16),
        jax.ShapeDtypeStruct((SSD_NG, ntok, SSD_GW), bf16),
        jax.ShapeDtypeStruct((2 * SSD_NG, ntok, SSD_NS), bf16),
        jax.ShapeDtypeStruct((ntok, 2 * GLA_KD + 2 * GLA_VD), bf16),
        jax.ShapeDtypeStruct((ntok, 2 * D), bf16),
        jax.ShapeDtypeStruct((ntok, LANES), f32),
        jax.ShapeDtypeStruct((ntok, GLA_KD), f32),
        jax.ShapeDtypeStruct((DEPTH, nseq, KPREV, CONV_DIM), f32),
    )
    out_specs = (
        pl.BlockSpec((2, tm, SSD_GW), lambda i, j: (jnp.clip(j, 0, 1), i, 0)),
        pl.BlockSpec((2, tm, SSD_GW), lambda i, j: (jnp.clip(j - 2, 0, 1), i, 0)),
        pl.BlockSpec((2 * SSD_NG, tm, SSD_NS), lambda i, j: (0, i, 0)),
        pl.BlockSpec((tm, IN_TN), lambda i, j: (i, jnp.clip(j - 5, 0, 2))),
        pl.BlockSpec((tm, IN_TN), lambda i, j: (i, jnp.clip(j - 8, 0, 1))),
        pl.BlockSpec((tm, LANES), lambda i, j: (i, 0)),
        pl.BlockSpec((tm, GLA_KD), lambda i, j: (i, 0)),
        pl.BlockSpec((None, ns, KPREV, CONV_DIM), lambda i, j: (l, seq_blk(i), 0, 0)),
    )
    return pl.pallas_call(
        functools.partial(_inproj_kernel, seq3d=seq3d, has_state=has_state, per_seq=per_seq, T=T),
        grid=(nt, IN_NJ),
        in_specs=in_specs,
        out_specs=out_specs,
        out_shape=out_shapes,
        input_output_aliases=aliases,
        scratch_shapes=[
            pltpu.VMEM((tm, D), bf16),
            pltpu.VMEM((IN_NSUB, ns, CONV_PAD + T if has_state else 3 * CONV_PAD, IN_SUB), f32),
            pltpu.VMEM((3, IN_NSUB, CONV_PAD, IN_SUB), f32),
        ],
        compiler_params=pltpu.CompilerParams(
            dimension_semantics=("arbitrary", "arbitrary"), vmem_limit_bytes=VMEM_LIMIT),
        name="inproj",
    )(*args)


SSD_CS_PIECES = 2


def _ssd_consts():
    bigsel = np.zeros((SSD_NG, SSD_CS_PIECES * LANES, SSD_HPG * LANES), np.float32)
    e8 = np.zeros((SSD_NG, 2 * LANES, SSD_GW), np.float32)
    for g in range(SSD_NG):
        for j in range(SSD_HPG):
            for k in range(SSD_CS_PIECES):
                bigsel[g, k * LANES + SSD_HPG * g + j, LANES * j:LANES * (j + 1)] = 1.0
            for k in range(2):
                e8[g, k * LANES + SSD_HPG * g + j, SSD_HD * j:SSD_HD * (j + 1)] = 1.0
    return jnp.asarray(bigsel, bf16), jnp.asarray(e8, bf16)


def _ssd_kernel(*refs, L, sb, cps, nchunks, has_state):
    R = sb * L
    xs4_ref, bc_ref, z4_ref, tail_ref = refs[:4]
    k = 4
    ssm0_ref = None
    if has_state:
        ssm0_ref = refs[k]
        k += 1
    dtb_ref, alog_ref, dsk_ref, nw_ref, wbr_ref, bigsel_ref, e8_ref = refs[k:k + 7]
    br_ref, ssmn_ref, cst_scr = refs[-3:]
    c = pl.program_id(1)

    rowi = lax.broadcasted_iota(jnp.int32, (R, R), 0)
    coli = lax.broadcasted_iota(jnp.int32, (R, R), 1)
    mask = rowi >= coli
    if sb > 1:
        sh = L.bit_length() - 1
        mask = jnp.logical_and(mask, (rowi >> sh) == (coli >> sh))
    tri = mask.astype(bf16)
    lane = lax.broadcasted_iota(jnp.int32, (R, LANES), 1)
    lo_half = lane < SSD_HD

    def init():
        if has_state:
            ssmn_ref[...] = ssm0_ref[...]
        else:
            ssmn_ref[...] = jnp.zeros(ssmn_ref.shape, f32)

    if nchunks == 1:
        init()
    else:
        pl.when(c == 0)(init)

    def seq_last(x):
        if sb == 1:
            return x[R - 1:R, :]
        x3 = x.reshape(sb, L, x.shape[-1])
        return jnp.broadcast_to(x3[:, L - 1:L, :], x3.shape).reshape(x.shape)

    def chunk(ci):
        rows = slice(ci * R, (ci + 1) * R)
        dtp = _softplus(tail_ref[rows, :] + dtb_ref[...])
        a = -jnp.exp(alog_ref[...])
        cs = _cumsum_rows(dtp * a, tri)
        cst_scr[ci] = cs.T
        h3 = jnp.concatenate(_split3(cs)[:SSD_CS_PIECES], axis=1)
        d2 = jnp.concatenate(_split2(dtp), axis=1)
        br_ref[rows, :] = jnp.zeros((R, D), f32)

        for g in range(SSD_NG):
            cm = _dot(h3, bigsel_ref[g])
            dt_exp = _dot(d2, e8_ref[g])
            cs_exp = jnp.concatenate(
                [jnp.where(lo_half, cm[:, 2 * LANES * i:2 * LANES * i + LANES],
                           cm[:, 2 * LANES * i + LANES:2 * LANES * (i + 1)]) for i in range(SSD_HPG // 2)],
                axis=1)
            ecs = jnp.exp(cs_exp)
            ce = seq_last(cs_exp)
            xs = xs4_ref[g, rows, :].astype(f32)
            xdt = xs * dt_exp
            xdt_b = xdt.astype(bf16)
            xse = xdt * jnp.exp(ce - cs_exp)
            bg = bc_ref[g, rows, :]
            cg = bc_ref[SSD_NG + g, rows, :]
            cb = _dot_nt(cg, bg)

            pairs = []
            for i in range(SSD_HPG // 2):
                ws = []
                for j in (2 * i, 2 * i + 1):
                    row = cst_scr[ci, pl.ds(SSD_HPG * g + j, 1), :]
                    seg = cm[:, LANES * j:LANES * j + R] - row
                    dec = jnp.exp(jnp.where(mask, seg, -jnp.inf))
                    ws.append((cb * dec).astype(bf16))
                yy = _dot(jnp.concatenate(ws, axis=0), xdt_b[:, LANES * i:LANES * (i + 1)])
                pairs.append(jnp.where(lo_half, yy[:R], yy[R:]))
            y = jnp.concatenate(pairs, axis=1)

            if sb > 1:
                bgf = bg.astype(f32)
                cgf = cg.astype(f32)
            ys_parts = []
            for s in range(sb):
                rs = slice(s * L, (s + 1) * L)
                st = ssmn_ref[s, g]
                b_s = bg if sb == 1 else bgf[rs].astype(bf16)
                c_s = cg if sb == 1 else cgf[rs].astype(bf16)
                ys_parts.append(_dot_nt(c_s, st.astype(bf16)))
                upd = _dot_tn(xse[rs].astype(bf16), b_s)
                e_end = jnp.exp(ce[s * L:s * L + 1, :])
                ssmn_ref[s, g] = jnp.concatenate(
                    [st[SSD_HD * j:SSD_HD * (j + 1)] * e_end[:, SSD_HD * j:SSD_HD * j + 1]
                     + upd[SSD_HD * j:SSD_HD * (j + 1)] for j in range(SSD_HPG)], axis=0)
            ys = ys_parts[0] if sb == 1 else jnp.concatenate(ys_parts, axis=0)
            y = y + ys * ecs + dsk_ref[g] * xs

            yg = y * z4_ref[g, rows, :].astype(f32)
            yn = yg * lax.rsqrt(jnp.mean(yg * yg, axis=-1, keepdims=True) + EPS) * nw_ref[g]
            br_ref[rows, :] += _dot(yn.astype(bf16), wbr_ref[g])

    for ci in range(cps):
        chunk(ci)


def _ssd(l, xs4, bc, z4, tail, ssm0, ssmn_prev, w, *, nseq, L, sb, nchunks, cps=1):
    has_state = ssm0 is not None
    dtb, alog, dsk, nw, wbr, bigsel, e8 = w
    ntok = tail.shape[0]
    chunk_rows = sb * L
    R = cps * chunk_rows
    assert nchunks % cps == 0
    nchunks = nchunks // cps
    if nchunks == 1:
        rblk = lambda b, c: b
    else:
        rblk = lambda b, c: b * nchunks + c
    in_specs = [
        pl.BlockSpec((SSD_NG, R, SSD_GW), lambda b, c: (0, rblk(b, c), 0)),
        pl.BlockSpec((2 * SSD_NG, R, SSD_NS), lambda b, c: (0, rblk(b, c), 0)),
        pl.BlockSpec((SSD_NG, R, SSD_GW), lambda b, c: (0, rblk(b, c), 0)),
        pl.BlockSpec((R, LANES), lambda b, c: (rblk(b, c), 0)),
    ]
    args = [xs4, bc, z4, tail]
    if has_state:
        in_specs.append(pl.BlockSpec((None, sb, SSD_NG, SSD_GW, SSD_NS), lambda b, c: (l, b, 0, 0, 0)))
        args.append(ssm0)
    in_specs += [
        pl.BlockSpec((None, 1, LANES), lambda b, c: (l, 0, 0)),
        pl.BlockSpec((None, 1, LANES), lambda b, c: (l, 0, 0)),
        pl.BlockSpec((None, SSD_NG, 1, SSD_GW), lambda b, c: (l, 0, 0, 0)),
        pl.BlockSpec((None, SSD_NG, 1, SSD_GW), lambda b, c: (l, 0, 0, 0)),
        pl.BlockSpec((None, SSD_NG, SSD_GW, D), lambda b, c: (l, 0, 0, 0)),
        pl.BlockSpec((SSD_NG, SSD_CS_PIECES * LANES, SSD_HPG * LANES), lambda b, c: (0, 0, 0)),
        pl.BlockSpec((SSD_NG, 2 * LANES, SSD_GW), lambda b, c: (0, 0, 0)),
    ]
    args += [dtb, alog, dsk, nw, wbr, bigsel, e8]
    aliases = {}
    _stacked_out(ssmn_prev, in_specs, args, aliases, 1)
    return pl.pallas_call(
        functools.partial(_ssd_kernel, L=L, sb=sb, cps=cps, nchunks=nchunks, has_state=has_state),
        grid=(nseq // sb, nchunks),
        in_specs=in_specs,
        out_specs=(
            pl.BlockSpec((R, D), lambda b, c: (rblk(b, c), 0)),
            pl.BlockSpec((None, sb, SSD_NG, SSD_GW, SSD_NS), lambda b, c: (l, b, 0, 0, 0)),
        ),
        out_shape=(
            jax.ShapeDtypeStruct((ntok, D), f32),
            jax.ShapeDtypeStruct((DEPTH, nseq, SSD_NG, SSD_GW, SSD_NS), f32),
        ),
        input_output_aliases=aliases,
        scratch_shapes=[pltpu.VMEM((cps, LANES, chunk_rows), f32)],
        compiler_params=pltpu.CompilerParams(
            dimension_semantics=("arbitrary", "arbitrary"), vmem_limit_bytes=VMEM_LIMIT),
        name="ssd",
    )(*args)


GLA_C2 = 4


def _gla_consts(L, C1, sb):
    R = sb * L
    nb2 = C1 // GLA_C2
    i = np.arange(R)[:, None]
    s = np.arange(R)[None, :]
    same_seq = (i // L) == (s // L)
    mats = [(s <= i) & (s > i - d) & same_seq for d in range(1, GLA_C2)]
    mats.append((s <= i) & (s >= (i // GLA_C2) * GLA_C2))
    for r in range(1, nb2):
        mats.append((s > i) & (s <= (i // C1) * C1 + GLA_C2 * r - 1))
    shifts = [(s == i - d) & same_seq for d in range(1, GLA_C2)]
    return (jnp.asarray(np.concatenate(mats, 0), bf16), jnp.asarray(np.concatenate(shifts, 0), bf16))


def _gla_kernel(*refs, L, C1, sb, nchunks, has_state):
    R = sb * L
    nb1 = L // C1
    nb2 = C1 // GLA_C2
    qkvr_ref, lg_ref = refs[:2]
    k = 2
    gla0_ref = None
    if has_state:
        gla0_ref = refs[k]
        k += 1
    gnw_ref, wbr_ref, sums_ref, shift_ref = refs[k:k + 4]
    br_ref, glan_ref = refs[-2:]
    c = pl.program_id(1)

    rowi = lax.broadcasted_iota(jnp.int32, (R, R), 0)
    coli = lax.broadcasted_iota(jnp.int32, (R, R), 1)
    shl = L.bit_length() - 1
    same_seq = (rowi >> shl) == (coli >> shl)
    tri = jnp.logical_and(rowi >= coli, same_seq).astype(bf16)
    sh1 = C1.bit_length() - 1
    same_blk1 = (rowi >> sh1) == (coli >> sh1)
    sub_i = (rowi & (C1 - 1)) >> (GLA_C2.bit_length() - 1)
    off_i = rowi & (GLA_C2 - 1)
    rowl = lax.broadcasted_iota(jnp.int32, (R, 1), 0)
    sub_l = (rowl & (C1 - 1)) >> (GLA_C2.bit_length() - 1)
    blk_i = (rowi & (L - 1)) >> sh1
    rel_l = rowl & (L - 1)
    pad_rows = [jnp.zeros((LANES - R, GLA_DK), bf16)] if R < LANES else []

    def init():
        if has_state:
            glan_ref[...] = gla0_ref[...]
        else:
            glan_ref[...] = jnp.zeros(glan_ref.shape, f32)

    if nchunks == 1:
        init()
    else:
        pl.when(c == 0)(init)

    def seq_last(x):
        if sb == 1:
            return x[R - 1:R, :]
        x3 = x.reshape(sb, L, x.shape[-1])
        return jnp.broadcast_to(x3[:, L - 1:L, :], x3.shape).reshape(x.shape)

    def rows2d(x):
        return x.reshape(R, x.shape[-1]) if x.ndim == 3 else x

    lg = rows2d(lg_ref[...])
    g = _cumsum_rows(lg, tri)
    g_end = seq_last(g)
    lg_hi, lg_lo = _split2(lg)
    if R % LANES == 0:
        sums = _dot(jnp.concatenate([sums_ref[...], sums_ref[...]], axis=1),
                    jnp.concatenate([lg_hi, lg_lo], axis=0))
    else:
        sums = _dot(sums_ref[...], lg_hi) + _dot(sums_ref[...], lg_lo)
    a_d = [sums[R * (d - 1):R * d] for d in range(1, GLA_C2)]
    a_sub = sums[R * (GLA_C2 - 1):R * GLA_C2]
    b_sub = [sums[R * (GLA_C2 - 1 + r):R * (GLA_C2 + r)] for r in range(1, nb2)]
    q = rows2d(qkvr_ref[..., 0:GLA_KD]).astype(f32)
    kb = rows2d(qkvr_ref[..., GLA_KD:2 * GLA_KD])
    k_ = kb.astype(f32)
    k_sh = [_dot(shift_ref[R * (d - 1):R * d, :], kb) for d in range(1, GLA_C2)]

    acc = jnp.zeros((R, D), f32)
    for h in range(GLA_NH):
        kl = slice(GLA_DK * h, GLA_DK * (h + 1))
        vl = slice(2 * GLA_KD + GLA_DV * h, 2 * GLA_KD + GLA_DV * (h + 1))
        rl = slice(2 * GLA_KD + GLA_VD + GLA_DV * h, 2 * GLA_KD + GLA_VD + GLA_DV * (h + 1))
        gh = g[:, kl]
        qh = q[:, kl]
        kh = k_[:, kl]
        vb = rows2d(qkvr_ref[..., vl])
        qg = qh * jnp.exp(gh)
        kd = kh * jnp.exp(g_end[:, kl] - gh)

        if sb == 1:
            st = glan_ref[0, h]
            o = _dot(qg.astype(bf16), st.astype(bf16))
            e_col = jnp.exp(jnp.broadcast_to(g_end[0:1, kl], (8, GLA_DK))).T[:, 0:1]
            glan_ref[0, h] = st * e_col + _dot_tn(kd.astype(bf16), vb)
        else:
            vf = vb.astype(f32)
            parts = []
            for s in range(sb):
                rs = slice(s * L, (s + 1) * L)
                st = glan_ref[s, h]
                parts.append(_dot(qg[rs].astype(bf16), st.astype(bf16)))
                e_col = jnp.exp(jnp.broadcast_to(g_end[s * L:s * L + 1, kl], (8, GLA_DK))).T[:, 0:1]
                glan_ref[s, h] = st * e_col + _dot_tn(kd[rs].astype(bf16), vf[rs].astype(bf16))
            o = jnp.concatenate(parts, axis=0)

        att = jnp.zeros((R, R), f32)
        if nb1 > 1:
            def bnd(s, i):
                if i == 0:
                    return jnp.zeros((1, GLA_DK), f32)
                return gh[s * L + C1 * i - 1:s * L + C1 * i, :]

            gblk = jnp.concatenate([jnp.broadcast_to(bnd(s, i), (C1, GLA_DK))
                                    for s in range(sb) for i in range(nb1)], axis=0)
            qt = (qh * jnp.exp(gh - gblk)).astype(bf16)
            kts = []
            for i in range(1, nb1):
                gb = jnp.concatenate([jnp.broadcast_to(bnd(s, i), (L, GLA_DK)) for s in range(sb)], axis=0)
                kt = kh * jnp.exp(jnp.where(rel_l < C1 * i, gb - gh, -jnp.inf))
                kts += [kt.astype(bf16)] + pad_rows
            out1 = _dot_nt(qt, jnp.concatenate(kts, axis=0))
            for i in range(1, nb1):
                att = att + jnp.where(jnp.logical_and(blk_i == i, same_seq),
                                      out1[:, LANES * (i - 1):LANES * (i - 1) + R], 0.0)

        qt2 = (qh * jnp.exp(a_sub[:, kl])).astype(bf16)
        kts = []
        for r in range(1, nb2):
            kt = kh * jnp.exp(jnp.where(sub_l < r, b_sub[r - 1][:, kl], -jnp.inf))
            kts += [kt.astype(bf16)] + pad_rows
        out2 = _dot_nt(qt2, jnp.concatenate(kts, axis=0))
        for r in range(1, nb2):
            att = att + jnp.where(jnp.logical_and(sub_i == r, same_blk1),
                                  out2[:, LANES * (r - 1):LANES * (r - 1) + R], 0.0)

        for d in range(GLA_C2):
            t = qh * kh if d == 0 else qh * jnp.exp(a_d[d - 1][:, kl]) * k_sh[d - 1][:, kl]
            band = jnp.sum(t, axis=-1, keepdims=True)
            hit = jnp.logical_and(off_i >= d, coli == rowi - d)
            att = att + jnp.where(hit, band, 0.0)
        o = o + _dot(att.astype(bf16), vb)

        on = o * lax.rsqrt(jnp.mean(o * o, axis=-1, keepdims=True) + EPS) * gnw_ref[...]
        og = on * rows2d(qkvr_ref[..., rl]).astype(f32)
        acc = acc + _dot(og.astype(bf16), wbr_ref[GLA_DV * h:GLA_DV * (h + 1), :])
    br_ref[...] = acc.reshape(br_ref.shape)


def _gla(l, qkvr, lg, gla0, glan_prev, w, *, nseq, L, C1, sb, nchunks):
    has_state = gla0 is not None
    gnw, wbr = w
    sums, shifts = _gla_consts(L, C1, sb)
    ntok = qkvr.shape[0]
    rb = sb * L
    wq = 2 * GLA_KD + 2 * GLA_VD
    if nchunks == 1:
        row = lambda b, c: (b, 0)
        blk = lambda w_: (rb, w_)
        view = lambda a: a
    else:
        row = lambda b, c: (b, c, 0)
        blk = lambda w_: (sb, L, w_)
        view = lambda a: a.reshape(nseq, ntok // nseq, a.shape[-1])
    in_specs = [
        pl.BlockSpec(blk(wq), row),
        pl.BlockSpec(blk(GLA_KD), row),
    ]
    args = [view(qkvr), view(lg)]
    if has_state:
        in_specs.append(pl.BlockSpec((None, sb, GLA_NH, GLA_DK, GLA_DV), lambda b, c: (l, b, 0, 0, 0)))
        args.append(gla0)
    in_specs += [
        pl.BlockSpec((None, 1, GLA_DV), lambda b, c: (l, 0, 0)),
        pl.BlockSpec((None, GLA_VD, D), lambda b, c: (l, 0, 0)),
        pl.BlockSpec(sums.shape, lambda b, c: (0, 0)),
        pl.BlockSpec(shifts.shape, lambda b, c: (0, 0)),
    ]
    args += [gnw, wbr, sums, shifts]
    aliases = {}
    _stacked_out(glan_prev, in_specs, args, aliases, 1)
    br_shape = (ntok, D) if nchunks == 1 else (nseq, ntok // nseq, D)
    br, glan = pl.pallas_call(
        functools.partial(_gla_kernel, L=L, C1=C1, sb=sb, nchunks=nchunks, has_state=has_state),
        grid=(nseq // sb, nchunks),
        in_specs=in_specs,
        out_specs=(
            pl.BlockSpec(blk(D), row),
            pl.BlockSpec((None, sb, GLA_NH, GLA_DK, GLA_DV), lambda b, c: (l, b, 0, 0, 0)),
        ),
        out_shape=(
            jax.ShapeDtypeStruct(br_shape, f32),
            jax.ShapeDtypeStruct((DEPTH, nseq, GLA_NH, GLA_DK, GLA_DV), f32),
        ),
        input_output_aliases=aliases,
        compiler_params=pltpu.CompilerParams(
            dimension_semantics=("arbitrary", "arbitrary"), vmem_limit_bytes=VMEM_LIMIT),
        name="gla",
    )(*args)
    return br.reshape(ntok, D), glan


def _top2_sum(a, b, c, d):
    hi1, lo1 = jnp.maximum(a, b), jnp.minimum(a, b)
    hi2, lo2 = jnp.maximum(c, d), jnp.minimum(c, d)
    return jnp.maximum(hi1, hi2) + jnp.maximum(jnp.minimum(hi1, hi2), jnp.maximum(lo1, lo2))


def _route_rows(sig, biased):
    gsc = [_top2_sum(*biased[EXP_PER_GRP * g:EXP_PER_GRP * (g + 1)]) for g in range(N_EGRP)]
    best = jnp.zeros_like(gsc[0], dtype=jnp.int32)
    m = gsc[0]
    for g in range(1, N_EGRP):
        better = gsc[g] > m
        best = jnp.where(better, g, best)
        m = jnp.where(better, gsc[g], m)
    masked = [jnp.where(best == (e // EXP_PER_GRP), biased[e], -jnp.inf) for e in range(N_EXP)]

    def first_argmax(vals):
        idx = jnp.zeros_like(best)
        mx = vals[0]
        for e in range(1, N_EXP):
            better = vals[e] > mx
            idx = jnp.where(better, e, idx)
            mx = jnp.where(better, vals[e], mx)
        return idx

    i1 = first_argmax(masked)
    i2 = first_argmax([jnp.where(i1 == e, -jnp.inf, masked[e]) for e in range(N_EXP)])
    w1 = sum(jnp.where(i1 == e, sig[e], 0.0) for e in range(N_EXP))
    w2 = sum(jnp.where(i2 == e, sig[e], 0.0) for e in range(N_EXP))
    den = w1 + w2
    return best, i1, i2, w1 / den, w2 / den


def _merge_kernel(*refs, seq3d, sparse):
    (x_ref, bra_ref, brb_ref, gate_ref, mod_ref, wo_ref, n2_ref, wr_hi_ref, wr_lo_ref, rb_ref) = refs[:10]
    mixed_in = (gate_ref[:, 0:D].astype(f32) * bra_ref[...]
                + gate_ref[:, D:2 * D].astype(f32) * brb_ref[...])
    mixed = _dot(mixed_in.astype(bf16), wo_ref[...])
    x = x_ref[...]
    if seq3d:
        x1 = x + mod_ref[:, 2:3, :] * mixed.reshape(x.shape)
    else:
        x1 = x + mod_ref[2:3, :] * mixed
    refs[10][...] = x1
    h2 = _norm_mod(x1, mod_ref, n2_ref, 3, 4, seq3d)
    hi, lo = _split2(h2)
    logits = _dot(hi, wr_hi_ref[...]) + (_dot(hi, wr_lo_ref[...]) + _dot(lo, wr_hi_ref[...]))
    tm = logits.shape[0]
    lt = logits.T
    sig_all = _sigmoid(lt[0:N_EXP, :])
    bias_all = sig_all + rb_ref[...]
    sig = [sig_all[e:e + 1, :] for e in range(N_EXP)]
    biased = [bias_all[e:e + 1, :] for e in range(N_EXP)]
    best, i1, i2, w1, w2 = _route_rows(sig, biased)
    if not sparse:
        h2_ref, comb_ref = refs[11:13]
        h2_ref[...] = h2.astype(bf16)
        comb = [jnp.where(i1 == e, w1, 0.0) + jnp.where(i2 == e, w2, 0.0) for e in range(N_EXP)]
        comb_t = jnp.concatenate(comb + [jnp.zeros((LANES - N_EXP, tm), f32)], axis=0)
        comb_ref[...] = comb_t.T
        return

    h2x_ref, plan_ref, counts_ref, base_scr = refs[11:15]
    swap = i1 > i2
    wa = jnp.where(swap, w2, w1)
    wb = jnp.where(swap, w1, w2)
    a = jnp.minimum(i1, i2) & (EXP_PER_GRP - 1)
    b = jnp.maximum(i1, i2) & (EXP_PER_GRP - 1)
    pair = jnp.where(a == 0, b - 1, jnp.where(a == 1, b + 1, 5))
    cls = best * N_PAIR + pair
    sub = lax.broadcasted_iota(jnp.int32, (CLS_PAD, tm), 0)
    onehot = (sub == cls).astype(f32)
    before = (lax.broadcasted_iota(jnp.int32, (tm, tm), 0)
              < lax.broadcasted_iota(jnp.int32, (tm, tm), 1)).astype(bf16)
    prefix = _dot(onehot.astype(bf16), before)

    @pl.when(pl.program_id(0) == 0)
    def _():
        base_scr[...] = jnp.zeros_like(base_scr)

    base = base_scr[...]
    rank = jnp.sum(onehot * (prefix + base[:, 0:1]), axis=0, keepdims=True)
    base = base + jnp.sum(onehot, axis=1, keepdims=True)
    base_scr[...] = base
    counts_ref[...] = base
    plan_ref[...] = jnp.concatenate(
        [cls, rank.astype(jnp.int32), jnp.zeros((6, tm), jnp.int32)], axis=0)
    ext_t = jnp.concatenate([wa, wb, jnp.zeros((MOE_EXT - 2, tm), f32)], axis=0)
    h2x_ref[:, 0:D] = h2
    h2x_ref[:, D:D + MOE_EXT] = ext_t.T


def _merge(l, x, bra, brb, gate, mod, wo, n2w, wr_hi, wr_lo, rb, *, seq3d, sparse):
    if seq3d:
        ns = _seq_tile(x.shape[0], 32)
        ntok = x.shape[0] * x.shape[1]
        tm = ns * x.shape[1]
        nt = x.shape[0] // ns
        x_spec = pl.BlockSpec((ns, x.shape[1], D), lambda i: (i, 0, 0))
        mod_spec = pl.BlockSpec((None, ns, 6, D), lambda i: (l, i, 0, 0))
    else:
        ntok = x.shape[0]
        nseq = mod.shape[1]
        tm = min(512, ntok // nseq)
        nt = ntok // tm
        per_seq = ntok // nseq // tm
        x_spec = pl.BlockSpec((tm, D), lambda i: (i, 0))
        mod_spec = pl.BlockSpec((None, None, 6, D), lambda i: (l, i // per_seq, 0, 0))
    row = lambda i: (i, 0)
    const = lambda i: (0, 0)
    if sparse:
        out_specs = (x_spec, pl.BlockSpec((tm, D + MOE_EXT), row), pl.BlockSpec((8, tm), lambda i: (0, i)),
                     pl.BlockSpec((CLS_PAD, LANES), const))
        out_shape = (jax.ShapeDtypeStruct(x.shape, f32), jax.ShapeDtypeStruct((ntok, D + MOE_EXT), f32),
                     jax.ShapeDtypeStruct((8, ntok), jnp.int32), jax.ShapeDtypeStruct((CLS_PAD, LANES), f32))
        scratch = [pltpu.VMEM((CLS_PAD, LANES), f32)]
    else:
        out_specs = (x_spec, pl.BlockSpec((tm, D), row), pl.BlockSpec((tm, LANES), row))
        out_shape = (jax.ShapeDtypeStruct(x.shape, f32), jax.ShapeDtypeStruct((ntok, D), bf16),
                     jax.ShapeDtypeStruct((ntok, LANES), f32))
        scratch = []
    return pl.pallas_call(
        functools.partial(_merge_kernel, seq3d=seq3d, sparse=sparse),
        grid=(nt,),
        in_specs=[
            x_spec,
            pl.BlockSpec((tm, D), row),
            pl.BlockSpec((tm, D), row),
            pl.BlockSpec((tm, 2 * D), row),
            mod_spec,
            pl.BlockSpec((None, D, D), lambda i: (l, 0, 0)),
            pl.BlockSpec((None, 1, D), lambda i: (l, 0, 0)),
            pl.BlockSpec((D, LANES), const),
            pl.BlockSpec((D, LANES), const),
            pl.BlockSpec((N_EXP, 1), const),
        ],
        out_specs=out_specs,
        out_shape=out_shape,
        scratch_shapes=scratch,
        compiler_params=pltpu.CompilerParams(
            dimension_semantics=("arbitrary",), vmem_limit_bytes=VMEM_LIMIT),
        name="merge",
    )(x, bra, brb, gate, mod, wo, n2w, wr_hi, wr_lo, rb)


def _dispatch_kernel(pos_ref, src_ref, buf_in, xs_hbm, sem):
    del buf_in
    tm = src_ref.shape[0]

    def issue(t, c):
        p = pos_ref[0, t]
        pltpu.make_async_copy(src_ref.at[pl.ds(t, 1), :], xs_hbm.at[pl.ds(p, 1), :], sem).start()
        return c

    lax.fori_loop(0, tm, issue, 0, unroll=MOE_DMA_UNROLL)

    def drain(t, c):
        pltpu.make_async_copy(src_ref.at[pl.ds(0, 1), :], xs_hbm.at[pl.ds(0, 1), :], sem).wait()
        return c

    lax.fori_loop(0, tm, drain, 0, unroll=MOE_DMA_UNROLL)


def _dispatch(pos3, h2x, xs_buf):
    nt, _, tm = pos3.shape
    return pl.pallas_call(
        _dispatch_kernel,
        grid=(nt,),
        in_specs=[
            pl.BlockSpec((None, 1, tm), lambda i: (i, 0, 0), memory_space=pltpu.SMEM),
            pl.BlockSpec((tm, D + MOE_EXT), lambda i: (i, 0)),
            pl.BlockSpec(memory_space=pl.ANY),
        ],
        out_specs=pl.BlockSpec(memory_space=pl.ANY),
        out_shape=jax.ShapeDtypeStruct(xs_buf.shape, f32),
        input_output_aliases={2: 0},
        scratch_shapes=[pltpu.SemaphoreType.DMA(())],
        compiler_params=pltpu.CompilerParams(dimension_semantics=("arbitrary",)),
        name="dispatch",
    )(pos3, h2x, xs_buf)


def _moe_sparse_kernel(src_ref, valid_ref, ea_ref, eb_ref, xs_ref,
                       wga_ref, wua_ref, wda_ref, wgb_ref, wub_ref, wdb_ref, ys_ref):
    del src_ref, ea_ref, eb_ref
    t = pl.program_id(0)

    @pl.when(valid_ref[t] == 1)
    def _():
        x = xs_ref[...]
        h = x[:, 0:D].astype(bf16)

        def ffn(wg, wu, wd):
            u = (_silu(_dot(h, wg[...])) * _dot(h, wu[...])).astype(bf16)
            return _dot(u, wd[...])

        ys_ref[...] = (x[:, D:D + 1] * ffn(wga_ref, wua_ref, wda_ref)
                       + x[:, D + 1:D + 2] * ffn(wgb_ref, wub_ref, wdb_ref))

    @pl.when(valid_ref[pl.program_id(0)] == 0)
    def _():
        ys_ref[...] = jnp.zeros(ys_ref.shape, f32)


def _moe_sparse(tiles, xs, wg, wu, wd):
    src, valid, ea, eb = tiles
    nrow = xs.shape[0]
    ntile = nrow // MOE_R
    wa_map = lambda t, src, valid, ea, eb: (ea[t], 0, 0)
    wb_map = lambda t, src, valid, ea, eb: (eb[t], 0, 0)
    row_map = lambda t, src, valid, ea, eb: (src[t], 0)
    return pl.pallas_call(
        _moe_sparse_kernel,
        grid_spec=pltpu.PrefetchScalarGridSpec(
            num_scalar_prefetch=4,
            grid=(ntile,),
            in_specs=[
                pl.BlockSpec((MOE_R, D + MOE_EXT), row_map),
                pl.BlockSpec((None, D, EXP_FF), wa_map),
                pl.BlockSpec((None, D, EXP_FF), wa_map),
                pl.BlockSpec((None, EXP_FF, D), wa_map),
                pl.BlockSpec((None, D, EXP_FF), wb_map),
                pl.BlockSpec((None, D, EXP_FF), wb_map),
                pl.BlockSpec((None, EXP_FF, D), wb_map),
            ],
            out_specs=pl.BlockSpec((MOE_R, D), lambda t, src, valid, ea, eb: (t, 0)),
        ),
        out_shape=jax.ShapeDtypeStruct((nrow, D), f32),
        compiler_params=pltpu.CompilerParams(
            dimension_semantics=("arbitrary",), vmem_limit_bytes=VMEM_LIMIT),
        name="moe_sparse",
    )(src, valid, ea, eb, xs, wg, wu, wd, wg, wu, wd)


def _combine_kernel(pos_ref, ys_hbm, x1_ref, mod_ref, fw_ref, x2_ref, buf, sem, *, final):
    tm = x1_ref.shape[0]

    def issue(t, c):
        p = pos_ref[0, t]
        pltpu.make_async_copy(ys_hbm.at[pl.ds(p, 1), :], buf.at[pl.ds(t, 1), :], sem).start()
        return c

    lax.fori_loop(0, tm, issue, 0, unroll=MOE_DMA_UNROLL)

    def drain(t, c):
        pltpu.make_async_copy(ys_hbm.at[pl.ds(0, 1), :], buf.at[pl.ds(0, 1), :], sem).wait()
        return c

    lax.fori_loop(0, tm, drain, 0, unroll=MOE_DMA_UNROLL)
    x2 = x1_ref[...] + mod_ref[5:6, :] * buf[...]
    if final:
        x2 = x2 * lax.rsqrt(jnp.mean(x2 * x2, axis=-1, keepdims=True) + EPS) * fw_ref[...]
    x2_ref[...] = x2


def _combine(l, pos3, ys, x1, mod, fw, *, final):
    nt, _, tm = pos3.shape
    ntok = x1.shape[0]
    per_seq = ntok // mod.shape[1] // tm
    return pl.pallas_call(
        functools.partial(_combine_kernel, final=final),
        grid=(nt,),
        in_specs=[
            pl.BlockSpec((None, 1, tm), lambda i: (i, 0, 0), memory_space=pltpu.SMEM),
            pl.BlockSpec(memory_space=pl.ANY),
            pl.BlockSpec((tm, D), lambda i: (i, 0)),
            pl.BlockSpec((None, None, 6, D), lambda i: (l, i // per_seq, 0, 0)),
            pl.BlockSpec((1, D), lambda i: (0, 0)),
        ],
        out_specs=pl.BlockSpec((tm, D), lambda i: (i, 0)),
        out_shape=jax.ShapeDtypeStruct((ntok, D), f32),
        scratch_shapes=[pltpu.VMEM((tm, D), f32), pltpu.SemaphoreType.DMA(())],
        compiler_params=pltpu.CompilerParams(dimension_semantics=("arbitrary",)),
        name="combine",
    )(pos3, ys, x1, mod, fw)


def _moe_plan(plan, counts, ntile):
    cnt = counts[:N_CLS, 0].astype(jnp.int32)
    tiles_per = (cnt + (MOE_R - 1)) // MOE_R
    tstart = jnp.cumsum(tiles_per) - tiles_per
    total = jnp.sum(tiles_per)
    pos = tstart[plan[0]] * MOE_R + plan[1]
    t = jnp.arange(ntile, dtype=jnp.int32)
    src = jnp.minimum(t, total - 1)
    tcls = jnp.sum((src[:, None] >= tstart[None, :]).astype(jnp.int32), axis=1) - 1
    grp = tcls // N_PAIR
    pair = tcls % N_PAIR
    pa = jnp.asarray(np.array([0, 0, 0, 1, 1, 2], np.int32))
    pb = jnp.asarray(np.array([1, 2, 3, 2, 3, 3], np.int32))
    ea = grp * EXP_PER_GRP + pa[pair]
    eb = grp * EXP_PER_GRP + pb[pair]
    valid = (t < total).astype(jnp.int32)
    return pos, (src, valid, ea, eb)


def _moe_kernel(h2_ref, comb_ref, wg_ref, wu_ref, wd_ref, x1_ref, mod_ref,
                x2_ref, wgb_ref, wub_ref, wdb_ref, acc_scr, *, seq3d):
    e = pl.program_id(1)

    @pl.when(e == 0)
    def _():
        acc_scr[...] = jnp.zeros_like(acc_scr)

    wgb_ref[...] = wg_ref[...].astype(bf16)
    wub_ref[...] = wu_ref[...].astype(bf16)
    wdb_ref[...] = wd_ref[...].astype(bf16)
    h = h2_ref[...]
    a = _dot(h, wgb_ref[...])
    b = _dot(h, wub_ref[...])
    t = (_silu(a) * b).astype(bf16)
    ye = _dot(t, wdb_ref[...])
    lane = lax.broadcasted_iota(jnp.int32, comb_ref.shape, 1)
    w = jnp.sum(jnp.where(lane == e, comb_ref[...], 0.0), axis=-1, keepdims=True)
    acc_scr[...] += w * ye

    @pl.when(e == N_EXP - 1)
    def _():
        x1 = x1_ref[...]
        if seq3d:
            x2_ref[...] = x1 + mod_ref[:, 5:6, :] * acc_scr[...].reshape(x1.shape)
        else:
            x2_ref[...] = x1 + mod_ref[5:6, :] * acc_scr[...]


def _moe(l, h2, comb, wg, wu, wd, x1, mod, *, seq3d):
    if seq3d:
        ns = x1.shape[0]
        ntok = ns * x1.shape[1]
        tm = ntok
        nt = 1
        x_spec = pl.BlockSpec((ns, x1.shape[1], D), lambda i, e: (0, 0, 0))
        mod_spec = pl.BlockSpec((None, ns, 6, D), lambda i, e: (l, 0, 0, 0))
    else:
        ntok = x1.shape[0]
        nseq = mod.shape[1]
        tm = min(1024, ntok // nseq)
        nt = ntok // tm
        per_seq = ntok // nseq // tm
        x_spec = pl.BlockSpec((tm, D), lambda i, e: (i, 0))
        mod_spec = pl.BlockSpec((None, None, 6, D), lambda i, e: (l, i // per_seq, 0, 0))
    assert nt == 1, "the bf16 weight copies are written once per expert"
    row = lambda i, e: (i, 0)
    return pl.pallas_call(
        functools.partial(_moe_kernel, seq3d=seq3d),
        grid=(nt, N_EXP),
        in_specs=[
            pl.BlockSpec((tm, D), row),
            pl.BlockSpec((tm, LANES), row),
            pl.BlockSpec((None, None, D, EXP_FF), lambda i, e: (l, e, 0, 0)),
            pl.BlockSpec((None, None, D, EXP_FF), lambda i, e: (l, e, 0, 0)),
            pl.BlockSpec((None, None, EXP_FF, D), lambda i, e: (l, e, 0, 0)),
            x_spec,
            mod_spec,
        ],
        out_specs=(
            x_spec,
            pl.BlockSpec((None, D, EXP_FF), lambda i, e: (e, 0, 0)),
            pl.BlockSpec((None, D, EXP_FF), lambda i, e: (e, 0, 0)),
            pl.BlockSpec((None, EXP_FF, D), lambda i, e: (e, 0, 0)),
        ),
        out_shape=(
            jax.ShapeDtypeStruct(x1.shape, f32),
            jax.ShapeDtypeStruct((N_EXP, D, EXP_FF), bf16),
            jax.ShapeDtypeStruct((N_EXP, D, EXP_FF), bf16),
            jax.ShapeDtypeStruct((N_EXP, EXP_FF, D), bf16),
        ),
        scratch_shapes=[pltpu.VMEM((tm, D), f32)],
        compiler_params=pltpu.CompilerParams(
            dimension_semantics=("arbitrary", "arbitrary"), vmem_limit_bytes=VMEM_LIMIT),
        name="moe",
    )(h2, comb, wg, wu, wd, x1, mod)


def _final_norm_kernel(x_ref, w_ref, o_ref):
    x = x_ref[...]
    o_ref[...] = x * lax.rsqrt(jnp.mean(x * x, axis=-1, keepdims=True) + EPS) * w_ref[...]


def _final_norm(x2d, w):
    ntok = x2d.shape[0]
    tm = min(1024, ntok)
    return pl.pallas_call(
        _final_norm_kernel,
        grid=(ntok // tm,),
        in_specs=[pl.BlockSpec((tm, D), lambda i: (i, 0)), pl.BlockSpec((1, D), lambda i: (0, 0))],
        out_specs=pl.BlockSpec((tm, D), lambda i: (i, 0)),
        out_shape=jax.ShapeDtypeStruct((ntok, D), f32),
        compiler_params=pltpu.CompilerParams(dimension_semantics=("arbitrary",)),
        name="final_norm",
    )(x2d, w)


def _prep_in_weights(w_in):
    o = 0
    z = w_in[:, :, o:o + SSD_INNER]; o += SSD_INNER
    xbc = w_in[:, :, o:o + CONV_DIM]; o += CONV_DIM
    dt = w_in[:, :, o:o + SSD_NH]; o += SSD_NH
    q = w_in[:, :, o:o + GLA_KD] * (GLA_DK ** -0.5); o += GLA_KD
    kvr = w_in[:, :, o:o + GLA_KD + 2 * GLA_VD]; o += GLA_KD + 2 * GLA_VD
    glr = w_in[:, :, o:o + GLA_RANK]; o += GLA_RANK
    gates = w_in[:, :, o:o + 2 * D]
    main = jnp.concatenate([z, xbc, q, kvr, gates], axis=-1).astype(bf16)
    pad = jnp.zeros(w_in.shape[:2] + (LANES - SSD_NH - GLA_RANK,), w_in.dtype)
    tail = jnp.concatenate([dt, glr, pad], axis=-1).astype(bf16)
    return main, tail


def _pad_lanes(v, width=LANES):
    return jnp.concatenate([v, jnp.zeros(v.shape[:-1] + (width - v.shape[-1],), v.dtype)], axis=-1)


def kernel(x_prompt, x_sample, c_prompt, c_sample, state_conv, state_ssm, state_gla, w_ada, b_ada, norm1_w, w_in, conv_w, conv_b, dt_bias, a_log, d_skip, ssd_norm_w, w_ssd_br, gla_gate_up, gla_gate_b, gla_norm_w, w_gla_br, merge_b, w_out, norm2_w, w_router, router_bias, w_exp_gate, w_exp_up, w_exp_down, final_norm_w):
    bp, sp, _ = x_prompt.shape
    bs, ss, _ = x_sample.shape

    mod = _adaln(jnp.concatenate([c_prompt, c_sample], axis=0), w_ada, b_ada)
    mod_p = mod[:, :bp].reshape(DEPTH, bp, 6, D)
    mod_s = mod[:, bp:].reshape(DEPTH, bs, 6, D)

    w_main, w_tail = _prep_in_weights(w_in)
    gate_up = jnp.concatenate(
        [jnp.zeros((DEPTH, TAIL_GLR0, GLA_KD), f32), gla_gate_up,
         jnp.zeros((DEPTH, LANES - TAIL_GLR0 - GLA_RANK, GLA_KD), f32)], axis=1).astype(bf16)
    in_w = (norm1_w.reshape(DEPTH, 1, D), w_main, w_tail, gate_up, gla_gate_b.reshape(DEPTH, 1, GLA_KD),
            conv_w, conv_b.reshape(DEPTH, 1, CONV_DIM), merge_b.reshape(DEPTH, 1, 2 * D))
    bigsel, e8 = _ssd_consts()
    ssd_w = (_pad_lanes(dt_bias).reshape(DEPTH, 1, LANES), _pad_lanes(a_log).reshape(DEPTH, 1, LANES),
             jnp.repeat(d_skip, SSD_HD, axis=-1).reshape(DEPTH, SSD_NG, 1, SSD_GW),
             ssd_norm_w.reshape(DEPTH, SSD_NG, 1, SSD_GW),
             w_ssd_br.astype(bf16).reshape(DEPTH, SSD_NG, SSD_GW, D), bigsel, e8)
    gla_w = (gla_norm_w.reshape(DEPTH, 1, GLA_DV), w_gla_br.astype(bf16))
    w_out_b = w_out.astype(bf16)
    n2 = norm2_w.reshape(DEPTH, 1, D)
    wr = _pad_lanes(w_router)
    wr_hi = wr.astype(bf16)
    wr_lo = (wr - wr_hi.astype(f32)).astype(bf16)
    rb = router_bias.reshape(N_EXP, 1)

    xp = x_prompt.reshape(bp * sp, D)
    xs = x_sample
    ssm_s = state_ssm.reshape(DEPTH, bs, SSD_NG, SSD_GW, SSD_NS)
    ssd_lp = min(128, sp)
    gla_lp = min(64, sp)
    cp = sp_ = gp = cs_ = ss_ = gs = None
    moe_tm = min(MOE_TM, sp)
    moe_tiles = -(-(bp * sp) // MOE_R) + N_CLS
    xs_rows = jnp.zeros((moe_tiles * MOE_R, D + MOE_EXT), f32)
    fw = final_norm_w.reshape(1, D)
    for l in range(DEPTH):
        z4, xs4, bc, qkvr, gate, tail, lg, cs_ = _inproj(l, xs, mod_s, state_conv, cs_, in_w, seq3d=True)
        bra, ss_ = _ssd(l, xs4, bc, z4, tail, ssm_s, ss_, ssd_w, nseq=bs, L=ss, sb=8, nchunks=1)
        brb, gs = _gla(l, qkvr, lg, state_gla, gs, gla_w, nseq=bs, L=ss, C1=ss, sb=8, nchunks=1)
        x1, h2, comb = _merge(l, xs, bra, brb, gate, mod_s, w_out_b, n2, wr_hi, wr_lo, rb,
                              seq3d=True, sparse=False)
        xs, wg_b, wu_b, wd_b = _moe(l, h2, comb, w_exp_gate, w_exp_up, w_exp_down, x1, mod_s, seq3d=True)

        z4, xs4, bc, qkvr, gate, tail, lg, cp = _inproj(l, xp, mod_p, None, cp, in_w, seq3d=False)
        bra, sp_ = _ssd(l, xs4, bc, z4, tail, None, sp_, ssd_w, nseq=bp, L=ssd_lp, sb=1, nchunks=sp // ssd_lp,
                        cps=2 if (sp // ssd_lp) % 2 == 0 else 1)
        brb, gp = _gla(l, qkvr, lg, None, gp, gla_w, nseq=bp, L=gla_lp, C1=16,
                       sb=2 if bp % 2 == 0 else 1, nchunks=sp // gla_lp)
        x1, h2x, plan, counts = _merge(l, xp, bra, brb, gate, mod_p, w_out_b, n2, wr_hi, wr_lo, rb,
                                       seq3d=False, sparse=True)
        pos, tiles = _moe_plan(plan, counts, moe_tiles)
        pos3 = pos.reshape(bp * sp // moe_tm, 1, moe_tm)
        xs_rows = _dispatch(pos3, h2x, xs_rows)
        ys_rows = _moe_sparse(tiles, xs_rows, wg_b, wu_b, wd_b)
        xp = _combine(l, pos3, ys_rows, x1, mod_p, fw, final=(l == DEPTH - 1))

    y_prompt = xp.reshape(bp, sp, D)
    y_sample = _final_norm(xs.reshape(bs * ss, D), fw).reshape(bs, ss, D)
    return (y_prompt, y_sample, cp, sp_.reshape(DEPTH, bp, SSD_NH, SSD_HD, SSD_NS), gp,
            cs_, ss_.reshape(DEPTH, bs, SSD_NH, SSD_HD, SSD_NS), gs)
```

```python
import functools

import numpy as np
import jax
import jax.numpy as jnp
from jax import lax
from jax.experimental import pallas as pl
from jax.experimental.pallas import tpu as pltpu

f32 = jnp.float32
bf16 = jnp.bfloat16

D = 1024
DEPTH = 4
SSD_INNER = 2048
SSD_HD = 64
SSD_NH = 32
SSD_NS = 128
SSD_NG = 4
SSD_HPG = 8
SSD_GW = SSD_INNER // SSD_NG
CONV_K = 4
CONV_DIM = SSD_INNER + 2 * SSD_NG * SSD_NS
GLA_NH = 4
GLA_DK = 128
GLA_DV = 256
GLA_KD = GLA_NH * GLA_DK
GLA_VD = GLA_NH * GLA_DV
GLA_RANK = 16
GLA_TAU = 16.0
N_EXP = 16
N_EGRP = 4
EXP_PER_GRP = 4
EXP_FF = 512
N_PAIR = 6
N_CLS = N_EGRP * N_PAIR
CLS_PAD = 32
MOE_EXT = 128
MOE_R = 256
MOE_TM = 512
MOE_DMA_UNROLL = 32
EPS = 1e-6
LANES = 128
TAIL_GLR0 = SSD_NH
MAIN_W = SSD_INNER + CONV_DIM + 2 * GLA_KD + 2 * GLA_VD + 2 * D
VMEM_LIMIT = 56 * 1024 * 1024


def _sigmoid(x):
    return 1.0 / (1.0 + jnp.exp(-x))


def _silu(x):
    return x * _sigmoid(x)


def _softplus(x):
    return jnp.maximum(x, 0.0) + jnp.log1p(jnp.exp(-jnp.abs(x)))


def _log_sigmoid(x):
    return jnp.minimum(x, 0.0) - jnp.log1p(jnp.exp(-jnp.abs(x)))


def _dot(a, b):
    return jnp.dot(a, b, preferred_element_type=f32)


def _dot_nt(a, b):
    return lax.dot_general(a, b, (((1,), (1,)), ((), ())), preferred_element_type=f32)


def _dot_tn(a, b):
    return lax.dot_general(a, b, (((0,), (0,)), ((), ())), preferred_element_type=f32)


def _split2(x):
    hi = x.astype(bf16)
    return hi, (x - hi.astype(f32)).astype(bf16)


def _split3(x):
    hi = x.astype(bf16)
    r1 = x - hi.astype(f32)
    mid = r1.astype(bf16)
    lo = (r1 - mid.astype(f32)).astype(bf16)
    return hi, mid, lo


def _cumsum_rows(x, tri):
    hi, mid, lo = _split3(x)
    if tri.shape[1] % LANES == 0:
        return _dot(jnp.concatenate([tri, tri, tri], axis=1), jnp.concatenate([hi, mid, lo], axis=0))
    return _dot(tri, hi) + _dot(tri, mid) + _dot(tri, lo)


def _tri(n):
    r = lax.broadcasted_iota(jnp.int32, (n, n), 0)
    c = lax.broadcasted_iota(jnp.int32, (n, n), 1)
    return r >= c


def _norm_mod(x, mod_ref, w_ref, i_shift, i_scale, seq3d):
    ms = jnp.mean(x * x, axis=-1, keepdims=True)
    y = x * lax.rsqrt(ms + EPS) * w_ref[...]
    if seq3d:
        sc = mod_ref[:, i_scale:i_scale + 1, :]
        sh = mod_ref[:, i_shift:i_shift + 1, :]
        h = y * (1.0 + sc) + sh
        return h.reshape(h.shape[0] * h.shape[1], h.shape[2])
    sc = mod_ref[i_scale:i_scale + 1, :]
    sh = mod_ref[i_shift:i_shift + 1, :]
    return y * (1.0 + sc) + sh


def _seq_tile(nseq, want):
    return want if nseq % want == 0 else nseq


def _stacked_out(prev, in_specs, args, aliases, out_index):
    if prev is not None:
        in_specs.append(pl.BlockSpec(memory_space=pl.ANY))
        args.append(prev)
        aliases[len(args) - 1] = out_index


def _adaln_kernel(c_ref, w_ref, b_ref, o_ref):
    s = _silu(c_ref[...]).astype(bf16)
    o_ref[...] = _dot(s, w_ref[...].astype(bf16)) + b_ref[...]


def _adaln(c_all, w_ada, b_ada):
    n = c_all.shape[0]
    tn = 1024
    return pl.pallas_call(
        _adaln_kernel,
        grid=(DEPTH, 6 * D // tn),
        in_specs=[
            pl.BlockSpec((n, D), lambda l, j: (0, 0)),
            pl.BlockSpec((None, D, tn), lambda l, j: (l, 0, j)),
            pl.BlockSpec((None, 1, tn), lambda l, j: (l, 0, j)),
        ],
        out_specs=pl.BlockSpec((None, n, tn), lambda l, j: (l, 0, j)),
        out_shape=jax.ShapeDtypeStruct((DEPTH, n, 6 * D), f32),
        compiler_params=pltpu.CompilerParams(
            dimension_semantics=("arbitrary", "arbitrary"), vmem_limit_bytes=VMEM_LIMIT),
        name="adaln",
    )(c_all, w_ada, b_ada.reshape(DEPTH, 1, 6 * D))


IN_TN = 1024
IN_NJ = MAIN_W // IN_TN
IN_SUB = 256
IN_NSUB = IN_TN // IN_SUB
CONV_PAD = 8
KPREV = CONV_K - 1


def _inproj_kernel(*refs, seq3d, has_state, per_seq, T):
    if has_state:
        (x_ref, mod_ref, n1_ref, wm_ref, wt_ref, wg_ref, gb_ref, cw_ref, cb_ref, mb_ref, conv0_ref) = refs[:11]
        rest = refs[11:]
    else:
        (x_ref, mod_ref, n1_ref, wm_ref, wt_ref, wg_ref, gb_ref, cw_ref, cb_ref, mb_ref) = refs[:10]
        conv0_ref = None
        rest = refs[10:]
    (z4_ref, xs4_ref, bc_ref, qkvr_ref, gate_ref, tail_ref, lg_ref, convn_ref,
     h_scr, cscr, carry) = rest[-11:]
    i = pl.program_id(0)
    j = pl.program_id(1)
    tm = h_scr.shape[0]
    ns = cscr.shape[1]

    @pl.when(j == 0)
    def _():
        h = _norm_mod(x_ref[...], mod_ref, n1_ref, 0, 1, seq3d).astype(bf16)
        h_scr[...] = h
        t = _dot(h, wt_ref[...])
        tail_ref[...] = t
        pre = _dot(t.astype(bf16), wg_ref[...]) + gb_ref[...]
        lg_ref[...] = _log_sigmoid(pre) * (1.0 / GLA_TAU)

    def sub_dot(c):
        return _dot(h_scr[...], wm_ref[:, IN_SUB * c:IN_SUB * (c + 1)])

    def conv_silu(a, cj, c):
        cols = slice(IN_SUB * c, IN_SUB * (c + 1))
        gcols = slice(IN_TN * cj + IN_SUB * c, IN_TN * cj + IN_SUB * (c + 1))
        if not has_state:
            prev = jnp.where(i % per_seq == 0, 0.0, carry[cj, c, CONV_PAD - KPREV:CONV_PAD, :])
            cscr[c, 0, CONV_PAD - KPREV:CONV_PAD, :] = prev
            cscr[c, 0, CONV_PAD:2 * CONV_PAD, :] = a[0:CONV_PAD]
            cscr[c, 0, 2 * CONV_PAD:3 * CONV_PAD, :] = a[tm - CONV_PAD:tm]
            w = [cw_ref[k:k + 1, cols] for k in range(CONV_K)]
            out = cb_ref[:, cols] + a * w[KPREV]
            head = cb_ref[:, cols] + a[0:CONV_PAD] * w[KPREV]
            for k in range(KPREV):
                out = out + pltpu.roll(a, KPREV - k, 0) * w[k]
                head = head + cscr[c, 0, CONV_PAD - KPREV + k:2 * CONV_PAD - KPREV + k, :] * w[k]
            out = jnp.concatenate([head, out[CONV_PAD:]], axis=0)
            new_tail = cscr[c, 0, 3 * CONV_PAD - KPREV:3 * CONV_PAD, :]
            convn_ref[0, :, gcols] = new_tail
            carry[cj, c, CONV_PAD - KPREV:CONV_PAD, :] = new_tail
            return _silu(out).astype(bf16)
        a3 = a.reshape(ns, T, IN_SUB)
        prev = conv0_ref[:, :, cols]
        cscr[c, :, CONV_PAD - KPREV:CONV_PAD, :] = prev
        cscr[c, :, CONV_PAD:CONV_PAD + T, :] = a3
        out = cb_ref[:, cols] + a3 * cw_ref[KPREV:KPREV + 1, cols]
        for k in range(KPREV):
            out = out + cscr[c, :, CONV_PAD - KPREV + k:CONV_PAD - KPREV + k + T, :] * cw_ref[k:k + 1, cols]
        new_tail = cscr[c, :, CONV_PAD + T - KPREV:CONV_PAD + T, :]
        convn_ref[:, :, gcols] = new_tail
        return _silu(out).reshape(tm, IN_SUB).astype(bf16)

    per_grp = SSD_GW // IN_SUB

    @pl.when(j < 2)
    def _():
        for c in range(IN_NSUB):
            lanes = slice(IN_SUB * (c % per_grp), IN_SUB * (c % per_grp + 1))
            z4_ref[c // per_grp, :, lanes] = _silu(sub_dot(c)).astype(bf16)

    for cj in range(2):
        @pl.when(j == 2 + cj)
        def _(cj=cj):
            for c in range(IN_NSUB):
                lanes = slice(IN_SUB * (c % per_grp), IN_SUB * (c % per_grp + 1))
                xs4_ref[c // per_grp, :, lanes] = conv_silu(sub_dot(c), cj, c)

    @pl.when(j == 4)
    def _():
        for c in range(IN_NSUB):
            xc = conv_silu(sub_dot(c), 2, c)
            for q in range(IN_SUB // SSD_NS):
                bc_ref[c * (IN_SUB // SSD_NS) + q] = xc[:, SSD_NS * q:SSD_NS * (q + 1)]

    @pl.when(jnp.logical_and(j >= 5, j < 7))
    def _():
        for c in range(IN_NSUB):
            qkvr_ref[:, IN_SUB * c:IN_SUB * (c + 1)] = sub_dot(c).astype(bf16)

    @pl.when(j == 7)
    def _():
        for c in range(IN_NSUB):
            qkvr_ref[:, IN_SUB * c:IN_SUB * (c + 1)] = _silu(sub_dot(c)).astype(bf16)

    @pl.when(j >= 8)
    def _():
        for c in range(IN_NSUB):
            cols = slice(IN_SUB * c, IN_SUB * (c + 1))
            gate_ref[:, cols] = _sigmoid(sub_dot(c) + mb_ref[:, cols]).astype(bf16)


def _inproj(l, x, mod, conv0, convn_prev, w, *, seq3d):
    (n1w, w_main, w_tail, gate_up, gate_b, conv_w, conv_b, merge_b) = w
    has_state = conv0 is not None
    if seq3d:
        nseq, T = x.shape[0], x.shape[1]
        ns = _seq_tile(nseq, 128)
        ntok = nseq * T
        tm = ns * T
        nt = nseq // ns
        per_seq = 1
        x_spec = pl.BlockSpec((ns, T, D), lambda i, j: (i, 0, 0))
        mod_spec = pl.BlockSpec((None, ns, 6, D), lambda i, j: (l, i, 0, 0))
        seq_blk = lambda i: i
    else:
        nseq = mod.shape[1]
        ntok = x.shape[0]
        tm = min(1024, ntok // nseq)
        T = tm
        ns = 1
        nt = ntok // tm
        per_seq = ntok // nseq // tm
        x_spec = pl.BlockSpec((tm, D), lambda i, j: (i, 0))
        mod_spec = pl.BlockSpec((None, None, 6, D), lambda i, j: (l, i // per_seq, 0, 0))
        seq_blk = lambda i: i // per_seq

    def cj(j):
        return jnp.clip(j - 2, 0, 2)

    in_specs = [
        x_spec,
        mod_spec,
        pl.BlockSpec((None, 1, D), lambda i, j: (l, 0, 0)),
        pl.BlockSpec((None, D, IN_TN), lambda i, j: (l, 0, j)),
        pl.BlockSpec((None, D, LANES), lambda i, j: (l, 0, 0)),
        pl.BlockSpec((None, LANES, GLA_KD), lambda i, j: (l, 0, 0)),
        pl.BlockSpec((None, 1, GLA_KD), lambda i, j: (l, 0, 0)),
        pl.BlockSpec((None, CONV_K, IN_TN), lambda i, j: (l, 0, cj(j))),
        pl.BlockSpec((None, 1, IN_TN), lambda i, j: (l, 0, cj(j))),
        pl.BlockSpec((None, 1, IN_TN), lambda i, j: (l, 0, jnp.clip(j - 8, 0, 1))),
    ]
    args = [x, mod, n1w, w_main, w_tail, gate_up, gate_b, conv_w, conv_b, merge_b]
    if has_state:
        in_specs.append(pl.BlockSpec((None, ns, KPREV, IN_TN), lambda i, j: (l, i, 0, cj(j))))
        args.append(conv0)
    aliases = {}
    _stacked_out(convn_prev, in_specs, args, aliases, 7)
    out_shapes = (
        jax.ShapeDtypeStruct((SSD_NG, ntok, SSD_GW), bf16),
        jax.ShapeDtypeStruct((SSD_NG, ntok, SSD_GW), bf16),
        jax.ShapeDtypeStruct((2 * SSD_NG, ntok, SSD_NS), bf16),
        jax.ShapeDtypeStruct((ntok, 2 * GLA_KD + 2 * GLA_VD), bf16),
        jax.ShapeDtypeStruct((ntok, 2 * D), bf16),
        jax.ShapeDtypeStruct((ntok, LANES), f32),
        jax.ShapeDtypeStruct((ntok, GLA_KD), f32),
        jax.ShapeDtypeStruct((DEPTH, nseq, KPREV, CONV_DIM), f32),
    )
    out_specs = (
        pl.BlockSpec((2, tm, SSD_GW), lambda i, j: (jnp.clip(j, 0, 1), i, 0)),
        pl.BlockSpec((2, tm, SSD_GW), lambda i, j: (jnp.clip(j - 2, 0, 1), i, 0)),
        pl.BlockSpec((2 * SSD_NG, tm, SSD_NS), lambda i, j: (0, i, 0)),
        pl.BlockSpec((tm, IN_TN), lambda i, j: (i, jnp.clip(j - 5, 0, 2))),
        pl.BlockSpec((tm, IN_TN), lambda i, j: (i, jnp.clip(j - 8, 0, 1))),
        pl.BlockSpec((tm, LANES), lambda i, j: (i, 0)),
        pl.BlockSpec((tm, GLA_KD), lambda i, j: (i, 0)),
        pl.BlockSpec((None, ns, KPREV, CONV_DIM), lambda i, j: (l, seq_blk(i), 0, 0)),
    )
    return pl.pallas_call(
        functools.partial(_inproj_kernel, seq3d=seq3d, has_state=has_state, per_seq=per_seq, T=T),
        grid=(nt, IN_NJ),
        in_specs=in_specs,
        out_specs=out_specs,
        out_shape=out_shapes,
        input_output_aliases=aliases,
        scratch_shapes=[
            pltpu.VMEM((tm, D), bf16),
            pltpu.VMEM((IN_NSUB, ns, CONV_PAD + T if has_state else 3 * CONV_PAD, IN_SUB), f32),
            pltpu.VMEM((3, IN_NSUB, CONV_PAD, IN_SUB), f32),
        ],
        compiler_params=pltpu.CompilerParams(
            dimension_semantics=("arbitrary", "arbitrary"), vmem_limit_bytes=VMEM_LIMIT),
        name="inproj",
    )(*args)


SSD_CS_PIECES = 2


def _ssd_consts():
    bigsel = np.zeros((SSD_NG, SSD_CS_PIECES * LANES, SSD_HPG * LANES), np.float32)
    e8 = np.zeros((SSD_NG, 2 * LANES, SSD_GW), np.float32)
    for g in range(SSD_NG):
        for j in range(SSD_HPG):
            for k in range(SSD_CS_PIECES):
                bigsel[g, k * LANES + SSD_HPG * g + j, LANES * j:LANES * (j + 1)] = 1.0
            for k in range(2):
                e8[g, k * LANES + SSD_HPG * g + j, SSD_HD * j:SSD_HD * (j + 1)] = 1.0
    return jnp.asarray(bigsel, bf16), jnp.asarray(e8, bf16)


def _ssd_kernel(*refs, L, sb, cps, nchunks, has_state):
    R = sb * L
    xs4_ref, bc_ref, z4_ref, tail_ref = refs[:4]
    k = 4
    ssm0_ref = None
    if has_state:
        ssm0_ref = refs[k]
        k += 1
    dtb_ref, alog_ref, dsk_ref, nw_ref, wbr_ref, bigsel_ref, e8_ref = refs[k:k + 7]
    br_ref, ssmn_ref, cst_scr = refs[-3:]
    c = pl.program_id(1)

    rowi = lax.broadcasted_iota(jnp.int32, (R, R), 0)
    coli = lax.broadcasted_iota(jnp.int32, (R, R), 1)
    mask = rowi >= coli
    if sb > 1:
        sh = L.bit_length() - 1
        mask = jnp.logical_and(mask, (rowi >> sh) == (coli >> sh))
    tri = mask.astype(bf16)
    lane = lax.broadcasted_iota(jnp.int32, (R, LANES), 1)
    lo_half = lane < SSD_HD

    def init():
        if has_state:
            ssmn_ref[...] = ssm0_ref[...]
        else:
            ssmn_ref[...] = jnp.zeros(ssmn_ref.shape, f32)

    if nchunks == 1:
        init()
    else:
        pl.when(c == 0)(init)

    def seq_last(x):
        if sb == 1:
            return x[R - 1:R, :]
        x3 = x.reshape(sb, L, x.shape[-1])
        return jnp.broadcast_to(x3[:, L - 1:L, :], x3.shape).reshape(x.shape)

    def chunk(ci):
        rows = slice(ci * R, (ci + 1) * R)
        dtp = _softplus(tail_ref[rows, :] + dtb_ref[...])
        a = -jnp.exp(alog_ref[...])
        cs = _cumsum_rows(dtp * a, tri)
        cst_scr[ci] = cs.T
        h3 = jnp.concatenate(_split3(cs)[:SSD_CS_PIECES], axis=1)
        d2 = jnp.concatenate(_split2(dtp), axis=1)
        br_ref[rows, :] = jnp.zeros((R, D), f32)

        for g in range(SSD_NG):
            cm = _dot(h3, bigsel_ref[g])
            dt_exp = _dot(d2, e8_ref[g])
            cs_exp = jnp.concatenate(
                [jnp.where(lo_half, cm[:, 2 * LANES * i:2 * LANES * i + LANES],
                           cm[:, 2 * LANES * i + LANES:2 * LANES * (i + 1)]) for i in range(SSD_HPG // 2)],
                axis=1)
            ecs = jnp.exp(cs_exp)
            ce = seq_last(cs_exp)
            xs = xs4_ref[g, rows, :].astype(f32)
            xdt = xs * dt_exp
            xdt_b = xdt.astype(bf16)
            xse = xdt * jnp.exp(ce - cs_exp)
            bg = bc_ref[g, rows, :]
            cg = bc_ref[SSD_NG + g, rows, :]
            cb = _dot_nt(cg, bg)

            pairs = []
            for i in range(SSD_HPG // 2):
                ws = []
                for j in (2 * i, 2 * i + 1):
                    row = cst_scr[ci, pl.ds(SSD_HPG * g + j, 1), :]
                    seg = cm[:, LANES * j:LANES * j + R] - row
                    dec = jnp.exp(jnp.where(mask, seg, -jnp.inf))
                    ws.append((cb * dec).astype(bf16))
                yy = _dot(jnp.concatenate(ws, axis=0), xdt_b[:, LANES * i:LANES * (i + 1)])
                pairs.append(jnp.where(lo_half, yy[:R], yy[R:]))
            y = jnp.concatenate(pairs, axis=1)

            if sb > 1:
                bgf = bg.astype(f32)
                cgf = cg.astype(f32)
            ys_parts = []
            for s in range(sb):
                rs = slice(s * L, (s + 1) * L)
                st = ssmn_ref[s, g]
                b_s = bg if sb == 1 else bgf[rs].astype(bf16)
                c_s = cg if sb == 1 else cgf[rs].astype(bf16)
                ys_parts.append(_dot_nt(c_s, st.astype(bf16)))
                upd = _dot_tn(xse[rs].astype(bf16), b_s)
                e_end = jnp.exp(ce[s * L:s * L + 1, :])
                ssmn_ref[s, g] = jnp.concatenate(
                    [st[SSD_HD * j:SSD_HD * (j + 1)] * e_end[:, SSD_HD * j:SSD_HD * j + 1]
                     + upd[SSD_HD * j:SSD_HD * (j + 1)] for j in range(SSD_HPG)], axis=0)
            ys = ys_parts[0] if sb == 1 else jnp.concatenate(ys_parts, axis=0)
            y = y + ys * ecs + dsk_ref[g] * xs

            yg = y * z4_ref[g, rows, :].astype(f32)
            yn = yg * lax.rsqrt(jnp.mean(yg * yg, axis=-1, keepdims=True) + EPS) * nw_ref[g]
            br_ref[rows, :] += _dot(yn.astype(bf16), wbr_ref[g])

    for ci in range(cps):
        chunk(ci)


def _ssd(l, xs4, bc, z4, tail, ssm0, ssmn_prev, w, *, nseq, L, sb, nchunks, cps=1):
    has_state = ssm0 is not None
    dtb, alog, dsk, nw, wbr, bigsel, e8 = w
    ntok = tail.shape[0]
    chunk_rows = sb * L
    R = cps * chunk_rows
    assert nchunks % cps == 0
    nchunks = nchunks // cps
    if nchunks == 1:
        rblk = lambda b, c: b
    else:
        rblk = lambda b, c: b * nchunks + c
    in_specs = [
        pl.BlockSpec((SSD_NG, R, SSD_GW), lambda b, c: (0, rblk(b, c), 0)),
        pl.BlockSpec((2 * SSD_NG, R, SSD_NS), lambda b, c: (0, rblk(b, c), 0)),
        pl.BlockSpec((SSD_NG, R, SSD_GW), lambda b, c: (0, rblk(b, c), 0)),
        pl.BlockSpec((R, LANES), lambda b, c: (rblk(b, c), 0)),
    ]
    args = [xs4, bc, z4, tail]
    if has_state:
        in_specs.append(pl.BlockSpec((None, sb, SSD_NG, SSD_GW, SSD_NS), lambda b, c: (l, b, 0, 0, 0)))
        args.append(ssm0)
    in_specs += [
        pl.BlockSpec((None, 1, LANES), lambda b, c: (l, 0, 0)),
        pl.BlockSpec((None, 1, LANES), lambda b, c: (l, 0, 0)),
        pl.BlockSpec((None, SSD_NG, 1, SSD_GW), lambda b, c: (l, 0, 0, 0)),
        pl.BlockSpec((None, SSD_NG, 1, SSD_GW), lambda b, c: (l, 0, 0, 0)),
        pl.BlockSpec((None, SSD_NG, SSD_GW, D), lambda b, c: (l, 0, 0, 0)),
        pl.BlockSpec((SSD_NG, SSD_CS_PIECES * LANES, SSD_HPG * LANES), lambda b, c: (0, 0, 0)),
        pl.BlockSpec((SSD_NG, 2 * LANES, SSD_GW), lambda b, c: (0, 0, 0)),
    ]
    args += [dtb, alog, dsk, nw, wbr, bigsel, e8]
    aliases = {}
    _stacked_out(ssmn_prev, in_specs, args, aliases, 1)
    return pl.pallas_call(
        functools.partial(_ssd_kernel, L=L, sb=sb, cps=cps, nchunks=nchunks, has_state=has_state),
        grid=(nseq // sb, nchunks),
        in_specs=in_specs,
        out_specs=(
            pl.BlockSpec((R, D), lambda b, c: (rblk(b, c), 0)),
            pl.BlockSpec((None, sb, SSD_NG, SSD_GW, SSD_NS), lambda b, c: (l, b, 0, 0, 0)),
        ),
        out_shape=(
            jax.ShapeDtypeStruct((ntok, D), f32),
            jax.ShapeDtypeStruct((DEPTH, nseq, SSD_NG, SSD_GW, SSD_NS), f32),
        ),
        input_output_aliases=aliases,
        scratch_shapes=[pltpu.VMEM((cps, LANES, chunk_rows), f32)],
        compiler_params=pltpu.CompilerParams(
            dimension_semantics=("arbitrary", "arbitrary"), vmem_limit_bytes=VMEM_LIMIT),
        name="ssd",
    )(*args)


GLA_C2 = 4


def _gla_consts(L, C1, sb):
    R = sb * L
    nb2 = C1 // GLA_C2
    i = np.arange(R)[:, None]
    s = np.arange(R)[None, :]
    same_seq = (i // L) == (s // L)
    mats = [(s <= i) & (s > i - d) & same_seq for d in range(1, GLA_C2)]
    mats.append((s <= i) & (s >= (i // GLA_C2) * GLA_C2))
    for r in range(1, nb2):
        mats.append((s > i) & (s <= (i // C1) * C1 + GLA_C2 * r - 1))
    shifts = [(s == i - d) & same_seq for d in range(1, GLA_C2)]
    return (jnp.asarray(np.concatenate(mats, 0), bf16), jnp.asarray(np.concatenate(shifts, 0), bf16))


def _gla_kernel(*refs, L, C1, sb, cps, nchunks, has_state):
    R = sb * L
    nb1 = L // C1
    nb2 = C1 // GLA_C2
    qkvr_ref, lg_ref = refs[:2]
    k = 2
    gla0_ref = None
    if has_state:
        gla0_ref = refs[k]
        k += 1
    gnw_ref, wbr_ref, sums_ref, shift_ref = refs[k:k + 4]
    br_ref, glan_ref = refs[-2:]
    c = pl.program_id(1)

    rowi = lax.broadcasted_iota(jnp.int32, (R, R), 0)
    coli = lax.broadcasted_iota(jnp.int32, (R, R), 1)
    shl = L.bit_length() - 1
    same_seq = (rowi >> shl) == (coli >> shl)
    tri = jnp.logical_and(rowi >= coli, same_seq).astype(bf16)
    sh1 = C1.bit_length() - 1
    same_blk1 = (rowi >> sh1) == (coli >> sh1)
    sub_i = (rowi & (C1 - 1)) >> (GLA_C2.bit_length() - 1)
    off_i = rowi & (GLA_C2 - 1)
    rowl = lax.broadcasted_iota(jnp.int32, (R, 1), 0)
    sub_l = (rowl & (C1 - 1)) >> (GLA_C2.bit_length() - 1)
    blk_i = (rowi & (L - 1)) >> sh1
    rel_l = rowl & (L - 1)
    pad_rows = [jnp.zeros((LANES - R, GLA_DK), bf16)] if R < LANES else []

    def init():
        if has_state:
            glan_ref[...] = gla0_ref[...]
        else:
            glan_ref[...] = jnp.zeros(glan_ref.shape, f32)

    if nchunks == 1:
        init()
    else:
        pl.when(c == 0)(init)

    def seq_last(x):
        if sb == 1:
            return x[R - 1:R, :]
        x3 = x.reshape(sb, L, x.shape[-1])
        return jnp.broadcast_to(x3[:, L - 1:L, :], x3.shape).reshape(x.shape)

    def chunk(ci):
        def ld(ref, cols):
            if len(ref.shape) == 3:
                return ref[:, ci * L:(ci + 1) * L, cols].reshape(R, -1)
            return ref[:, cols]

        lg = ld(lg_ref, slice(None))
        g = _cumsum_rows(lg, tri)
        g_end = seq_last(g)
        lg_hi, lg_lo = _split2(lg)
        if R % LANES == 0:
            sums = _dot(jnp.concatenate([sums_ref[...], sums_ref[...]], axis=1),
                        jnp.concatenate([lg_hi, lg_lo], axis=0))
        else:
            sums = _dot(sums_ref[...], lg_hi) + _dot(sums_ref[...], lg_lo)
        a_d = [sums[R * (d - 1):R * d] for d in range(1, GLA_C2)]
        a_sub = sums[R * (GLA_C2 - 1):R * GLA_C2]
        b_sub = [sums[R * (GLA_C2 - 1 + r):R * (GLA_C2 + r)] for r in range(1, nb2)]
        q = ld(qkvr_ref, slice(0, GLA_KD)).astype(f32)
        kb = ld(qkvr_ref, slice(GLA_KD, 2 * GLA_KD))
        k_ = kb.astype(f32)
        k_sh = [_dot(shift_ref[R * (d - 1):R * d, :], kb) for d in range(1, GLA_C2)]

        acc = jnp.zeros((R, D), f32)
        for h in range(GLA_NH):
            kl = slice(GLA_DK * h, GLA_DK * (h + 1))
            vl = slice(2 * GLA_KD + GLA_DV * h, 2 * GLA_KD + GLA_DV * (h + 1))
            rl = slice(2 * GLA_KD + GLA_VD + GLA_DV * h, 2 * GLA_KD + GLA_VD + GLA_DV * (h + 1))
            gh = g[:, kl]
            qh = q[:, kl]
            kh = k_[:, kl]
            vb = ld(qkvr_ref, vl)
            qg = qh * jnp.exp(gh)
            kd = kh * jnp.exp(g_end[:, kl] - gh)

            if sb == 1:
                st = glan_ref[0, h]
                o = _dot(qg.astype(bf16), st.astype(bf16))
                e_col = jnp.exp(jnp.broadcast_to(g_end[0:1, kl], (8, GLA_DK))).T[:, 0:1]
                glan_ref[0, h] = st * e_col + _dot_tn(kd.astype(bf16), vb)
            else:
                vf = vb.astype(f32)
                parts = []
                for s in range(sb):
                    rs = slice(s * L, (s + 1) * L)
                    st = glan_ref[s, h]
                    parts.append(_dot(qg[rs].astype(bf16), st.astype(bf16)))
                    e_col = jnp.exp(jnp.broadcast_to(g_end[s * L:s * L + 1, kl], (8, GLA_DK))).T[:, 0:1]
                    glan_ref[s, h] = st * e_col + _dot_tn(kd[rs].astype(bf16), vf[rs].astype(bf16))
                o = jnp.concatenate(parts, axis=0)

            att = jnp.zeros((R, R), f32)
            if nb1 > 1:
                def bnd(s, i):
                    if i == 0:
                        return jnp.zeros((1, GLA_DK), f32)
                    return gh[s * L + C1 * i - 1:s * L + C1 * i, :]

                gblk = jnp.concatenate([jnp.broadcast_to(bnd(s, i), (C1, GLA_DK))
                                        for s in range(sb) for i in range(nb1)], axis=0)
                qt = (qh * jnp.exp(gh - gblk)).astype(bf16)
                kts = []
                for i in range(1, nb1):
                    gb = jnp.concatenate([jnp.broadcast_to(bnd(s, i), (L, GLA_DK)) for s in range(sb)], axis=0)
                    kt = kh * jnp.exp(jnp.where(rel_l < C1 * i, gb - gh, -jnp.inf))
                    kts += [kt.astype(bf16)] + pad_rows
                out1 = _dot_nt(qt, jnp.concatenate(kts, axis=0))
                for i in range(1, nb1):
                    att = att + jnp.where(jnp.logical_and(blk_i == i, same_seq),
                                          out1[:, LANES * (i - 1):LANES * (i - 1) + R], 0.0)

            qt2 = (qh * jnp.exp(a_sub[:, kl])).astype(bf16)
            kts = []
            for r in range(1, nb2):
                kt = kh * jnp.exp(jnp.where(sub_l < r, b_sub[r - 1][:, kl], -jnp.inf))
                kts += [kt.astype(bf16)] + pad_rows
            out2 = _dot_nt(qt2, jnp.concatenate(kts, axis=0))
            for r in range(1, nb2):
                att = att + jnp.where(jnp.logical_and(sub_i == r, same_blk1),
                                      out2[:, LANES * (r - 1):LANES * (r - 1) + R], 0.0)

            for d in range(GLA_C2):
                t = qh * kh if d == 0 else qh * jnp.exp(a_d[d - 1][:, kl]) * k_sh[d - 1][:, kl]
                band = jnp.sum(t, axis=-1, keepdims=True)
                hit = jnp.logical_and(off_i >= d, coli == rowi - d)
                att = att + jnp.where(hit, band, 0.0)
            o = o + _dot(att.astype(bf16), vb)

            on = o * lax.rsqrt(jnp.mean(o * o, axis=-1, keepdims=True) + EPS) * gnw_ref[...]
            og = on * ld(qkvr_ref, rl).astype(f32)
            acc = acc + _dot(og.astype(bf16), wbr_ref[GLA_DV * h:GLA_DV * (h + 1), :])
        if len(br_ref.shape) == 3:
            br_ref[:, ci * L:(ci + 1) * L, :] = acc.reshape(sb, L, D)
        else:
            br_ref[...] = acc

    for ci in range(cps):
        chunk(ci)


def _gla(l, qkvr, lg, gla0, glan_prev, w, *, nseq, L, C1, sb, nchunks, cps=1):
    has_state = gla0 is not None
    gnw, wbr = w
    sums, shifts = _gla_consts(L, C1, sb)
    ntok = qkvr.shape[0]
    rb = sb * L
    wq = 2 * GLA_KD + 2 * GLA_VD
    flat = nchunks == 1
    if flat:
        row = lambda b, c: (b, 0)
        blk = lambda w_: (rb, w_)
        view = lambda a: a
    else:
        assert nchunks % cps == 0
        nchunks = nchunks // cps
        row = lambda b, c: (b, c, 0)
        blk = lambda w_: (sb, cps * L, w_)
        view = lambda a: a.reshape(nseq, ntok // nseq, a.shape[-1])
    in_specs = [
        pl.BlockSpec(blk(wq), row),
        pl.BlockSpec(blk(GLA_KD), row),
    ]
    args = [view(qkvr), view(lg)]
    if has_state:
        in_specs.append(pl.BlockSpec((None, sb, GLA_NH, GLA_DK, GLA_DV), lambda b, c: (l, b, 0, 0, 0)))
        args.append(gla0)
    in_specs += [
        pl.BlockSpec((None, 1, GLA_DV), lambda b, c: (l, 0, 0)),
        pl.BlockSpec((None, GLA_VD, D), lambda b, c: (l, 0, 0)),
        pl.BlockSpec(sums.shape, lambda b, c: (0, 0)),
        pl.BlockSpec(shifts.shape, lambda b, c: (0, 0)),
    ]
    args += [gnw, wbr, sums, shifts]
    aliases = {}
    _stacked_out(glan_prev, in_specs, args, aliases, 1)
    br_shape = (ntok, D) if flat else (nseq, ntok // nseq, D)
    br, glan = pl.pallas_call(
        functools.partial(_gla_kernel, L=L, C1=C1, sb=sb, cps=cps, nchunks=nchunks, has_state=has_state),
        grid=(nseq // sb, nchunks),
        in_specs=in_specs,
        out_specs=(
            pl.BlockSpec(blk(D), row),
            pl.BlockSpec((None, sb, GLA_NH, GLA_DK, GLA_DV), lambda b, c: (l, b, 0, 0, 0)),
        ),
        out_shape=(
            jax.ShapeDtypeStruct(br_shape, f32),
            jax.ShapeDtypeStruct((DEPTH, nseq, GLA_NH, GLA_DK, GLA_DV), f32),
        ),
        input_output_aliases=aliases,
        compiler_params=pltpu.CompilerParams(
            dimension_semantics=("arbitrary", "arbitrary"), vmem_limit_bytes=VMEM_LIMIT),
        name="gla",
    )(*args)
    return br.reshape(ntok, D), glan


def _top2_sum(a, b, c, d):
    hi1, lo1 = jnp.maximum(a, b), jnp.minimum(a, b)
    hi2, lo2 = jnp.maximum(c, d), jnp.minimum(c, d)
    return jnp.maximum(hi1, hi2) + jnp.maximum(jnp.minimum(hi1, hi2), jnp.maximum(lo1, lo2))


def _route_rows(sig, biased):
    gsc = [_top2_sum(*biased[EXP_PER_GRP * g:EXP_PER_GRP * (g + 1)]) for g in range(N_EGRP)]
    best = jnp.zeros_like(gsc[0], dtype=jnp.int32)
    m = gsc[0]
    for g in range(1, N_EGRP):
        better = gsc[g] > m
        best = jnp.where(better, g, best)
        m = jnp.where(better, gsc[g], m)
    masked = [jnp.where(best == (e // EXP_PER_GRP), biased[e], -jnp.inf) for e in range(N_EXP)]

    def first_argmax(vals):
        idx = jnp.zeros_like(best)
        mx = vals[0]
        for e in range(1, N_EXP):
            better = vals[e] > mx
            idx = jnp.where(better, e, idx)
            mx = jnp.where(better, vals[e], mx)
        return idx

    i1 = first_argmax(masked)
    i2 = first_argmax([jnp.where(i1 == e, -jnp.inf, masked[e]) for e in range(N_EXP)])
    w1 = sum(jnp.where(i1 == e, sig[e], 0.0) for e in range(N_EXP))
    w2 = sum(jnp.where(i2 == e, sig[e], 0.0) for e in range(N_EXP))
    den = w1 + w2
    return best, i1, i2, w1 / den, w2 / den


def _merge_kernel(*refs, seq3d, sparse):
    (x_ref, bra_ref, brb_ref, gate_ref, mod_ref, wo_ref, n2_ref, wr_hi_ref, wr_lo_ref, rb_ref) = refs[:10]
    mixed_in = (gate_ref[:, 0:D].astype(f32) * bra_ref[...]
                + gate_ref[:, D:2 * D].astype(f32) * brb_ref[...])
    mixed = _dot(mixed_in.astype(bf16), wo_ref[...])
    x = x_ref[...]
    if seq3d:
        x1 = x + mod_ref[:, 2:3, :] * mixed.reshape(x.shape)
    else:
        x1 = x + mod_ref[2:3, :] * mixed
    refs[10][...] = x1
    h2 = _norm_mod(x1, mod_ref, n2_ref, 3, 4, seq3d)
    hi, lo = _split2(h2)
    logits = _dot(hi, wr_hi_ref[...]) + (_dot(hi, wr_lo_ref[...]) + _dot(lo, wr_hi_ref[...]))
    tm = logits.shape[0]
    lt = logits.T
    sig_all = _sigmoid(lt[0:N_EXP, :])
    bias_all = sig_all + rb_ref[...]
    sig = [sig_all[e:e + 1, :] for e in range(N_EXP)]
    biased = [bias_all[e:e + 1, :] for e in range(N_EXP)]
    best, i1, i2, w1, w2 = _route_rows(sig, biased)
    if not sparse:
        h2_ref, comb_ref = refs[11:13]
        h2_ref[...] = h2.astype(bf16)
        comb = [jnp.where(i1 == e, w1, 0.0) + jnp.where(i2 == e, w2, 0.0) for e in range(N_EXP)]
        comb_t = jnp.concatenate(comb + [jnp.zeros((LANES - N_EXP, tm), f32)], axis=0)
        comb_ref[...] = comb_t.T
        return

    h2x_ref, plan_ref, counts_ref, base_scr = refs[11:15]
    swap = i1 > i2
    wa = jnp.where(swap, w2, w1)
    wb = jnp.where(swap, w1, w2)
    a = jnp.minimum(i1, i2) & (EXP_PER_GRP - 1)
    b = jnp.maximum(i1, i2) & (EXP_PER_GRP - 1)
    pair = jnp.where(a == 0, b - 1, jnp.where(a == 1, b + 1, 5))
    cls = best * N_PAIR + pair
    sub = lax.broadcasted_iota(jnp.int32, (CLS_PAD, tm), 0)
    onehot = (sub == cls).astype(f32)
    before = (lax.broadcasted_iota(jnp.int32, (tm, tm), 0)
              < lax.broadcasted_iota(jnp.int32, (tm, tm), 1)).astype(bf16)
    prefix = _dot(onehot.astype(bf16), before)

    @pl.when(pl.program_id(0) == 0)
    def _():
        base_scr[...] = jnp.zeros_like(base_scr)

    base = base_scr[...]
    rank = jnp.sum(onehot * (prefix + base[:, 0:1]), axis=0, keepdims=True)
    base = base + jnp.sum(onehot, axis=1, keepdims=True)
    base_scr[...] = base
    counts_ref[...] = base
    plan_ref[...] = jnp.concatenate(
        [cls, rank.astype(jnp.int32), jnp.zeros((6, tm), jnp.int32)], axis=0)
    ext_t = jnp.concatenate([wa, wb, jnp.zeros((MOE_EXT - 2, tm), f32)], axis=0)
    h2x_ref[:, 0:D] = h2
    h2x_ref[:, D:D + MOE_EXT] = ext_t.T


def _merge(l, x, bra, brb, gate, mod, wo, n2w, wr_hi, wr_lo, rb, *, seq3d, sparse):
    if seq3d:
        ns = _seq_tile(x.shape[0], 32)
        ntok = x.shape[0] * x.shape[1]
        tm = ns * x.shape[1]
        nt = x.shape[0] // ns
        x_spec = pl.BlockSpec((ns, x.shape[1], D), lambda i: (i, 0, 0))
        mod_spec = pl.BlockSpec((None, ns, 6, D), lambda i: (l, i, 0, 0))
    else:
        ntok = x.shape[0]
        nseq = mod.shape[1]
        tm = min(512, ntok // nseq)
        nt = ntok // tm
        per_seq = ntok // nseq // tm
        x_spec = pl.BlockSpec((tm, D), lambda i: (i, 0))
        mod_spec = pl.BlockSpec((None, None, 6, D), lambda i: (l, i // per_seq, 0, 0))
    row = lambda i: (i, 0)
    const = lambda i: (0, 0)
    if sparse:
        out_specs = (x_spec, pl.BlockSpec((tm, D + MOE_EXT), row), pl.BlockSpec((8, tm), lambda i: (0, i)),
                     pl.BlockSpec((CLS_PAD, LANES), const))
        out_shape = (jax.ShapeDtypeStruct(x.shape, f32), jax.ShapeDtypeStruct((ntok, D + MOE_EXT), f32),
                     jax.ShapeDtypeStruct((8, ntok), jnp.int32), jax.ShapeDtypeStruct((CLS_PAD, LANES), f32))
        scratch = [pltpu.VMEM((CLS_PAD, LANES), f32)]
    else:
        out_specs = (x_spec, pl.BlockSpec((tm, D), row), pl.BlockSpec((tm, LANES), row))
        out_shape = (jax.ShapeDtypeStruct(x.shape, f32), jax.ShapeDtypeStruct((ntok, D), bf16),
                     jax.ShapeDtypeStruct((ntok, LANES), f32))
        scratch = []
    return pl.pallas_call(
        functools.partial(_merge_kernel, seq3d=seq3d, sparse=sparse),
        grid=(nt,),
        in_specs=[
            x_spec,
            pl.BlockSpec((tm, D), row),
            pl.BlockSpec((tm, D), row),
            pl.BlockSpec((tm, 2 * D), row),
            mod_spec,
            pl.BlockSpec((None, D, D), lambda i: (l, 0, 0)),
            pl.BlockSpec((None, 1, D), lambda i: (l, 0, 0)),
            pl.BlockSpec((D, LANES), const),
            pl.BlockSpec((D, LANES), const),
            pl.BlockSpec((N_EXP, 1), const),
        ],
        out_specs=out_specs,
        out_shape=out_shape,
        scratch_shapes=scratch,
        compiler_params=pltpu.CompilerParams(
            dimension_semantics=("arbitrary",), vmem_limit_bytes=VMEM_LIMIT),
        name="merge",
    )(x, bra, brb, gate, mod, wo, n2w, wr_hi, wr_lo, rb)


def _dispatch_kernel(pos_ref, src_ref, buf_in, xs_hbm, sem):
    del buf_in
    tm = src_ref.shape[0]

    def issue(t, c):
        p = pos_ref[0, t]
        pltpu.make_async_copy(src_ref.at[pl.ds(t, 1), :], xs_hbm.at[pl.ds(p, 1), :], sem).start()
        return c

    lax.fori_loop(0, tm, issue, 0, unroll=MOE_DMA_UNROLL)

    def drain(t, c):
        pltpu.make_async_copy(src_ref.at[pl.ds(0, 1), :], xs_hbm.at[pl.ds(0, 1), :], sem).wait()
        return c

    lax.fori_loop(0, tm, drain, 0, unroll=MOE_DMA_UNROLL)


def _dispatch(pos3, h2x, xs_buf):
    nt, _, tm = pos3.shape
    return pl.pallas_call(
        _dispatch_kernel,
        grid=(nt,),
        in_specs=[
            pl.BlockSpec((None, 1, tm), lambda i: (i, 0, 0), memory_space=pltpu.SMEM),
            pl.BlockSpec((tm, D + MOE_EXT), lambda i: (i, 0)),
            pl.BlockSpec(memory_space=pl.ANY),
        ],
        out_specs=pl.BlockSpec(memory_space=pl.ANY),
        out_shape=jax.ShapeDtypeStruct(xs_buf.shape, f32),
        input_output_aliases={2: 0},
        scratch_shapes=[pltpu.SemaphoreType.DMA(())],
        compiler_params=pltpu.CompilerParams(dimension_semantics=("arbitrary",)),
        name="dispatch",
    )(pos3, h2x, xs_buf)


def _moe_sparse_kernel(src_ref, valid_ref, ea_ref, eb_ref, xs_ref,
                       wga_ref, wua_ref, wda_ref, wgb_ref, wub_ref, wdb_ref, ys_ref):
    del src_ref, ea_ref, eb_ref
    t = pl.program_id(0)

    @pl.when(valid_ref[t] == 1)
    def _():
        x = xs_ref[...]
        h = x[:, 0:D].astype(bf16)

        def ffn(wg, wu, wd):
            u = (_silu(_dot(h, wg[...])) * _dot(h, wu[...])).astype(bf16)
            return _dot(u, wd[...])

        ys_ref[...] = (x[:, D:D + 1] * ffn(wga_ref, wua_ref, wda_ref)
                       + x[:, D + 1:D + 2] * ffn(wgb_ref, wub_ref, wdb_ref))

    @pl.when(valid_ref[pl.program_id(0)] == 0)
    def _():
        ys_ref[...] = jnp.zeros(ys_ref.shape, f32)


def _moe_sparse(tiles, xs, wg, wu, wd):
    src, valid, ea, eb = tiles
    nrow = xs.shape[0]
    ntile = nrow // MOE_R
    wa_map = lambda t, src, valid, ea, eb: (ea[t], 0, 0)
    wb_map = lambda t, src, valid, ea, eb: (eb[t], 0, 0)
    row_map = lambda t, src, valid, ea, eb: (src[t], 0)
    return pl.pallas_call(
        _moe_sparse_kernel,
        grid_spec=pltpu.PrefetchScalarGridSpec(
            num_scalar_prefetch=4,
            grid=(ntile,),
            in_specs=[
                pl.BlockSpec((MOE_R, D + MOE_EXT), row_map),
                pl.BlockSpec((None, D, EXP_FF), wa_map),
                pl.BlockSpec((None, D, EXP_FF), wa_map),
                pl.BlockSpec((None, EXP_FF, D), wa_map),
                pl.BlockSpec((None, D, EXP_FF), wb_map),
                pl.BlockSpec((None, D, EXP_FF), wb_map),
                pl.BlockSpec((None, EXP_FF, D), wb_map),
            ],
            out_specs=pl.BlockSpec((MOE_R, D), lambda t, src, valid, ea, eb: (t, 0)),
        ),
        out_shape=jax.ShapeDtypeStruct((nrow, D), f32),
        compiler_params=pltpu.CompilerParams(
            dimension_semantics=("arbitrary",), vmem_limit_bytes=VMEM_LIMIT),
        name="moe_sparse",
    )(src, valid, ea, eb, xs, wg, wu, wd, wg, wu, wd)


def _combine_kernel(pos_ref, ys_hbm, x1_ref, mod_ref, fw_ref, x2_ref, buf, sem, *, final):
    tm = x1_ref.shape[0]

    def issue(t, c):
        p = pos_ref[0, t]
        pltpu.make_async_copy(ys_hbm.at[pl.ds(p, 1), :], buf.at[pl.ds(t, 1), :], sem).start()
        return c

    lax.fori_loop(0, tm, issue, 0, unroll=MOE_DMA_UNROLL)

    def drain(t, c):
        pltpu.make_async_copy(ys_hbm.at[pl.ds(0, 1), :], buf.at[pl.ds(0, 1), :], sem).wait()
        return c

    lax.fori_loop(0, tm, drain, 0, unroll=MOE_DMA_UNROLL)
    x2 = x1_ref[...] + mod_ref[5:6, :] * buf[...]
    if final:
        x2 = x2 * lax.rsqrt(jnp.mean(x2 * x2, axis=-1, keepdims=True) + EPS) * fw_ref[...]
    x2_ref[...] = x2


def _combine(l, pos3, ys, x1, mod, fw, *, final):
    nt, _, tm = pos3.shape
    ntok = x1.shape[0]
    per_seq = ntok // mod.shape[1] // tm
    return pl.pallas_call(
        functools.partial(_combine_kernel, final=final),
        grid=(nt,),
        in_specs=[
            pl.BlockSpec((None, 1, tm), lambda i: (i, 0, 0), memory_space=pltpu.SMEM),
            pl.BlockSpec(memory_space=pl.ANY),
            pl.BlockSpec((tm, D), lambda i: (i, 0)),
            pl.BlockSpec((None, None, 6, D), lambda i: (l, i // per_seq, 0, 0)),
            pl.BlockSpec((1, D), lambda i: (0, 0)),
        ],
        out_specs=pl.BlockSpec((tm, D), lambda i: (i, 0)),
        out_shape=jax.ShapeDtypeStruct((ntok, D), f32),
        scratch_shapes=[pltpu.VMEM((tm, D), f32), pltpu.SemaphoreType.DMA(())],
        compiler_params=pltpu.CompilerParams(dimension_semantics=("arbitrary",)),
        name="combine",
    )(pos3, ys, x1, mod, fw)


def _moe_plan(plan, counts, ntile):
    cnt = counts[:N_CLS, 0].astype(jnp.int32)
    tiles_per = (cnt + (MOE_R - 1)) // MOE_R
    tstart = jnp.cumsum(tiles_per) - tiles_per
    total = jnp.sum(tiles_per)
    pos = tstart[plan[0]] * MOE_R + plan[1]
    t = jnp.arange(ntile, dtype=jnp.int32)
    src = jnp.minimum(t, total - 1)
    tcls = jnp.sum((src[:, None] >= tstart[None, :]).astype(jnp.int32), axis=1) - 1
    grp = tcls // N_PAIR
    pair = tcls % N_PAIR
    pa = jnp.asarray(np.array([0, 0, 0, 1, 1, 2], np.int32))
    pb = jnp.asarray(np.array([1, 2, 3, 2, 3, 3], np.int32))
    ea = grp * EXP_PER_GRP + pa[pair]
    eb = grp * EXP_PER_GRP + pb[pair]
    valid = (t < total).astype(jnp.int32)
    return pos, (src, valid, ea, eb)


def _moe_kernel(h2_ref, comb_ref, wg_ref, wu_ref, wd_ref, x1_ref, mod_ref,
                x2_ref, wgb_ref, wub_ref, wdb_ref, acc_scr, *, seq3d):
    e = pl.program_id(1)

    @pl.when(e == 0)
    def _():
        acc_scr[...] = jnp.zeros_like(acc_scr)

    wgb_ref[...] = wg_ref[...].astype(bf16)
    wub_ref[...] = wu_ref[...].astype(bf16)
    wdb_ref[...] = wd_ref[...].astype(bf16)
    h = h2_ref[...]
    a = _dot(h, wgb_ref[...])
    b = _dot(h, wub_ref[...])
    t = (_silu(a) * b).astype(bf16)
    ye = _dot(t, wdb_ref[...])
    lane = lax.broadcasted_iota(jnp.int32, comb_ref.shape, 1)
    w = jnp.sum(jnp.where(lane == e, comb_ref[...], 0.0), axis=-1, keepdims=True)
    acc_scr[...] += w * ye

    @pl.when(e == N_EXP - 1)
    def _():
        x1 = x1_ref[...]
        if seq3d:
            x2_ref[...] = x1 + mod_ref[:, 5:6, :] * acc_scr[...].reshape(x1.shape)
        else:
            x2_ref[...] = x1 + mod_ref[5:6, :] * acc_scr[...]


def _moe(l, h2, comb, wg, wu, wd, x1, mod, *, seq3d):
    if seq3d:
        ns = x1.shape[0]
        ntok = ns * x1.shape[1]
        tm = ntok
        nt = 1
        x_spec = pl.BlockSpec((ns, x1.shape[1], D), lambda i, e: (0, 0, 0))
        mod_spec = pl.BlockSpec((None, ns, 6, D), lambda i, e: (l, 0, 0, 0))
    else:
        ntok = x1.shape[0]
        nseq = mod.shape[1]
        tm = min(1024, ntok // nseq)
        nt = ntok // tm
        per_seq = ntok // nseq // tm
        x_spec = pl.BlockSpec((tm, D), lambda i, e: (i, 0))
        mod_spec = pl.BlockSpec((None, None, 6, D), lambda i, e: (l, i // per_seq, 0, 0))
    assert nt == 1, "the bf16 weight copies are written once per expert"
    row = lambda i, e: (i, 0)
    return pl.pallas_call(
        functools.partial(_moe_kernel, seq3d=seq3d),
        grid=(nt, N_EXP),
        in_specs=[
            pl.BlockSpec((tm, D), row),
            pl.BlockSpec((tm, LANES), row),
            pl.BlockSpec((None, None, D, EXP_FF), lambda i, e: (l, e, 0, 0)),
            pl.BlockSpec((None, None, D, EXP_FF), lambda i, e: (l, e, 0, 0)),
            pl.BlockSpec((None, None, EXP_FF, D), lambda i, e: (l, e, 0, 0)),
            x_spec,
            mod_spec,
        ],
        out_specs=(
            x_spec,
            pl.BlockSpec((None, D, EXP_FF), lambda i, e: (e, 0, 0)),
            pl.BlockSpec((None, D, EXP_FF), lambda i, e: (e, 0, 0)),
            pl.BlockSpec((None, EXP_FF, D), lambda i, e: (e, 0, 0)),
        ),
        out_shape=(
            jax.ShapeDtypeStruct(x1.shape, f32),
            jax.ShapeDtypeStruct((N_EXP, D, EXP_FF), bf16),
            jax.ShapeDtypeStruct((N_EXP, D, EXP_FF), bf16),
            jax.ShapeDtypeStruct((N_EXP, EXP_FF, D), bf16),
        ),
        scratch_shapes=[pltpu.VMEM((tm, D), f32)],
        compiler_params=pltpu.CompilerParams(
            dimension_semantics=("arbitrary", "arbitrary"), vmem_limit_bytes=VMEM_LIMIT),
        name="moe",
    )(h2, comb, wg, wu, wd, x1, mod)


def _final_norm_kernel(x_ref, w_ref, o_ref):
    x = x_ref[...]
    o_ref[...] = x * lax.rsqrt(jnp.mean(x * x, axis=-1, keepdims=True) + EPS) * w_ref[...]


def _final_norm(x2d, w):
    ntok = x2d.shape[0]
    tm = min(1024, ntok)
    return pl.pallas_call(
        _final_norm_kernel,
        grid=(ntok // tm,),
        in_specs=[pl.BlockSpec((tm, D), lambda i: (i, 0)), pl.BlockSpec((1, D), lambda i: (0, 0))],
        out_specs=pl.BlockSpec((tm, D), lambda i: (i, 0)),
        out_shape=jax.ShapeDtypeStruct((ntok, D), f32),
        compiler_params=pltpu.CompilerParams(dimension_semantics=("arbitrary",)),
        name="final_norm",
    )(x2d, w)


def _prep_in_weights(w_in):
    o = 0
    z = w_in[:, :, o:o + SSD_INNER]; o += SSD_INNER
    xbc = w_in[:, :, o:o + CONV_DIM]; o += CONV_DIM
    dt = w_in[:, :, o:o + SSD_NH]; o += SSD_NH
    q = w_in[:, :, o:o + GLA_KD] * (GLA_DK ** -0.5); o += GLA_KD
    kvr = w_in[:, :, o:o + GLA_KD + 2 * GLA_VD]; o += GLA_KD + 2 * GLA_VD
    glr = w_in[:, :, o:o + GLA_RANK]; o += GLA_RANK
    gates = w_in[:, :, o:o + 2 * D]
    main = jnp.concatenate([z, xbc, q, kvr, gates], axis=-1).astype(bf16)
    pad = jnp.zeros(w_in.shape[:2] + (LANES - SSD_NH - GLA_RANK,), w_in.dtype)
    tail = jnp.concatenate([dt, glr, pad], axis=-1).astype(bf16)
    return main, tail


def _pad_lanes(v, width=LANES):
    return jnp.concatenate([v, jnp.zeros(v.shape[:-1] + (width - v.shape[-1],), v.dtype)], axis=-1)


def kernel(x_prompt, x_sample, c_prompt, c_sample, state_conv, state_ssm, state_gla, w_ada, b_ada, norm1_w, w_in, conv_w, conv_b, dt_bias, a_log, d_skip, ssd_norm_w, w_ssd_br, gla_gate_up, gla_gate_b, gla_norm_w, w_gla_br, merge_b, w_out, norm2_w, w_router, router_bias, w_exp_gate, w_exp_up, w_exp_down, final_norm_w):
    bp, sp, _ = x_prompt.shape
    bs, ss, _ = x_sample.shape

    mod = _adaln(jnp.concatenate([c_prompt, c_sample], axis=0), w_ada, b_ada)
    mod_p = mod[:, :bp].reshape(DEPTH, bp, 6, D)
    mod_s = mod[:, bp:].reshape(DEPTH, bs, 6, D)

    w_main, w_tail = _prep_in_weights(w_in)
    gate_up = jnp.concatenate(
        [jnp.zeros((DEPTH, TAIL_GLR0, GLA_KD), f32), gla_gate_up,
         jnp.zeros((DEPTH, LANES - TAIL_GLR0 - GLA_RANK, GLA_KD), f32)], axis=1).astype(bf16)
    in_w = (norm1_w.reshape(DEPTH, 1, D), w_main, w_tail, gate_up, gla_gate_b.reshape(DEPTH, 1, GLA_KD),
            conv_w, conv_b.reshape(DEPTH, 1, CONV_DIM), merge_b.reshape(DEPTH, 1, 2 * D))
    bigsel, e8 = _ssd_consts()
    ssd_w = (_pad_lanes(dt_bias).reshape(DEPTH, 1, LANES), _pad_lanes(a_log).reshape(DEPTH, 1, LANES),
             jnp.repeat(d_skip, SSD_HD, axis=-1).reshape(DEPTH, SSD_NG, 1, SSD_GW),
             ssd_norm_w.reshape(DEPTH, SSD_NG, 1, SSD_GW),
             w_ssd_br.astype(bf16).reshape(DEPTH, SSD_NG, SSD_GW, D), bigsel, e8)
    gla_w = (gla_norm_w.reshape(DEPTH, 1, GLA_DV), w_gla_br.astype(bf16))
    w_out_b = w_out.astype(bf16)
    n2 = norm2_w.reshape(DEPTH, 1, D)
    wr = _pad_lanes(w_router)
    wr_hi = wr.astype(bf16)
    wr_lo = (wr - wr_hi.astype(f32)).astype(bf16)
    rb = router_bias.reshape(N_EXP, 1)

    xp = x_prompt.reshape(bp * sp, D)
    xs = x_sample
    ssm_s = state_ssm.reshape(DEPTH, bs, SSD_NG, SSD_GW, SSD_NS)
    ssd_lp = min(128, sp)
    gla_lp = min(64, sp)
    cp = sp_ = gp = cs_ = ss_ = gs = None
    moe_tm = min(MOE_TM, sp)
    moe_tiles = -(-(bp * sp) // MOE_R) + N_CLS
    xs_rows = jnp.zeros((moe_tiles * MOE_R, D + MOE_EXT), f32)
    fw = final_norm_w.reshape(1, D)
    for l in range(DEPTH):
        z4, xs4, bc, qkvr, gate, tail, lg, cs_ = _inproj(l, xs, mod_s, state_conv, cs_, in_w, seq3d=True)
        bra, ss_ = _ssd(l, xs4, bc, z4, tail, ssm_s, ss_, ssd_w, nseq=bs, L=ss, sb=8, nchunks=1)
        brb, gs = _gla(l, qkvr, lg, state_gla, gs, gla_w, nseq=bs, L=ss, C1=ss, sb=8, nchunks=1)
        x1, h2, comb = _merge(l, xs, bra, brb, gate, mod_s, w_out_b, n2, wr_hi, wr_lo, rb,
                              seq3d=True, sparse=False)
        xs, wg_b, wu_b, wd_b = _moe(l, h2, comb, w_exp_gate, w_exp_up, w_exp_down, x1, mod_s, seq3d=True)

        z4, xs4, bc, qkvr, gate, tail, lg, cp = _inproj(l, xp, mod_p, None, cp, in_w, seq3d=False)
        bra, sp_ = _ssd(l, xs4, bc, z4, tail, None, sp_, ssd_w, nseq=bp, L=ssd_lp, sb=1, nchunks=sp // ssd_lp,
                        cps=4 if (sp // ssd_lp) % 4 == 0 else 1)
        brb, gp = _gla(l, qkvr, lg, None, gp, gla_w, nseq=bp, L=gla_lp, C1=16,
                       sb=2 if bp % 2 == 0 else 1, nchunks=sp // gla_lp,
                       cps=2 if (sp // gla_lp) % 2 == 0 else 1)
        x1, h2x, plan, counts = _merge(l, xp, bra, brb, gate, mod_p, w_out_b, n2, wr_hi, wr_lo, rb,
                                       seq3d=False, sparse=True)
        pos, tiles = _moe_plan(plan, counts, moe_tiles)
        pos3 = pos.reshape(bp * sp // moe_tm, 1, moe_tm)
        xs_rows = _dispatch(pos3, h2x, xs_rows)
        ys_rows = _moe_sparse(tiles, xs_rows, wg_b, wu_b, wd_b)
        xp = _combine(l, pos3, ys_rows, x1, mod_p, fw, final=(l == DEPTH - 1))

    y_prompt = xp.reshape(bp, sp, D)
    y_sample = _final_norm(xs.reshape(bs * ss, D), fw).reshape(bs, ss, D)
    return (y_prompt, y_sample, cp, sp_.reshape(DEPTH, bp, SSD_NH, SSD_HD, SSD_NS), gp,
            cs_, ss_.reshape(DEPTH, bs, SSD_NH, SSD_HD, SSD_NS), gs)
```

```python
import functools

import numpy as np
import jax
import jax.numpy as jnp
from jax import lax
from jax.experimental import pallas as pl
from jax.experimental.pallas import tpu as pltpu

f32 = jnp.float32
bf16 = jnp.bfloat16

D = 1024
DEPTH = 4
SSD_INNER = 2048
SSD_HD = 64
SSD_NH = 32
SSD_NS = 128
SSD_NG = 4
SSD_HPG = 8
SSD_GW = SSD_INNER // SSD_NG
CONV_K = 4
CONV_DIM = SSD_INNER + 2 * SSD_NG * SSD_NS
GLA_NH = 4
GLA_DK = 128
GLA_DV = 256
GLA_KD = GLA_NH * GLA_DK
GLA_VD = GLA_NH * GLA_DV
GLA_RANK = 16
GLA_TAU = 16.0
N_EXP = 16
N_EGRP = 4
EXP_PER_GRP = 4
EXP_FF = 512
N_PAIR = 6
N_CLS = N_EGRP * N_PAIR
CLS_PAD = 32
MOE_EXT = 128
MOE_R = 256
MOE_TM = 512
MOE_DMA_UNROLL = 32
EPS = 1e-6
LANES = 128
TAIL_GLR0 = SSD_NH
MAIN_W = SSD_INNER + CONV_DIM + 2 * GLA_KD + 2 * GLA_VD + 2 * D
VMEM_LIMIT = 56 * 1024 * 1024


def _sigmoid(x):
    return 1.0 / (1.0 + jnp.exp(-x))


def _silu(x):
    return x * _sigmoid(x)


def _softplus(x):
    return jnp.maximum(x, 0.0) + jnp.log1p(jnp.exp(-jnp.abs(x)))


def _log_sigmoid(x):
    return jnp.minimum(x, 0.0) - jnp.log1p(jnp.exp(-jnp.abs(x)))


def _dot(a, b):
    return jnp.dot(a, b, preferred_element_type=f32)


def _dot_nt(a, b):
    return lax.dot_general(a, b, (((1,), (1,)), ((), ())), preferred_element_type=f32)


def _dot_tn(a, b):
    return lax.dot_general(a, b, (((0,), (0,)), ((), ())), preferred_element_type=f32)


def _split2(x):
    hi = x.astype(bf16)
    return hi, (x - hi.astype(f32)).astype(bf16)


def _split3(x):
    hi = x.astype(bf16)
    r1 = x - hi.astype(f32)
    mid = r1.astype(bf16)
    lo = (r1 - mid.astype(f32)).astype(bf16)
    return hi, mid, lo


def _cumsum_rows(x, tri):
    hi, mid, lo = _split3(x)
    if tri.shape[1] % LANES == 0:
        return _dot(jnp.concatenate([tri, tri, tri], axis=1), jnp.concatenate([hi, mid, lo], axis=0))
    return _dot(tri, hi) + _dot(tri, mid) + _dot(tri, lo)


def _tri(n):
    r = lax.broadcasted_iota(jnp.int32, (n, n), 0)
    c = lax.broadcasted_iota(jnp.int32, (n, n), 1)
    return r >= c


def _norm_mod(x, mod_ref, w_ref, i_shift, i_scale, seq3d):
    ms = jnp.mean(x * x, axis=-1, keepdims=True)
    y = x * lax.rsqrt(ms + EPS) * w_ref[...]
    if seq3d:
        sc = mod_ref[:, i_scale:i_scale + 1, :]
        sh = mod_ref[:, i_shift:i_shift + 1, :]
        h = y * (1.0 + sc) + sh
        return h.reshape(h.shape[0] * h.shape[1], h.shape[2])
    sc = mod_ref[i_scale:i_scale + 1, :]
    sh = mod_ref[i_shift:i_shift + 1, :]
    return y * (1.0 + sc) + sh


def _seq_tile(nseq, want):
    return want if nseq % want == 0 else nseq


def _stacked_out(prev, in_specs, args, aliases, out_index):
    if prev is not None:
        in_specs.append(pl.BlockSpec(memory_space=pl.ANY))
        args.append(prev)
        aliases[len(args) - 1] = out_index


def _adaln_kernel(c_ref, w_ref, b_ref, o_ref):
    s = _silu(c_ref[...]).astype(bf16)
    o_ref[...] = _dot(s, w_ref[...].astype(bf16)) + b_ref[...]


def _adaln(c_all, w_ada, b_ada):
    n = c_all.shape[0]
    tn = 1024
    return pl.pallas_call(
        _adaln_kernel,
        grid=(DEPTH, 6 * D // tn),
        in_specs=[
            pl.BlockSpec((n, D), lambda l, j: (0, 0)),
            pl.BlockSpec((None, D, tn), lambda l, j: (l, 0, j)),
            pl.BlockSpec((None, 1, tn), lambda l, j: (l, 0, j)),
        ],
        out_specs=pl.BlockSpec((None, n, tn), lambda l, j: (l, 0, j)),
        out_shape=jax.ShapeDtypeStruct((DEPTH, n, 6 * D), f32),
        compiler_params=pltpu.CompilerParams(
            dimension_semantics=("arbitrary", "arbitrary"), vmem_limit_bytes=VMEM_LIMIT),
        name="adaln",
    )(c_all, w_ada, b_ada.reshape(DEPTH, 1, 6 * D))


IN_TN = 1024
IN_NJ = MAIN_W // IN_TN
IN_SUB = 256
IN_NSUB = IN_TN // IN_SUB
CONV_PAD = 8
KPREV = CONV_K - 1


def _inproj_kernel(*refs, seq3d, has_state, per_seq, T):
    if has_state:
        (x_ref, mod_ref, n1_ref, wm_ref, wt_ref, wg_ref, gb_ref, cw_ref, cb_ref, mb_ref, conv0_ref) = refs[:11]
        rest = refs[11:]
    else:
        (x_ref, mod_ref, n1_ref, wm_ref, wt_ref, wg_ref, gb_ref, cw_ref, cb_ref, mb_ref) = refs[:10]
        conv0_ref = None
        rest = refs[10:]
    (z4_ref, xs4_ref, bc_ref, qkvr_ref, gate_ref, tail_ref, lg_ref, convn_ref,
     h_scr, cscr, carry) = rest[-11:]
    i = pl.program_id(0)
    j = pl.program_id(1)
    tm = h_scr.shape[0]
    ns = cscr.shape[1]

    @pl.when(j == 0)
    def _():
        h = _norm_mod(x_ref[...], mod_ref, n1_ref, 0, 1, seq3d).astype(bf16)
        h_scr[...] = h
        t = _dot(h, wt_ref[...])
        tail_ref[...] = t
        pre = _dot(t.astype(bf16), wg_ref[...]) + gb_ref[...]
        lg_ref[...] = _log_sigmoid(pre) * (1.0 / GLA_TAU)

    def sub_dot(c):
        return _dot(h_scr[...], wm_ref[:, IN_SUB * c:IN_SUB * (c + 1)])

    def conv_silu(a, cj, c):
        cols = slice(IN_SUB * c, IN_SUB * (c + 1))
        gcols = slice(IN_TN * cj + IN_SUB * c, IN_TN * cj + IN_SUB * (c + 1))
        if not has_state:
            prev = jnp.where(i % per_seq == 0, 0.0, carry[cj, c, CONV_PAD - KPREV:CONV_PAD, :])
            cscr[c, 0, CONV_PAD - KPREV:CONV_PAD, :] = prev
            cscr[c, 0, CONV_PAD:2 * CONV_PAD, :] = a[0:CONV_PAD]
            cscr[c, 0, 2 * CONV_PAD:3 * CONV_PAD, :] = a[tm - CONV_PAD:tm]
            w = [cw_ref[k:k + 1, cols] for k in range(CONV_K)]
            out = cb_ref[:, cols] + a * w[KPREV]
            head = cb_ref[:, cols] + a[0:CONV_PAD] * w[KPREV]
            for k in range(KPREV):
                out = out + pltpu.roll(a, KPREV - k, 0) * w[k]
                head = head + cscr[c, 0, CONV_PAD - KPREV + k:2 * CONV_PAD - KPREV + k, :] * w[k]
            out = jnp.concatenate([head, out[CONV_PAD:]], axis=0)
            new_tail = cscr[c, 0, 3 * CONV_PAD - KPREV:3 * CONV_PAD, :]
            convn_ref[0, :, gcols] = new_tail
            carry[cj, c, CONV_PAD - KPREV:CONV_PAD, :] = new_tail
            return _silu(out).astype(bf16)
        a3 = a.reshape(ns, T, IN_SUB)
        prev = conv0_ref[:, :, cols]
        cscr[c, :, CONV_PAD - KPREV:CONV_PAD, :] = prev
        cscr[c, :, CONV_PAD:CONV_PAD + T, :] = a3
        out = cb_ref[:, cols] + a3 * cw_ref[KPREV:KPREV + 1, cols]
        for k in range(KPREV):
            out = out + cscr[c, :, CONV_PAD - KPREV + k:CONV_PAD - KPREV + k + T, :] * cw_ref[k:k + 1, cols]
        new_tail = cscr[c, :, CONV_PAD + T - KPREV:CONV_PAD + T, :]
        convn_ref[:, :, gcols] = new_tail
        return _silu(out).reshape(tm, IN_SUB).astype(bf16)

    per_grp = SSD_GW // IN_SUB

    @pl.when(j < 2)
    def _():
        for c in range(IN_NSUB):
            lanes = slice(IN_SUB * (c % per_grp), IN_SUB * (c % per_grp + 1))
            z4_ref[c // per_grp, :, lanes] = _silu(sub_dot(c)).astype(bf16)

    for cj in range(2):
        @pl.when(j == 2 + cj)
        def _(cj=cj):
            for c in range(IN_NSUB):
                lanes = slice(IN_SUB * (c % per_grp), IN_SUB * (c % per_grp + 1))
                xs4_ref[c // per_grp, :, lanes] = conv_silu(sub_dot(c), cj, c)

    @pl.when(j == 4)
    def _():
        for c in range(IN_NSUB):
            xc = conv_silu(sub_dot(c), 2, c)
            for q in range(IN_SUB // SSD_NS):
                bc_ref[c * (IN_SUB // SSD_NS) + q] = xc[:, SSD_NS * q:SSD_NS * (q + 1)]

    @pl.when(jnp.logical_and(j >= 5, j < 7))
    def _():
        for c in range(IN_NSUB):
            qkvr_ref[:, IN_SUB * c:IN_SUB * (c + 1)] = sub_dot(c).astype(bf16)

    @pl.when(j == 7)
    def _():
        for c in range(IN_NSUB):
            qkvr_ref[:, IN_SUB * c:IN_SUB * (c + 1)] = _silu(sub_dot(c)).astype(bf16)

    @pl.when(j >= 8)
    def _():
        for c in range(IN_NSUB):
            cols = slice(IN_SUB * c, IN_SUB * (c + 1))
            gate_ref[:, cols] = _sigmoid(sub_dot(c) + mb_ref[:, cols]).astype(bf16)


def _inproj(l, x, mod, conv0, convn_prev, w, *, seq3d):
    (n1w, w_main, w_tail, gate_up, gate_b, conv_w, conv_b, merge_b) = w
    has_state = conv0 is not None
    if seq3d:
        nseq, T = x.shape[0], x.shape[1]
        ns = _seq_tile(nseq, 128)
        ntok = nseq * T
        tm = ns * T
        nt = nseq // ns
        per_seq = 1
        x_spec = pl.BlockSpec((ns, T, D), lambda i, j: (i, 0, 0))
        mod_spec = pl.BlockSpec((None, ns, 6, D), lambda i, j: (l, i, 0, 0))
        seq_blk = lambda i: i
    else:
        nseq = mod.shape[1]
        ntok = x.shape[0]
        tm = min(1024, ntok // nseq)
        T = tm
        ns = 1
        nt = ntok // tm
        per_seq = ntok // nseq // tm
        x_spec = pl.BlockSpec((tm, D), lambda i, j: (i, 0))
        mod_spec = pl.BlockSpec((None, None, 6, D), lambda i, j: (l, i // per_seq, 0, 0))
        seq_blk = lambda i: i // per_seq

    def cj(j):
        return jnp.clip(j - 2, 0, 2)

    in_specs = [
        x_spec,
        mod_spec,
        pl.BlockSpec((None, 1, D), lambda i, j: (l, 0, 0)),
        pl.BlockSpec((None, D, IN_TN), lambda i, j: (l, 0, j)),
        pl.BlockSpec((None, D, LANES), lambda i, j: (l, 0, 0)),
        pl.BlockSpec((None, LANES, GLA_KD), lambda i, j: (l, 0, 0)),
        pl.BlockSpec((None, 1, GLA_KD), lambda i, j: (l, 0, 0)),
        pl.BlockSpec((None, CONV_K, IN_TN), lambda i, j: (l, 0, cj(j))),
        pl.BlockSpec((None, 1, IN_TN), lambda i, j: (l, 0, cj(j))),
        pl.BlockSpec((None, 1, IN_TN), lambda i, j: (l, 0, jnp.clip(j - 8, 0, 1))),
    ]
    args = [x, mod, n1w, w_main, w_tail, gate_up, gate_b, conv_w, conv_b, merge_b]
    if has_state:
        in_specs.append(pl.BlockSpec((None, ns, KPREV, IN_TN), lambda i, j: (l, i, 0, cj(j))))
        args.append(conv0)
    aliases = {}
    _stacked_out(convn_prev, in_specs, args, aliases, 7)
    out_shapes = (
        jax.ShapeDtypeStruct((SSD_NG, ntok, SSD_GW), bf16),
        jax.ShapeDtypeStruct((SSD_NG, ntok, SSD_GW), bf16),
        jax.ShapeDtypeStruct((2 * SSD_NG, ntok, SSD_NS), bf16),
        jax.ShapeDtypeStruct((ntok, 2 * GLA_KD + 2 * GLA_VD), bf16),
        jax.ShapeDtypeStruct((ntok, 2 * D), bf16),
        jax.ShapeDtypeStruct((ntok, LANES), f32),
        jax.ShapeDtypeStruct((ntok, GLA_KD), f32),
        jax.ShapeDtypeStruct((DEPTH, nseq, KPREV, CONV_DIM), f32),
    )
    out_specs = (
        pl.BlockSpec((2, tm, SSD_GW), lambda i, j: (jnp.clip(j, 0, 1), i, 0)),
        pl.BlockSpec((2, tm, SSD_GW), lambda i, j: (jnp.clip(j - 2, 0, 1), i, 0)),
        pl.BlockSpec((2 * SSD_NG, tm, SSD_NS), lambda i, j: (0, i, 0)),
        pl.BlockSpec((tm, IN_TN), lambda i, j: (i, jnp.clip(j - 5, 0, 2))),
        pl.BlockSpec((tm, IN_TN), lambda i, j: (i, jnp.clip(j - 8, 0, 1))),
        pl.BlockSpec((tm, LANES), lambda i, j: (i, 0)),
        pl.BlockSpec((tm, GLA_KD), lambda i, j: (i, 0)),
        pl.BlockSpec((None, ns, KPREV, CONV_DIM), lambda i, j: (l, seq_blk(i), 0, 0)),
    )
    return pl.pallas_call(
        functools.partial(_inproj_kernel, seq3d=seq3d, has_state=has_state, per_seq=per_seq, T=T),
        grid=(nt, IN_NJ),
        in_specs=in_specs,
        out_specs=out_specs,
        out_shape=out_shapes,
        input_output_aliases=aliases,
        scratch_shapes=[
            pltpu.VMEM((tm, D), bf16),
            pltpu.VMEM((IN_NSUB, ns, CONV_PAD + T if has_state else 3 * CONV_PAD, IN_SUB), f32),
            pltpu.VMEM((3, IN_NSUB, CONV_PAD, IN_SUB), f32),
        ],
        compiler_params=pltpu.CompilerParams(
            dimension_semantics=("arbitrary", "arbitrary"), vmem_limit_bytes=VMEM_LIMIT),
        name="inproj",
    )(*args)


SSD_CS_PIECES = 2


def _ssd_consts():
    bigsel = np.zeros((SSD_NG, SSD_CS_PIECES * LANES, SSD_HPG * LANES), np.float32)
    e8 = np.zeros((SSD_NG, 2 * LANES, SSD_GW), np.float32)
    for g in range(SSD_NG):
        for j in range(SSD_HPG):
            for k in range(SSD_CS_PIECES):
                bigsel[g, k * LANES + SSD_HPG * g + j, LANES * j:LANES * (j + 1)] = 1.0
            for k in range(2):
                e8[g, k * LANES + SSD_HPG * g + j, SSD_HD * j:SSD_HD * (j + 1)] = 1.0
    return jnp.asarray(bigsel, bf16), jnp.asarray(e8, bf16)


def _ssd_kernel(*refs, L, sb, cps, nchunks, has_state):
    R = sb * L
    xs4_ref, bc_ref, z4_ref, tail_ref = refs[:4]
    k = 4
    ssm0_ref = None
    if has_state:
        ssm0_ref = refs[k]
        k += 1
    dtb_ref, alog_ref, dsk_ref, nw_ref, wbr_ref, bigsel_ref, e8_ref = refs[k:k + 7]
    br_ref, ssmn_ref, cst_scr = refs[-3:]
    c = pl.program_id(1)

    rowi = lax.broadcasted_iota(jnp.int32, (R, R), 0)
    coli = lax.broadcasted_iota(jnp.int32, (R, R), 1)
    mask = rowi >= coli
    if sb > 1:
        sh = L.bit_length() - 1
        mask = jnp.logical_and(mask, (rowi >> sh) == (coli >> sh))
    tri = mask.astype(bf16)
    lane = lax.broadcasted_iota(jnp.int32, (R, LANES), 1)
    lo_half = lane < SSD_HD

    def init():
        if has_state:
            ssmn_ref[...] = ssm0_ref[...]
        else:
            ssmn_ref[...] = jnp.zeros(ssmn_ref.shape, f32)

    if nchunks == 1:
        init()
    else:
        pl.when(c == 0)(init)

    def seq_last(x):
        if sb == 1:
            return x[R - 1:R, :]
        x3 = x.reshape(sb, L, x.shape[-1])
        return jnp.broadcast_to(x3[:, L - 1:L, :], x3.shape).reshape(x.shape)

    def chunk(ci):
        rows = slice(ci * R, (ci + 1) * R)
        dtp = _softplus(tail_ref[rows, :] + dtb_ref[...])
        a = -jnp.exp(alog_ref[...])
        cs = _cumsum_rows(dtp * a, tri)
        cst_scr[ci] = cs.T
        h3 = jnp.concatenate(_split3(cs)[:SSD_CS_PIECES], axis=1)
        d2 = jnp.concatenate(_split2(dtp), axis=1)
        br_ref[rows, :] = jnp.zeros((R, D), f32)

        for g in range(SSD_NG):
            cm = _dot(h3, bigsel_ref[g])
            dt_exp = _dot(d2, e8_ref[g])
            cs_exp = jnp.concatenate(
                [jnp.where(lo_half, cm[:, 2 * LANES * i:2 * LANES * i + LANES],
                           cm[:, 2 * LANES * i + LANES:2 * LANES * (i + 1)]) for i in range(SSD_HPG // 2)],
                axis=1)
            ecs = jnp.exp(cs_exp)
            ce = seq_last(cs_exp)
            xs = xs4_ref[g, rows, :].astype(f32)
            xdt = xs * dt_exp
            xdt_b = xdt.astype(bf16)
            xse = xdt * jnp.exp(ce - cs_exp)
            bg = bc_ref[g, rows, :]
            cg = bc_ref[SSD_NG + g, rows, :]
            cb = _dot_nt(cg, bg)

            pairs = []
            for i in range(SSD_HPG // 2):
                ws = []
                for j in (2 * i, 2 * i + 1):
                    row = cst_scr[ci, pl.ds(SSD_HPG * g + j, 1), :]
                    seg = cm[:, LANES * j:LANES * j + R] - row
                    dec = jnp.exp(jnp.where(mask, seg, -jnp.inf))
                    ws.append((cb * dec).astype(bf16))
                yy = _dot(jnp.concatenate(ws, axis=0), xdt_b[:, LANES * i:LANES * (i + 1)])
                pairs.append(jnp.where(lo_half, yy[:R], yy[R:]))
            y = jnp.concatenate(pairs, axis=1)

            if sb > 1:
                bgf = bg.astype(f32)
                cgf = cg.astype(f32)
            ys_parts = []
            for s in range(sb):
                rs = slice(s * L, (s + 1) * L)
                st = ssmn_ref[s, g]
                b_s = bg if sb == 1 else bgf[rs].astype(bf16)
                c_s = cg if sb == 1 else cgf[rs].astype(bf16)
                ys_parts.append(_dot_nt(c_s, st.astype(bf16)))
                upd = _dot_tn(xse[rs].astype(bf16), b_s)
                e_end = jnp.exp(ce[s * L:s * L + 1, :])
                ssmn_ref[s, g] = jnp.concatenate(
                    [st[SSD_HD * j:SSD_HD * (j + 1)] * e_end[:, SSD_HD * j:SSD_HD * j + 1]
                     + upd[SSD_HD * j:SSD_HD * (j + 1)] for j in range(SSD_HPG)], axis=0)
            ys = ys_parts[0] if sb == 1 else jnp.concatenate(ys_parts, axis=0)
            y = y + ys * ecs + dsk_ref[g] * xs

            yg = y * z4_ref[g, rows, :].astype(f32)
            yn = yg * lax.rsqrt(jnp.mean(yg * yg, axis=-1, keepdims=True) + EPS) * nw_ref[g]
            br_ref[rows, :] += _dot(yn.astype(bf16), wbr_ref[g])

    for ci in range(cps):
        chunk(ci)


def _ssd(l, xs4, bc, z4, tail, ssm0, ssmn_prev, w, *, nseq, L, sb, nchunks, cps=1):
    has_state = ssm0 is not None
    dtb, alog, dsk, nw, wbr, bigsel, e8 = w
    ntok = tail.shape[0]
    chunk_rows = sb * L
    R = cps * chunk_rows
    assert nchunks % cps == 0
    nchunks = nchunks // cps
    if nchunks == 1:
        rblk = lambda b, c: b
    else:
        rblk = lambda b, c: b * nchunks + c
    in_specs = [
        pl.BlockSpec((SSD_NG, R, SSD_GW), lambda b, c: (0, rblk(b, c), 0)),
        pl.BlockSpec((2 * SSD_NG, R, SSD_NS), lambda b, c: (0, rblk(b, c), 0)),
        pl.BlockSpec((SSD_NG, R, SSD_GW), lambda b, c: (0, rblk(b, c), 0)),
        pl.BlockSpec((R, LANES), lambda b, c: (rblk(b, c), 0)),
    ]
    args = [xs4, bc, z4, tail]
    if has_state:
        in_specs.append(pl.BlockSpec((None, sb, SSD_NG, SSD_GW, SSD_NS), lambda b, c: (l, b, 0, 0, 0)))
        args.append(ssm0)
    in_specs += [
        pl.BlockSpec((None, 1, LANES), lambda b, c: (l, 0, 0)),
        pl.BlockSpec((None, 1, LANES), lambda b, c: (l, 0, 0)),
        pl.BlockSpec((None, SSD_NG, 1, SSD_GW), lambda b, c: (l, 0, 0, 0)),
        pl.BlockSpec((None, SSD_NG, 1, SSD_GW), lambda b, c: (l, 0, 0, 0)),
        pl.BlockSpec((None, SSD_NG, SSD_GW, D), lambda b, c: (l, 0, 0, 0)),
        pl.BlockSpec((SSD_NG, SSD_CS_PIECES * LANES, SSD_HPG * LANES), lambda b, c: (0, 0, 0)),
        pl.BlockSpec((SSD_NG, 2 * LANES, SSD_GW), lambda b, c: (0, 0, 0)),
    ]
    args += [dtb, alog, dsk, nw, wbr, bigsel, e8]
    aliases = {}
    _stacked_out(ssmn_prev, in_specs, args, aliases, 1)
    return pl.pallas_call(
        functools.partial(_ssd_kernel, L=L, sb=sb, cps=cps, nchunks=nchunks, has_state=has_state),
        grid=(nseq // sb, nchunks),
        in_specs=in_specs,
        out_specs=(
            pl.BlockSpec((R, D), lambda b, c: (rblk(b, c), 0)),
            pl.BlockSpec((None, sb, SSD_NG, SSD_GW, SSD_NS), lambda b, c: (l, b, 0, 0, 0)),
        ),
        out_shape=(
            jax.ShapeDtypeStruct((ntok, D), f32),
            jax.ShapeDtypeStruct((DEPTH, nseq, SSD_NG, SSD_GW, SSD_NS), f32),
        ),
        input_output_aliases=aliases,
        scratch_shapes=[pltpu.VMEM((cps, LANES, chunk_rows), f32)],
        compiler_params=pltpu.CompilerParams(
            dimension_semantics=("arbitrary", "arbitrary"), vmem_limit_bytes=VMEM_LIMIT),
        name="ssd",
    )(*args)


GLA_C2 = 4


def _gla_consts(L, C1, sb):
    R = sb * L
    nb2 = C1 // GLA_C2
    i = np.arange(R)[:, None]
    s = np.arange(R)[None, :]
    same_seq = (i // L) == (s // L)
    mats = [(s <= i) & (s > i - d) & same_seq for d in range(1, GLA_C2)]
    mats.append((s <= i) & (s >= (i // GLA_C2) * GLA_C2))
    for r in range(1, nb2):
        mats.append((s > i) & (s <= (i // C1) * C1 + GLA_C2 * r - 1))
    shifts = [(s == i - d) & same_seq for d in range(1, GLA_C2)]
    return (jnp.asarray(np.concatenate(mats, 0), bf16), jnp.asarray(np.concatenate(shifts, 0), bf16))


def _gla_kernel(*refs, L, C1, sb, cps, nchunks, has_state):
    R = sb * L
    nb1 = L // C1
    nb2 = C1 // GLA_C2
    qkvr_ref, lg_ref = refs[:2]
    k = 2
    gla0_ref = None
    if has_state:
        gla0_ref = refs[k]
        k += 1
    gnw_ref, wbr_ref, sums_ref, shift_ref = refs[k:k + 4]
    br_ref, glan_ref = refs[-2:]
    c = pl.program_id(1)

    rowi = lax.broadcasted_iota(jnp.int32, (R, R), 0)
    coli = lax.broadcasted_iota(jnp.int32, (R, R), 1)
    shl = L.bit_length() - 1
    same_seq = (rowi >> shl) == (coli >> shl)
    tri = jnp.logical_and(rowi >= coli, same_seq).astype(bf16)
    sh1 = C1.bit_length() - 1
    same_blk1 = (rowi >> sh1) == (coli >> sh1)
    sub_i = (rowi & (C1 - 1)) >> (GLA_C2.bit_length() - 1)
    off_i = rowi & (GLA_C2 - 1)
    rowl = lax.broadcasted_iota(jnp.int32, (R, 1), 0)
    sub_l = (rowl & (C1 - 1)) >> (GLA_C2.bit_length() - 1)
    blk_i = (rowi & (L - 1)) >> sh1
    rel_l = rowl & (L - 1)
    pad_rows = [jnp.zeros((LANES - R, GLA_DK), bf16)] if R < LANES else []

    def init():
        if has_state:
            glan_ref[...] = gla0_ref[...]
        else:
            glan_ref[...] = jnp.zeros(glan_ref.shape, f32)

    if nchunks == 1:
        init()
    else:
        pl.when(c == 0)(init)

    def seq_last(x):
        if sb == 1:
            return x[R - 1:R, :]
        x3 = x.reshape(sb, L, x.shape[-1])
        return jnp.broadcast_to(x3[:, L - 1:L, :], x3.shape).reshape(x.shape)

    def chunk(ci):
        def ld(ref, cols):
            if len(ref.shape) == 3:
                return ref[:, ci * L:(ci + 1) * L, cols].reshape(R, -1)
            return ref[:, cols]

        lg = ld(lg_ref, slice(None))
        g = _cumsum_rows(lg, tri)
        g_end = seq_last(g)
        lg_hi, lg_lo = _split2(lg)
        if R % LANES == 0:
            sums = _dot(jnp.concatenate([sums_ref[...], sums_ref[...]], axis=1),
                        jnp.concatenate([lg_hi, lg_lo], axis=0))
        else:
            sums = _dot(sums_ref[...], lg_hi) + _dot(sums_ref[...], lg_lo)
        a_d = [sums[R * (d - 1):R * d] for d in range(1, GLA_C2)]
        a_sub = sums[R * (GLA_C2 - 1):R * GLA_C2]
        b_sub = [sums[R * (GLA_C2 - 1 + r):R * (GLA_C2 + r)] for r in range(1, nb2)]
        q = ld(qkvr_ref, slice(0, GLA_KD)).astype(f32)
        kb = ld(qkvr_ref, slice(GLA_KD, 2 * GLA_KD))
        k_ = kb.astype(f32)
        k_sh = [_dot(shift_ref[R * (d - 1):R * d, :], kb) for d in range(1, GLA_C2)]

        acc = jnp.zeros((R, D), f32)
        for h in range(GLA_NH):
            kl = slice(GLA_DK * h, GLA_DK * (h + 1))
            vl = slice(2 * GLA_KD + GLA_DV * h, 2 * GLA_KD + GLA_DV * (h + 1))
            rl = slice(2 * GLA_KD + GLA_VD + GLA_DV * h, 2 * GLA_KD + GLA_VD + GLA_DV * (h + 1))
            gh = g[:, kl]
            qh = q[:, kl]
            kh = k_[:, kl]
            vb = ld(qkvr_ref, vl)
            qg = qh * jnp.exp(gh)
            kd = kh * jnp.exp(g_end[:, kl] - gh)

            if sb == 1:
                st = glan_ref[0, h]
                o = _dot(qg.astype(bf16), st.astype(bf16))
                e_col = jnp.exp(jnp.broadcast_to(g_end[0:1, kl], (8, GLA_DK))).T[:, 0:1]
                glan_ref[0, h] = st * e_col + _dot_tn(kd.astype(bf16), vb)
            else:
                vf = vb.astype(f32)
                parts = []
                for s in range(sb):
                    rs = slice(s * L, (s + 1) * L)
                    st = glan_ref[s, h]
                    parts.append(_dot(qg[rs].astype(bf16), st.astype(bf16)))
                    e_col = jnp.exp(jnp.broadcast_to(g_end[s * L:s * L + 1, kl], (8, GLA_DK))).T[:, 0:1]
                    glan_ref[s, h] = st * e_col + _dot_tn(kd[rs].astype(bf16), vf[rs].astype(bf16))
                o = jnp.concatenate(parts, axis=0)

            att = jnp.zeros((R, R), f32)
            if nb1 > 1:
                def bnd(s, i):
                    if i == 0:
                        return jnp.zeros((1, GLA_DK), f32)
                    return gh[s * L + C1 * i - 1:s * L + C1 * i, :]

                gblk = jnp.concatenate([jnp.broadcast_to(bnd(s, i), (C1, GLA_DK))
                                        for s in range(sb) for i in range(nb1)], axis=0)
                qt = (qh * jnp.exp(gh - gblk)).astype(bf16)
                kts = []
                for i in range(1, nb1):
                    gb = jnp.concatenate([jnp.broadcast_to(bnd(s, i), (L, GLA_DK)) for s in range(sb)], axis=0)
                    kt = kh * jnp.exp(jnp.where(rel_l < C1 * i, gb - gh, -jnp.inf))
                    kts += [kt.astype(bf16)] + pad_rows
                out1 = _dot_nt(qt, jnp.concatenate(kts, axis=0))
                for i in range(1, nb1):
                    att = att + jnp.where(jnp.logical_and(blk_i == i, same_seq),
                                          out1[:, LANES * (i - 1):LANES * (i - 1) + R], 0.0)

            qt2 = (qh * jnp.exp(a_sub[:, kl])).astype(bf16)
            kts = []
            for r in range(1, nb2):
                kt = kh * jnp.exp(jnp.where(sub_l < r, b_sub[r - 1][:, kl], -jnp.inf))
                kts += [kt.astype(bf16)] + pad_rows
            out2 = _dot_nt(qt2, jnp.concatenate(kts, axis=0))
            for r in range(1, nb2):
                att = att + jnp.where(jnp.logical_and(sub_i == r, same_blk1),
                                      out2[:, LANES * (r - 1):LANES * (r - 1) + R], 0.0)

            for d in range(GLA_C2):
                t = qh * kh if d == 0 else qh * jnp.exp(a_d[d - 1][:, kl]) * k_sh[d - 1][:, kl]
                band = jnp.sum(t, axis=-1, keepdims=True)
                hit = jnp.logical_and(off_i >= d, coli == rowi - d)
                att = att + jnp.where(hit, band, 0.0)
            o = o + _dot(att.astype(bf16), vb)

            on = o * lax.rsqrt(jnp.mean(o * o, axis=-1, keepdims=True) + EPS) * gnw_ref[...]
            og = on * ld(qkvr_ref, rl).astype(f32)
            acc = acc + _dot(og.astype(bf16), wbr_ref[GLA_DV * h:GLA_DV * (h + 1), :])
        if len(br_ref.shape) == 3:
            br_ref[:, ci * L:(ci + 1) * L, :] = acc.reshape(sb, L, D)
        else:
            br_ref[...] = acc

    for ci in range(cps):
        chunk(ci)


def _gla(l, qkvr, lg, gla0, glan_prev, w, *, nseq, L, C1, sb, nchunks, cps=1):
    has_state = gla0 is not None
    gnw, wbr = w
    sums, shifts = _gla_consts(L, C1, sb)
    ntok = qkvr.shape[0]
    rb = sb * L
    wq = 2 * GLA_KD + 2 * GLA_VD
    flat = nchunks == 1
    if flat:
        row = lambda b, c: (b, 0)
        blk = lambda w_: (rb, w_)
        view = lambda a: a
    else:
        assert nchunks % cps == 0
        nchunks = nchunks // cps
        row = lambda b, c: (b, c, 0)
        blk = lambda w_: (sb, cps * L, w_)
        view = lambda a: a.reshape(nseq, ntok // nseq, a.shape[-1])
    in_specs = [
        pl.BlockSpec(blk(wq), row),
        pl.BlockSpec(blk(GLA_KD), row),
    ]
    args = [view(qkvr), view(lg)]
    if has_state:
        in_specs.append(pl.BlockSpec((None, sb, GLA_NH, GLA_DK, GLA_DV), lambda b, c: (l, b, 0, 0, 0)))
        args.append(gla0)
    in_specs += [
        pl.BlockSpec((None, 1, GLA_DV), lambda b, c: (l, 0, 0)),
        pl.BlockSpec((None, GLA_VD, D), lambda b, c: (l, 0, 0)),
        pl.BlockSpec(sums.shape, lambda b, c: (0, 0)),
        pl.BlockSpec(shifts.shape, lambda b, c: (0, 0)),
    ]
    args += [gnw, wbr, sums, shifts]
    aliases = {}
    _stacked_out(glan_prev, in_specs, args, aliases, 1)
    br_shape = (ntok, D) if flat else (nseq, ntok // nseq, D)
    br, glan = pl.pallas_call(
        functools.partial(_gla_kernel, L=L, C1=C1, sb=sb, cps=cps, nchunks=nchunks, has_state=has_state),
        grid=(nseq // sb, nchunks),
        in_specs=in_specs,
        out_specs=(
            pl.BlockSpec(blk(D), row),
            pl.BlockSpec((None, sb, GLA_NH, GLA_DK, GLA_DV), lambda b, c: (l, b, 0, 0, 0)),
        ),
        out_shape=(
            jax.ShapeDtypeStruct(br_shape, f32),
            jax.ShapeDtypeStruct((DEPTH, nseq, GLA_NH, GLA_DK, GLA_DV), f32),
        ),
        input_output_aliases=aliases,
        compiler_params=pltpu.CompilerParams(
            dimension_semantics=("arbitrary", "arbitrary"), vmem_limit_bytes=VMEM_LIMIT),
        name="gla",
    )(*args)
    return br.reshape(ntok, D), glan


def _top2_sum(a, b, c, d):
    hi1, lo1 = jnp.maximum(a, b), jnp.minimum(a, b)
    hi2, lo2 = jnp.maximum(c, d), jnp.minimum(c, d)
    return jnp.maximum(hi1, hi2) + jnp.maximum(jnp.minimum(hi1, hi2), jnp.maximum(lo1, lo2))


def _route_rows(sig, biased):
    gsc = [_top2_sum(*biased[EXP_PER_GRP * g:EXP_PER_GRP * (g + 1)]) for g in range(N_EGRP)]
    best = jnp.zeros_like(gsc[0], dtype=jnp.int32)
    m = gsc[0]
    for g in range(1, N_EGRP):
        better = gsc[g] > m
        best = jnp.where(better, g, best)
        m = jnp.where(better, gsc[g], m)
    masked = [jnp.where(best == (e // EXP_PER_GRP), biased[e], -jnp.inf) for e in range(N_EXP)]

    def first_argmax(vals):
        idx = jnp.zeros_like(best)
        mx = vals[0]
        for e in range(1, N_EXP):
            better = vals[e] > mx
            idx = jnp.where(better, e, idx)
            mx = jnp.where(better, vals[e], mx)
        return idx

    i1 = first_argmax(masked)
    i2 = first_argmax([jnp.where(i1 == e, -jnp.inf, masked[e]) for e in range(N_EXP)])
    w1 = sum(jnp.where(i1 == e, sig[e], 0.0) for e in range(N_EXP))
    w2 = sum(jnp.where(i2 == e, sig[e], 0.0) for e in range(N_EXP))
    den = w1 + w2
    return best, i1, i2, w1 / den, w2 / den


def _merge_kernel(*refs, seq3d, sparse):
    (x_ref, bra_ref, brb_ref, gate_ref, mod_ref, wo_ref, n2_ref, wr_hi_ref, wr_lo_ref, rb_ref) = refs[:10]
    mixed_in = (gate_ref[:, 0:D].astype(f32) * bra_ref[...]
                + gate_ref[:, D:2 * D].astype(f32) * brb_ref[...])
    mixed = _dot(mixed_in.astype(bf16), wo_ref[...])
    x = x_ref[...]
    if seq3d:
        x1 = x + mod_ref[:, 2:3, :] * mixed.reshape(x.shape)
    else:
        x1 = x + mod_ref[2:3, :] * mixed
    refs[10][...] = x1
    h2 = _norm_mod(x1, mod_ref, n2_ref, 3, 4, seq3d)
    hi, lo = _split2(h2)
    logits = _dot(hi, wr_hi_ref[...]) + (_dot(hi, wr_lo_ref[...]) + _dot(lo, wr_hi_ref[...]))
    tm = logits.shape[0]
    lt = logits.T
    sig_all = _sigmoid(lt[0:N_EXP, :])
    bias_all = sig_all + rb_ref[...]
    sig = [sig_all[e:e + 1, :] for e in range(N_EXP)]
    biased = [bias_all[e:e + 1, :] for e in range(N_EXP)]
    best, i1, i2, w1, w2 = _route_rows(sig, biased)
    if not sparse:
        h2_ref, comb_ref = refs[11:13]
        h2_ref[...] = h2.astype(bf16)
        comb = [jnp.where(i1 == e, w1, 0.0) + jnp.where(i2 == e, w2, 0.0) for e in range(N_EXP)]
        comb_t = jnp.concatenate(comb + [jnp.zeros((LANES - N_EXP, tm), f32)], axis=0)
        comb_ref[...] = comb_t.T
        return

    h2x_ref, plan_ref, counts_ref, base_scr = refs[11:15]
    swap = i1 > i2
    wa = jnp.where(swap, w2, w1)
    wb = jnp.where(swap, w1, w2)
    a = jnp.minimum(i1, i2) & (EXP_PER_GRP - 1)
    b = jnp.maximum(i1, i2) & (EXP_PER_GRP - 1)
    pair = jnp.where(a == 0, b - 1, jnp.where(a == 1, b + 1, 5))
    cls = best * N_PAIR + pair
    sub = lax.broadcasted_iota(jnp.int32, (CLS_PAD, tm), 0)
    onehot = (sub == cls).astype(f32)
    before = (lax.broadcasted_iota(jnp.int32, (tm, tm), 0)
              < lax.broadcasted_iota(jnp.int32, (tm, tm), 1)).astype(bf16)
    prefix = _dot(onehot.astype(bf16), before)

    @pl.when(pl.program_id(0) == 0)
    def _():
        base_scr[...] = jnp.zeros_like(base_scr)

    base = base_scr[...]
    rank = jnp.sum(onehot * (prefix + base[:, 0:1]), axis=0, keepdims=True)
    base = base + jnp.sum(onehot, axis=1, keepdims=True)
    base_scr[...] = base
    counts_ref[...] = base
    plan_ref[...] = jnp.concatenate(
        [cls, rank.astype(jnp.int32), jnp.zeros((6, tm), jnp.int32)], axis=0)
    ext_t = jnp.concatenate([wa, wb, jnp.zeros((MOE_EXT - 2, tm), f32)], axis=0)
    h2x_ref[:, 0:D] = h2
    h2x_ref[:, D:D + MOE_EXT] = ext_t.T


def _merge(l, x, bra, brb, gate, mod, wo, n2w, wr_hi, wr_lo, rb, *, seq3d, sparse):
    if seq3d:
        ns = _seq_tile(x.shape[0], 32)
        ntok = x.shape[0] * x.shape[1]
        tm = ns * x.shape[1]
        nt = x.shape[0] // ns
        x_spec = pl.BlockSpec((ns, x.shape[1], D), lambda i: (i, 0, 0))
        mod_spec = pl.BlockSpec((None, ns, 6, D), lambda i: (l, i, 0, 0))
    else:
        ntok = x.shape[0]
        nseq = mod.shape[1]
        tm = min(512, ntok // nseq)
        nt = ntok // tm
        per_seq = ntok // nseq // tm
        x_spec = pl.BlockSpec((tm, D), lambda i: (i, 0))
        mod_spec = pl.BlockSpec((None, None, 6, D), lambda i: (l, i // per_seq, 0, 0))
    row = lambda i: (i, 0)
    const = lambda i: (0, 0)
    if sparse:
        out_specs = (x_spec, pl.BlockSpec((tm, D + MOE_EXT), row), pl.BlockSpec((8, tm), lambda i: (0, i)),
                     pl.BlockSpec((CLS_PAD, LANES), const))
        out_shape = (jax.ShapeDtypeStruct(x.shape, f32), jax.ShapeDtypeStruct((ntok, D + MOE_EXT), f32),
                     jax.ShapeDtypeStruct((8, ntok), jnp.int32), jax.ShapeDtypeStruct((CLS_PAD, LANES), f32))
        scratch = [pltpu.VMEM((CLS_PAD, LANES), f32)]
    else:
        out_specs = (x_spec, pl.BlockSpec((tm, D), row), pl.BlockSpec((tm, LANES), row))
        out_shape = (jax.ShapeDtypeStruct(x.shape, f32), jax.ShapeDtypeStruct((ntok, D), bf16),
                     jax.ShapeDtypeStruct((ntok, LANES), f32))
        scratch = []
    return pl.pallas_call(
        functools.partial(_merge_kernel, seq3d=seq3d, sparse=sparse),
        grid=(nt,),
        in_specs=[
            x_spec,
            pl.BlockSpec((tm, D), row),
            pl.BlockSpec((tm, D), row),
            pl.BlockSpec((tm, 2 * D), row),
            mod_spec,
            pl.BlockSpec((None, D, D), lambda i: (l, 0, 0)),
            pl.BlockSpec((None, 1, D), lambda i: (l, 0, 0)),
            pl.BlockSpec((D, LANES), const),
            pl.BlockSpec((D, LANES), const),
            pl.BlockSpec((N_EXP, 1), const),
        ],
        out_specs=out_specs,
        out_shape=out_shape,
        scratch_shapes=scratch,
        compiler_params=pltpu.CompilerParams(
            dimension_semantics=("arbitrary",), vmem_limit_bytes=VMEM_LIMIT),
        name="merge",
    )(x, bra, brb, gate, mod, wo, n2w, wr_hi, wr_lo, rb)


def _dispatch_kernel(pos_ref, src_ref, buf_in, xs_hbm, sem):
    del buf_in
    tm = src_ref.shape[0]

    def issue(tb, c):
        for u in range(MOE_DMA_UNROLL):
            t = tb * MOE_DMA_UNROLL + u
            p = pos_ref[0, t]
            pltpu.make_async_copy(src_ref.at[pl.ds(t, 1), :], xs_hbm.at[pl.ds(p, 1), :], sem).start(priority=u % 2)
        return c

    lax.fori_loop(0, tm // MOE_DMA_UNROLL, issue, 0)

    def drain(t, c):
        pltpu.make_async_copy(src_ref.at[pl.ds(0, 1), :], xs_hbm.at[pl.ds(0, 1), :], sem).wait()
        return c

    lax.fori_loop(0, tm, drain, 0, unroll=MOE_DMA_UNROLL)


def _dispatch(pos3, h2x, xs_buf):
    nt, _, tm = pos3.shape
    return pl.pallas_call(
        _dispatch_kernel,
        grid=(nt,),
        in_specs=[
            pl.BlockSpec((None, 1, tm), lambda i: (i, 0, 0), memory_space=pltpu.SMEM),
            pl.BlockSpec((tm, D + MOE_EXT), lambda i: (i, 0)),
            pl.BlockSpec(memory_space=pl.ANY),
        ],
        out_specs=pl.BlockSpec(memory_space=pl.ANY),
        out_shape=jax.ShapeDtypeStruct(xs_buf.shape, f32),
        input_output_aliases={2: 0},
        scratch_shapes=[pltpu.SemaphoreType.DMA(())],
        compiler_params=pltpu.CompilerParams(dimension_semantics=("arbitrary",)),
        name="dispatch",
    )(pos3, h2x, xs_buf)


def _moe_sparse_kernel(src_ref, valid_ref, ea_ref, eb_ref, xs_ref,
                       wga_ref, wua_ref, wda_ref, wgb_ref, wub_ref, wdb_ref, ys_ref):
    del src_ref, ea_ref, eb_ref
    t = pl.program_id(0)

    @pl.when(valid_ref[t] == 1)
    def _():
        x = xs_ref[...]
        h = x[:, 0:D].astype(bf16)

        def ffn(wg, wu, wd):
            u = (_silu(_dot(h, wg[...])) * _dot(h, wu[...])).astype(bf16)
            return _dot(u, wd[...])

        ys_ref[...] = (x[:, D:D + 1] * ffn(wga_ref, wua_ref, wda_ref)
                       + x[:, D + 1:D + 2] * ffn(wgb_ref, wub_ref, wdb_ref))

    @pl.when(valid_ref[pl.program_id(0)] == 0)
    def _():
        ys_ref[...] = jnp.zeros(ys_ref.shape, f32)


def _moe_sparse(tiles, xs, wg, wu, wd):
    src, valid, ea, eb = tiles
    nrow = xs.shape[0]
    ntile = nrow // MOE_R
    wa_map = lambda t, src, valid, ea, eb: (ea[t], 0, 0)
    wb_map = lambda t, src, valid, ea, eb: (eb[t], 0, 0)
    row_map = lambda t, src, valid, ea, eb: (src[t], 0)
    return pl.pallas_call(
        _moe_sparse_kernel,
        grid_spec=pltpu.PrefetchScalarGridSpec(
            num_scalar_prefetch=4,
            grid=(ntile,),
            in_specs=[
                pl.BlockSpec((MOE_R, D + MOE_EXT), row_map),
                pl.BlockSpec((None, D, EXP_FF), wa_map),
                pl.BlockSpec((None, D, EXP_FF), wa_map),
                pl.BlockSpec((None, EXP_FF, D), wa_map),
                pl.BlockSpec((None, D, EXP_FF), wb_map),
                pl.BlockSpec((None, D, EXP_FF), wb_map),
                pl.BlockSpec((None, EXP_FF, D), wb_map),
            ],
            out_specs=pl.BlockSpec((MOE_R, D), lambda t, src, valid, ea, eb: (t, 0)),
        ),
        out_shape=jax.ShapeDtypeStruct((nrow, D), f32),
        compiler_params=pltpu.CompilerParams(
            dimension_semantics=("arbitrary",), vmem_limit_bytes=VMEM_LIMIT),
        name="moe_sparse",
    )(src, valid, ea, eb, xs, wg, wu, wd, wg, wu, wd)


def _combine_kernel(pos_ref, ys_hbm, x1_ref, mod_ref, fw_ref, x2_ref, buf, sem, *, final):
    tm = x1_ref.shape[0]

    def issue(tb, c):
        for u in range(MOE_DMA_UNROLL):
            t = tb * MOE_DMA_UNROLL + u
            p = pos_ref[0, t]
            pltpu.make_async_copy(ys_hbm.at[pl.ds(p, 1), :], buf.at[pl.ds(t, 1), :], sem).start(priority=u % 2)
        return c

    lax.fori_loop(0, tm // MOE_DMA_UNROLL, issue, 0)

    def drain(t, c):
        pltpu.make_async_copy(ys_hbm.at[pl.ds(0, 1), :], buf.at[pl.ds(0, 1), :], sem).wait()
        return c

    lax.fori_loop(0, tm, drain, 0, unroll=MOE_DMA_UNROLL)
    x2 = x1_ref[...] + mod_ref[5:6, :] * buf[...]
    if final:
        x2 = x2 * lax.rsqrt(jnp.mean(x2 * x2, axis=-1, keepdims=True) + EPS) * fw_ref[...]
    x2_ref[...] = x2


def _combine(l, pos3, ys, x1, mod, fw, *, final):
    nt, _, tm = pos3.shape
    ntok = x1.shape[0]
    per_seq = ntok // mod.shape[1] // tm
    return pl.pallas_call(
        functools.partial(_combine_kernel, final=final),
        grid=(nt,),
        in_specs=[
            pl.BlockSpec((None, 1, tm), lambda i: (i, 0, 0), memory_space=pltpu.SMEM),
            pl.BlockSpec(memory_space=pl.ANY),
            pl.BlockSpec((tm, D), lambda i: (i, 0)),
            pl.BlockSpec((None, None, 6, D), lambda i: (l, i // per_seq, 0, 0)),
            pl.BlockSpec((1, D), lambda i: (0, 0)),
        ],
        out_specs=pl.BlockSpec((tm, D), lambda i: (i, 0)),
        out_shape=jax.ShapeDtypeStruct((ntok, D), f32),
        scratch_shapes=[pltpu.VMEM((tm, D), f32), pltpu.SemaphoreType.DMA(())],
        compiler_params=pltpu.CompilerParams(dimension_semantics=("arbitrary",)),
        name="combine",
    )(pos3, ys, x1, mod, fw)


def _moe_plan(plan, counts, ntile):
    cnt = counts[:N_CLS, 0].astype(jnp.int32)
    tiles_per = (cnt + (MOE_R - 1)) // MOE_R
    tstart = jnp.cumsum(tiles_per) - tiles_per
    total = jnp.sum(tiles_per)
    pos = tstart[plan[0]] * MOE_R + plan[1]
    t = jnp.arange(ntile, dtype=jnp.int32)
    src = jnp.minimum(t, total - 1)
    tcls = jnp.sum((src[:, None] >= tstart[None, :]).astype(jnp.int32), axis=1) - 1
    grp = tcls // N_PAIR
    pair = tcls % N_PAIR
    pa = jnp.asarray(np.array([0, 0, 0, 1, 1, 2], np.int32))
    pb = jnp.asarray(np.array([1, 2, 3, 2, 3, 3], np.int32))
    ea = grp * EXP_PER_GRP + pa[pair]
    eb = grp * EXP_PER_GRP + pb[pair]
    valid = (t < total).astype(jnp.int32)
    return pos, (src, valid, ea, eb)


def _moe_kernel(h2_ref, comb_ref, wg_ref, wu_ref, wd_ref, x1_ref, mod_ref,
                x2_ref, wgb_ref, wub_ref, wdb_ref, acc_scr, *, seq3d):
    e = pl.program_id(1)

    @pl.when(e == 0)
    def _():
        acc_scr[...] = jnp.zeros_like(acc_scr)

    wgb_ref[...] = wg_ref[...].astype(bf16)
    wub_ref[...] = wu_ref[...].astype(bf16)
    wdb_ref[...] = wd_ref[...].astype(bf16)
    h = h2_ref[...]
    a = _dot(h, wgb_ref[...])
    b = _dot(h, wub_ref[...])
    t = (_silu(a) * b).astype(bf16)
    ye = _dot(t, wdb_ref[...])
    lane = lax.broadcasted_iota(jnp.int32, comb_ref.shape, 1)
    w = jnp.sum(jnp.where(lane == e, comb_ref[...], 0.0), axis=-1, keepdims=True)
    acc_scr[...] += w * ye

    @pl.when(e == N_EXP - 1)
    def _():
        x1 = x1_ref[...]
        if seq3d:
            x2_ref[...] = x1 + mod_ref[:, 5:6, :] * acc_scr[...].reshape(x1.shape)
        else:
            x2_ref[...] = x1 + mod_ref[5:6, :] * acc_scr[...]


def _moe(l, h2, comb, wg, wu, wd, x1, mod, *, seq3d):
    if seq3d:
        ns = x1.shape[0]
        ntok = ns * x1.shape[1]
        tm = ntok
        nt = 1
        x_spec = pl.BlockSpec((ns, x1.shape[1], D), lambda i, e: (0, 0, 0))
        mod_spec = pl.BlockSpec((None, ns, 6, D), lambda i, e: (l, 0, 0, 0))
    else:
        ntok = x1.shape[0]
        nseq = mod.shape[1]
        tm = min(1024, ntok // nseq)
        nt = ntok // tm
        per_seq = ntok // nseq // tm
        x_spec = pl.BlockSpec((tm, D), lambda i, e: (i, 0))
        mod_spec = pl.BlockSpec((None, None, 6, D), lambda i, e: (l, i // per_seq, 0, 0))
    assert nt == 1, "the bf16 weight copies are written once per expert"
    row = lambda i, e: (i, 0)
    return pl.pallas_call(
        functools.partial(_moe_kernel, seq3d=seq3d),
        grid=(nt, N_EXP),
        in_specs=[
            pl.BlockSpec((tm, D), row),
            pl.BlockSpec((tm, LANES), row),
            pl.BlockSpec((None, None, D, EXP_FF), lambda i, e: (l, e, 0, 0)),
            pl.BlockSpec((None, None, D, EXP_FF), lambda i, e: (l, e, 0, 0)),
            pl.BlockSpec((None, None, EXP_FF, D), lambda i, e: (l, e, 0, 0)),
            x_spec,
            mod_spec,
        ],
        out_specs=(
            x_spec,
            pl.BlockSpec((None, D, EXP_FF), lambda i, e: (e, 0, 0)),
            pl.BlockSpec((None, D, EXP_FF), lambda i, e: (e, 0, 0)),
            pl.BlockSpec((None, EXP_FF, D), lambda i, e: (e, 0, 0)),
        ),
        out_shape=(
            jax.ShapeDtypeStruct(x1.shape, f32),
            jax.ShapeDtypeStruct((N_EXP, D, EXP_FF), bf16),
            jax.ShapeDtypeStruct((N_EXP, D, EXP_FF), bf16),
            jax.ShapeDtypeStruct((N_EXP, EXP_FF, D), bf16),
        ),
        scratch_shapes=[pltpu.VMEM((tm, D), f32)],
        compiler_params=pltpu.CompilerParams(
            dimension_semantics=("arbitrary", "arbitrary"), vmem_limit_bytes=VMEM_LIMIT),
        name="moe",
    )(h2, comb, wg, wu, wd, x1, mod)


def _final_norm_kernel(x_ref, w_ref, o_ref):
    x = x_ref[...]
    o_ref[...] = x * lax.rsqrt(jnp.mean(x * x, axis=-1, keepdims=True) + EPS) * w_ref[...]


def _final_norm(x2d, w):
    ntok = x2d.shape[0]
    tm = min(1024, ntok)
    return pl.pallas_call(
        _final_norm_kernel,
        grid=(ntok // tm,),
        in_specs=[pl.BlockSpec((tm, D), lambda i: (i, 0)), pl.BlockSpec((1, D), lambda i: (0, 0))],
        out_specs=pl.BlockSpec((tm, D), lambda i: (i, 0)),
        out_shape=jax.ShapeDtypeStruct((ntok, D), f32),
        compiler_params=pltpu.CompilerParams(dimension_semantics=("arbitrary",)),
        name="final_norm",
    )(x2d, w)


def _prep_in_weights(w_in):
    o = 0
    z = w_in[:, :, o:o + SSD_INNER]; o += SSD_INNER
    xbc = w_in[:, :, o:o + CONV_DIM]; o += CONV_DIM
    dt = w_in[:, :, o:o + SSD_NH]; o += SSD_NH
    q = w_in[:, :, o:o + GLA_KD] * (GLA_DK ** -0.5); o += GLA_KD
    kvr = w_in[:, :, o:o + GLA_KD + 2 * GLA_VD]; o += GLA_KD + 2 * GLA_VD
    glr = w_in[:, :, o:o + GLA_RANK]; o += GLA_RANK
    gates = w_in[:, :, o:o + 2 * D]
    main = jnp.concatenate([z, xbc, q, kvr, gates], axis=-1).astype(bf16)
    pad = jnp.zeros(w_in.shape[:2] + (LANES - SSD_NH - GLA_RANK,), w_in.dtype)
    tail = jnp.concatenate([dt, glr, pad], axis=-1).astype(bf16)
    return main, tail


def _pad_lanes(v, width=LANES):
    return jnp.concatenate([v, jnp.zeros(v.shape[:-1] + (width - v.shape[-1],), v.dtype)], axis=-1)


def kernel(x_prompt, x_sample, c_prompt, c_sample, state_conv, state_ssm, state_gla, w_ada, b_ada, norm1_w, w_in, conv_w, conv_b, dt_bias, a_log, d_skip, ssd_norm_w, w_ssd_br, gla_gate_up, gla_gate_b, gla_norm_w, w_gla_br, merge_b, w_out, norm2_w, w_router, router_bias, w_exp_gate, w_exp_up, w_exp_down, final_norm_w):
    bp, sp, _ = x_prompt.shape
    bs, ss, _ = x_sample.shape

    mod = _adaln(jnp.concatenate([c_prompt, c_sample], axis=0), w_ada, b_ada)
    mod_p = mod[:, :bp].reshape(DEPTH, bp, 6, D)
    mod_s = mod[:, bp:].reshape(DEPTH, bs, 6, D)

    w_main, w_tail = _prep_in_weights(w_in)
    gate_up = jnp.concatenate(
        [jnp.zeros((DEPTH, TAIL_GLR0, GLA_KD), f32), gla_gate_up,
         jnp.zeros((DEPTH, LANES - TAIL_GLR0 - GLA_RANK, GLA_KD), f32)], axis=1).astype(bf16)
    in_w = (norm1_w.reshape(DEPTH, 1, D), w_main, w_tail, gate_up, gla_gate_b.reshape(DEPTH, 1, GLA_KD),
            conv_w, conv_b.reshape(DEPTH, 1, CONV_DIM), merge_b.reshape(DEPTH, 1, 2 * D))
    bigsel, e8 = _ssd_consts()
    ssd_w = (_pad_lanes(dt_bias).reshape(DEPTH, 1, LANES), _pad_lanes(a_log).reshape(DEPTH, 1, LANES),
             jnp.repeat(d_skip, SSD_HD, axis=-1).reshape(DEPTH, SSD_NG, 1, SSD_GW),
             ssd_norm_w.reshape(DEPTH, SSD_NG, 1, SSD_GW),
             w_ssd_br.astype(bf16).reshape(DEPTH, SSD_NG, SSD_GW, D), bigsel, e8)
    gla_w = (gla_norm_w.reshape(DEPTH, 1, GLA_DV), w_gla_br.astype(bf16))
    w_out_b = w_out.astype(bf16)
    n2 = norm2_w.reshape(DEPTH, 1, D)
    wr = _pad_lanes(w_router)
    wr_hi = wr.astype(bf16)
    wr_lo = (wr - wr_hi.astype(f32)).astype(bf16)
    rb = router_bias.reshape(N_EXP, 1)

    xp = x_prompt.reshape(bp * sp, D)
    xs = x_sample
    ssm_s = state_ssm.reshape(DEPTH, bs, SSD_NG, SSD_GW, SSD_NS)
    ssd_lp = min(128, sp)
    gla_lp = min(64, sp)
    cp = sp_ = gp = cs_ = ss_ = gs = None
    moe_tm = min(MOE_TM, sp)
    moe_tiles = -(-(bp * sp) // MOE_R) + N_CLS
    xs_rows = jnp.zeros((moe_tiles * MOE_R, D + MOE_EXT), f32)
    fw = final_norm_w.reshape(1, D)
    for l in range(DEPTH):
        z4, xs4, bc, qkvr, gate, tail, lg, cs_ = _inproj(l, xs, mod_s, state_conv, cs_, in_w, seq3d=True)
        bra, ss_ = _ssd(l, xs4, bc, z4, tail, ssm_s, ss_, ssd_w, nseq=bs, L=ss, sb=8, nchunks=1)
        brb, gs = _gla(l, qkvr, lg, state_gla, gs, gla_w, nseq=bs, L=ss, C1=ss, sb=8, nchunks=1)
        x1, h2, comb = _merge(l, xs, bra, brb, gate, mod_s, w_out_b, n2, wr_hi, wr_lo, rb,
                              seq3d=True, sparse=False)
        xs, wg_b, wu_b, wd_b = _moe(l, h2, comb, w_exp_gate, w_exp_up, w_exp_down, x1, mod_s, seq3d=True)

        z4, xs4, bc, qkvr, gate, tail, lg, cp = _inproj(l, xp, mod_p, None, cp, in_w, seq3d=False)
        bra, sp_ = _ssd(l, xs4, bc, z4, tail, None, sp_, ssd_w, nseq=bp, L=ssd_lp, sb=1, nchunks=sp // ssd_lp,
                        cps=4 if (sp // ssd_lp) % 4 == 0 else 1)
        brb, gp = _gla(l, qkvr, lg, None, gp, gla_w, nseq=bp, L=gla_lp, C1=16,
                       sb=2 if bp % 2 == 0 else 1, nchunks=sp // gla_lp,
                       cps=2 if (sp // gla_lp) % 2 == 0 else 1)
        x1, h2x, plan, counts = _merge(l, xp, bra, brb, gate, mod_p, w_out_b, n2, wr_hi, wr_lo, rb,
                                       seq3d=False, sparse=True)
        pos, tiles = _moe_plan(plan, counts, moe_tiles)
        pos3 = pos.reshape(bp * sp // moe_tm, 1, moe_tm)
        xs_rows = _dispatch(pos3, h2x, xs_rows)
        ys_rows = _moe_sparse(tiles, xs_rows, wg_b, wu_b, wd_b)
        xp = _combine(l, pos3, ys_rows, x1, mod_p, fw, final=(l == DEPTH - 1))

    y_prompt = xp.reshape(bp, sp, D)
    y_sample = _final_norm(xs.reshape(bs * ss, D), fw).reshape(bs, ss, D)
    return (y_prompt, y_sample, cp, sp_.reshape(DEPTH, bp, SSD_NH, SSD_HD, SSD_NS), gp,
            cs_, ss_.reshape(DEPTH, bs, SSD_NH, SSD_HD, SSD_NS), gs)
```
